```python
import math
import jax, jax.numpy as jnp
from jax import lax
import numpy as np

D_MODEL = 1024
BATCH = 4
SEQ = 4096
DEPTH = 4
DEC_BATCH = 128
DEC_SEQ = 8
PAST_LEN = 2048
PAGE_SIZE = 128

N_A_LAYERS = DEPTH // 2
N_B_LAYERS = DEPTH - N_A_LAYERS
D_RNN = D_MODEL
LRU_BLOCKS = 8
LRU_BW = D_RNN // LRU_BLOCKS
CONV_W = 4
LRU_C = 8.0
N_HEADS = 16
HEAD_DIM = 64
N_KV_HEADS = 4
GQA = N_HEADS // N_KV_HEADS
CMP_LEN = 32
CMP_STRIDE = 16
CMP_HIDDEN = 128
SLC_LEN = 64
N_SELECT = 16
WINDOW = 512
N_BRANCH = 3
N_KV_STREAMS = 6
Q_BLOCK = 64
N_BUCKETS = 32
MAX_DIST = 128
D_FF = -(-8 * D_MODEL // (3 * 256)) * 256
EPS = 1e-6
NEG = -1e30
FORCE = 1e9

kernel_name = 'hawk_nsa_yoco_decoder_step'


def rms_norm(x, g):
    xf = x.astype(jnp.float32)
    y = xf * lax.rsqrt(jnp.mean(xf * xf, axis=-1, keepdims=True) + EPS)
    return (y * g.astype(jnp.float32)).astype(x.dtype)


def ada(c, w, b, n):
    return jnp.split(jax.nn.silu(c) @ w + b, n, axis=-1)


def modulate(x, shift, scale):
    return x * (1 + scale[:, None, :]) + shift[:, None, :]


def swiglu(h, w_in, w_out):
    gate, up = jnp.split(h @ w_in, 2, axis=-1)
    return (jax.nn.silu(gate) * up) @ w_out


def rglru_block(h, conv_prev, h_prev, w_in, conv_w, conv_b, w_rg_a, b_rg_a, w_rg_x, b_rg_x, lam, w_out):
    bn, s = h.shape[0], h.shape[1]
    xb, yb = jnp.split(h @ w_in, 2, axis=-1)
    xpad = jnp.concatenate([conv_prev, xb], axis=1)
    xc = conv_b + xpad[:, 0:s] * conv_w[0]
    for k in range(1, CONV_W):
        xc = xc + xpad[:, k:k + s] * conv_w[k]
    new_conv = xpad[:, s:]
    xblk = xc.reshape(bn, s, LRU_BLOCKS, LRU_BW)
    r = jax.nn.sigmoid(jnp.einsum('bsnk,nkj->bsnj', xblk, w_rg_a).reshape(bn, s, D_RNN) + b_rg_a)
    i = jax.nn.sigmoid(jnp.einsum('bsnk,nkj->bsnj', xblk, w_rg_x).reshape(bn, s, D_RNN) + b_rg_x)
    log_a = -LRU_C * r * jax.nn.softplus(-lam)
    a = jnp.exp(log_a)
    b = jnp.sqrt(-jnp.expm1(2 * log_a)) * (i * xc)
    b = jnp.concatenate([b[:, :1] + a[:, :1] * h_prev[:, None, :], b[:, 1:]], axis=1)

    def combine(e1, e2):
        a1, b1 = e1
        a2, b2 = e2
        return a1 * a2, a2 * b1 + b2

    _, hs = lax.associative_scan(combine, (a, b), axis=1)
    y = (hs * jax.nn.gelu(yb)) @ w_out
    return y, new_conv, hs[:, -1]


def compress(k, pos, w1, w2):
    bn, t = k.shape[0], k.shape[1]
    n16 = t // CMP_STRIDE
    ratio = CMP_LEN // CMP_STRIDE
    n_cmp = n16 - ratio + 1
    c = k[:, :n16 * CMP_STRIDE].reshape(bn, n16, CMP_STRIDE, N_KV_HEADS, HEAD_DIM)
    blk = jnp.concatenate([c[:, r:r + n_cmp] for r in range(ratio)], axis=2)
    blk = blk + pos[None, None, :, None, :]
    blk = blk.transpose(0, 1, 3, 2, 4).reshape(bn, n_cmp, N_KV_HEADS, CMP_LEN * HEAD_DIM)
    return jax.nn.gelu(blk @ w1) @ w2


def cmp_slc_overlap(n_cmp, n_slc):
    i = np.arange(n_cmp)[:, None] * CMP_STRIDE
    j = np.arange(n_slc)[None, :] * SLC_LEN
    return ((i < j + SLC_LEN) & (i + CMP_LEN > j)).astype(np.float32)


def t5_bucket(dist):
    dist = jnp.maximum(dist, 0)
    max_exact = N_BUCKETS // 2
    scaled = jnp.log(jnp.maximum(dist, 1).astype(jnp.float32) / max_exact) / math.log(MAX_DIST / max_exact)
    large = jnp.minimum(max_exact + (scaled * (N_BUCKETS - max_exact)).astype(jnp.int32), N_BUCKETS - 1)
    return jnp.where(dist < max_exact, dist, large)


def bias_all_heads(table, dist):
    b = table[t5_bucket(dist)].astype(jnp.float32)
    return jnp.moveaxis(b, -1, 0).reshape((N_KV_HEADS, GQA) + dist.shape)


def bias_per_group(table, dist):
    tbl = table.reshape(N_BUCKETS, N_KV_HEADS, GQA).transpose(1, 0, 2)
    b = jax.vmap(lambda tb, bk: tb[bk])(tbl, t5_bucket(dist))
    return jnp.moveaxis(b, -1, 1).astype(jnp.float32)


def nsa_one_seq(q, gates, kc, vc, ks, vs, kw, vw, table, q_pos0, kw_pos0):
    s_len = q.shape[0]
    qb = math.gcd(s_len, Q_BLOCK)
    n_qb = s_len // qb
    n_cmp = kc.shape[0]
    t_k = ks.shape[0]
    n_slc = -(-t_k // SLC_LEN)
    top_n = min(N_SELECT, n_slc)
    pad = n_slc * SLC_LEN - t_k

    def to_blocks(a):
        a = jnp.pad(a, ((0, pad), (0, 0), (0, 0)))
        return a.reshape(n_slc, SLC_LEN, N_KV_HEADS, HEAD_DIM).transpose(2, 0, 1, 3)

    ks_b, vs_b = to_blocks(ks), to_blocks(vs)
    kw_p = jnp.pad(kw, ((WINDOW, 0), (0, 0), (0, 0)))
    vw_p = jnp.pad(vw, ((WINDOW, 0), (0, 0), (0, 0)))
    overlap = jnp.asarray(cmp_slc_overlap(n_cmp, n_slc))
    cmp_end = jnp.arange(n_cmp) * CMP_STRIDE + (CMP_LEN - 1)
    slc_id = jnp.arange(n_slc)
    scale = HEAD_DIM ** -0.5
    q_blocks = q.reshape(n_qb, qb, N_KV_HEADS, GQA, HEAD_DIM)
    g_blocks = gates.reshape(n_qb, qb, N_KV_HEADS, GQA, N_BRANCH)

    def one_block(args):
        qi, qq, gg = args
        t0 = q_pos0 + qi * qb
        t = t0 + jnp.arange(qb)
        s_c = jnp.einsum('qhgd,chd->hgqc', qq, kc).astype(jnp.float32) * scale
        dist_c = t[:, None] - cmp_end[None, :]
        valid_c = dist_c >= 0
        p_c = jax.nn.softmax(jnp.where(valid_c, s_c + bias_all_heads(table, dist_c), NEG), axis=-1)
        p_c = p_c * jnp.any(valid_c, axis=-1)[:, None]
        o_cmp = jnp.einsum('hgqc,chd->qhgd', p_c.astype(vc.dtype), vc)
        imp = jnp.einsum('hgqc,cj->hqj', p_c, overlap)
        cur = (t // SLC_LEN)[:, None]
        forced = (slc_id == 0) | (slc_id == cur) | (slc_id == cur - 1)
        imp = jnp.where(forced, FORCE, jnp.where(slc_id > cur, NEG, imp))
        _, idx = lax.top_k(imp, top_n)
        kb = jax.vmap(lambda a, i: a[i])(ks_b, idx)
        vb = jax.vmap(lambda a, i: a[i])(vs_b, idx)
        key_pos = idx[..., None] * SLC_LEN + jnp.arange(SLC_LEN)
        dist_s = t[None, :, None, None] - key_pos
        s_s = jnp.einsum('qhgd,hqnld->hgqnl', qq, kb).astype(jnp.float32) * scale + bias_per_group(table, dist_s)
        s_s = jnp.where((dist_s >= 0)[:, None], s_s, NEG).reshape(N_KV_HEADS, GQA, qb, top_n * SLC_LEN)
        p_s = jax.nn.softmax(s_s, axis=-1).reshape(N_KV_HEADS, GQA, qb, top_n, SLC_LEN)
        o_slc = jnp.einsum('hgqnl,hqnld->qhgd', p_s.astype(vb.dtype), vb)
        start = t0 - kw_pos0
        kwin = lax.dynamic_slice_in_dim(kw_p, start, WINDOW + qb, axis=0)
        vwin = lax.dynamic_slice_in_dim(vw_p, start, WINDOW + qb, axis=0)
        wpos = t0 - WINDOW + jnp.arange(WINDOW + qb)
        dist_w = t[:, None] - wpos[None, :]
        valid_w = (dist_w >= 0) & (dist_w < WINDOW) & (wpos[None, :] >= kw_pos0)
        s_w = jnp.einsum('qhgd,khd->hgqk', qq, kwin).astype(jnp.float32) * scale + bias_all_heads(table, dist_w)
        p_w = jax.nn.softmax(jnp.where(valid_w, s_w, NEG), axis=-1)
        o_win = jnp.einsum('hgqk,khd->qhgd', p_w.astype(vwin.dtype), vwin)
        return gg[..., 0:1] * o_cmp + gg[..., 1:2] * o_slc + gg[..., 2:3] * o_win

    out = lax.map(one_block, (jnp.arange(n_qb), q_blocks, g_blocks))
    return out.reshape(s_len, N_HEADS * HEAD_DIM)


def nsa_layer(h, w_in, w_out, kc, vc, ks, vs, kw, vw, table, q_pos0, kw_pos0):
    bn, s = h.shape[0], h.shape[1]
    u = h @ w_in
    q = u[..., :N_HEADS * HEAD_DIM].reshape(bn, s, N_HEADS, HEAD_DIM)
    gates = jax.nn.sigmoid(u[..., N_HEADS * HEAD_DIM:]).reshape(bn, s, N_HEADS, N_BRANCH)
    per_seq = lambda a: nsa_one_seq(a[0], a[1], a[2], a[3], a[4], a[5], a[6], a[7], table, q_pos0, kw_pos0)
    o = lax.map(per_seq, (q, gates, kc, vc, ks, vs, kw, vw))
    return o @ w_out


def forward(x, c, conv_state, h_state, past_kv, past_win, q_pos0,
            w_ada, b_ada, norm_g, w_in_a, conv_w, conv_b, w_rg_a, b_rg_a, w_rg_x, b_rg_x, lru_lambda, w_out_a,
            w_ada_kv, b_ada_kv, norm_kv, w_kv, cmp_pos, cmp_w1, cmp_w2, w_in_b, w_out_b, rel_bias,
            w_ffn_in, w_ffn_out):
    bn, s = x.shape[0], x.shape[1]
    new_conv, new_h = [], []
    kv_rows = win_state = None
    kc = vc = ks = vs = kw = vw = None
    kw_pos0 = 0
    for l in range(DEPTH):
        sh1, sc1, g1, sh2, sc2, g2 = ada(c, w_ada[l], b_ada[l], 6)
        h = modulate(rms_norm(x, norm_g[l, 0]), sh1, sc1)
        if l < N_A_LAYERS:
            y, cs, hs = rglru_block(h, conv_state[l], h_state[l], w_in_a[l], conv_w[l], conv_b[l],
                                    w_rg_a[l], b_rg_a[l], w_rg_x[l], b_rg_x[l], lru_lambda[l], w_out_a[l])
            new_conv.append(cs)
            new_h.append(hs)
        else:
            if l == N_A_LAYERS:
                sh_kv, sc_kv = ada(c, w_ada_kv, b_ada_kv, 2)
                hk = modulate(rms_norm(x, norm_kv), sh_kv, sc_kv)
                kv = (hk @ w_kv).reshape(bn, s, N_KV_STREAMS, N_KV_HEADS, HEAD_DIM)
                kv_rows = kv[:, :, :4]
                if past_kv is None:
                    full = kv_rows
                    win = kv[:, :, 4:]
                    kw_pos0 = 0
                    win_state = win[:, -min(WINDOW, s):]
                else:
                    full = jnp.concatenate([past_kv, kv_rows], axis=1)
                    win = jnp.concatenate([past_win, kv[:, :, 4:]], axis=1)
                    kw_pos0 = q_pos0 - past_win.shape[1]
                    win_state = win[:, -past_win.shape[1]:]
                kc = compress(full[:, :, 0], cmp_pos[0], cmp_w1[0], cmp_w2[0])
                vc = compress(full[:, :, 1], cmp_pos[1], cmp_w1[1], cmp_w2[1])
                ks, vs = full[:, :, 2], full[:, :, 3]
                kw, vw = win[:, :, 0], win[:, :, 1]
            lb = l - N_A_LAYERS
            y = nsa_layer(h, w_in_b[lb], w_out_b[lb], kc, vc, ks, vs, kw, vw, rel_bias, q_pos0, kw_pos0)
        x = x + g1[:, None, :] * rms_norm(y, norm_g[l, 1])
        h = modulate(rms_norm(x, norm_g[l, 2]), sh2, sc2)
        x = x + g2[:, None, :] * rms_norm(swiglu(h, w_ffn_in[l], w_ffn_out[l]), norm_g[l, 3])
    return x, kv_rows, win_state, jnp.stack(new_conv), jnp.stack(new_h)


def setup_inputs(seed: int = 0) -> dict:
    key = jax.random.key(seed)
    keys = iter(jax.random.split(key, 48))

    def nrm(shape, scale):
        return jax.random.normal(next(keys), shape, jnp.float32) * scale

    n_pages = PAST_LEN // PAGE_SIZE
    n_used = DEC_BATCH * n_pages
    n_pool = (5 * n_used) // 4
    win_buf = min(WINDOW, PAST_LEN)
    lru_a = jax.random.uniform(next(keys), (N_A_LAYERS, D_RNN), jnp.float32, 0.9, 0.999)
    page_table = jax.random.permutation(next(keys), n_pool)[:n_used].reshape(DEC_BATCH, n_pages).astype(jnp.int32)
    return {
        'x_prompt': nrm((BATCH, SEQ, D_MODEL), 1.0),
        'x_sample': nrm((DEC_BATCH, DEC_SEQ, D_MODEL), 1.0),
        'c_prompt': nrm((BATCH, D_MODEL), 1.0),
        'c_sample': nrm((DEC_BATCH, D_MODEL), 1.0),
        'cache_kv': nrm((n_pool, PAGE_SIZE, 4, N_KV_HEADS, HEAD_DIM), 1.0),
        'cache_win': nrm((DEC_BATCH, win_buf, 2, N_KV_HEADS, HEAD_DIM), 1.0),
        'state_conv': nrm((N_A_LAYERS, DEC_BATCH, CONV_W - 1, D_RNN), 0.5),
        'state_h': nrm((N_A_LAYERS, DEC_BATCH, D_RNN), 0.5),
        'page_table': page_table,
        'w_ada': nrm((DEPTH, D_MODEL, 6 * D_MODEL), 0.5 * D_MODEL ** -0.5),
        'b_ada': nrm((DEPTH, 6 * D_MODEL), 0.01),
        'norm_g': 1.0 + nrm((DEPTH, 4, D_MODEL), 0.05),
        'w_in_a': nrm((N_A_LAYERS, D_MODEL, 2 * D_RNN), D_MODEL ** -0.5),
        'conv_w': nrm((N_A_LAYERS, CONV_W, D_RNN), CONV_W ** -0.5),
        'conv_b': nrm((N_A_LAYERS, D_RNN), 0.01),
        'w_rg_a': nrm((N_A_LAYERS, LRU_BLOCKS, LRU_BW, LRU_BW), LRU_BW ** -0.5),
        'b_rg_a': nrm((N_A_LAYERS, D_RNN), 0.01),
        'w_rg_x': nrm((N_A_LAYERS, LRU_BLOCKS, LRU_BW, LRU_BW), LRU_BW ** -0.5),
        'b_rg_x': nrm((N_A_LAYERS, D_RNN), 0.01),
        'lru_lambda': jnp.log(lru_a) - jnp.log1p(-lru_a),
        'w_out_a': nrm((N_A_LAYERS, D_RNN, D_MODEL), D_RNN ** -0.5),
        'w_ada_kv': nrm((D_MODEL, 2 * D_MODEL), 0.5 * D_MODEL ** -0.5),
        'b_ada_kv': nrm((2 * D_MODEL,), 0.01),
        'norm_kv': 1.0 + nrm((D_MODEL,), 0.05),
        'w_kv': nrm((D_MODEL, N_KV_STREAMS * N_KV_HEADS * HEAD_DIM), D_MODEL ** -0.5),
        'cmp_pos': nrm((2, CMP_LEN, HEAD_DIM), 0.1),
        'cmp_w1': nrm((2, CMP_LEN * HEAD_DIM, CMP_HIDDEN), (CMP_LEN * HEAD_DIM) ** -0.5),
        'cmp_w2': nrm((2, CMP_HIDDEN, HEAD_DIM), CMP_HIDDEN ** -0.5),
        'w_in_b': nrm((N_B_LAYERS, D_MODEL, N_HEADS * HEAD_DIM + N_BRANCH * N_HEADS), D_MODEL ** -0.5),
        'w_out_b': nrm((N_B_LAYERS, N_HEADS * HEAD_DIM, D_MODEL), (N_HEADS * HEAD_DIM) ** -0.5),
        'rel_bias': nrm((N_BUCKETS, N_HEADS), 0.5),
        'w_ffn_in': nrm((DEPTH, D_MODEL, 2 * D_FF), D_MODEL ** -0.5),
        'w_ffn_out': nrm((DEPTH, D_FF, D_MODEL), D_FF ** -0.5),
    }


def reference(x_prompt, x_sample, c_prompt, c_sample, cache_kv, cache_win, state_conv, state_h, page_table,
              w_ada, b_ada, norm_g, w_in_a, conv_w, conv_b, w_rg_a, b_rg_a, w_rg_x, b_rg_x, lru_lambda, w_out_a,
              w_ada_kv, b_ada_kv, norm_kv, w_kv, cmp_pos, cmp_w1, cmp_w2, w_in_b, w_out_b, rel_bias,
              w_ffn_in, w_ffn_out):
    weights = (w_ada, b_ada, norm_g, w_in_a, conv_w, conv_b, w_rg_a, b_rg_a, w_rg_x, b_rg_x, lru_lambda, w_out_a,
               w_ada_kv, b_ada_kv, norm_kv, w_kv, cmp_pos, cmp_w1, cmp_w2, w_in_b, w_out_b, rel_bias,
               w_ffn_in, w_ffn_out)
    bp = x_prompt.shape[0]
    zero_conv = jnp.zeros((N_A_LAYERS, bp, CONV_W - 1, D_RNN), x_prompt.dtype)
    zero_h = jnp.zeros((N_A_LAYERS, bp, D_RNN), x_prompt.dtype)
    y_prompt, kv_prompt, win_prompt, conv_prompt, h_prompt = forward(
        x_prompt, c_prompt, zero_conv, zero_h, None, None, 0, *weights)
    bs = x_sample.shape[0]
    past_len = page_table.shape[1] * cache_kv.shape[1]
    past_kv = cache_kv[page_table].reshape(bs, past_len, 4, N_KV_HEADS, HEAD_DIM)
    y_sample, kv_sample, win_sample, conv_sample, h_sample = forward(
        x_sample, c_sample, state_conv, state_h, past_kv, cache_win, past_len, *weights)
    return (y_prompt, y_sample, kv_prompt, kv_sample, win_prompt, win_sample, conv_prompt, conv_sample, h_prompt, h_sample)
```

```python
import functools
import math

import numpy as np
import jax
import jax.numpy as jnp
from jax import lax
from jax.experimental import pallas as pl
from jax.experimental.pallas import tpu as pltpu

D_MODEL = 1024
DEPTH = 4
N_A_LAYERS = 2
D_RNN = 1024
LRU_BLOCKS = 8
LRU_BW = 128
CONV_W = 4
LRU_C = 8.0
N_HEADS = 16
HEAD_DIM = 64
N_KV_HEADS = 4
GQA = 4
CMP_LEN = 32
CMP_STRIDE = 16
CMP_HIDDEN = 128
SLC_LEN = 64
N_SELECT = 16
WINDOW = 512
Q_BLOCK = 64
N_BUCKETS = 32
MAX_DIST = 128
D_FF = 2816
EPS = 1e-6
NEG = -1e30
FORCE = 1e9

LANES = 128
PAIR_W = 2 * HEAD_DIM
N_PAIRS = N_KV_HEADS // 2
AUG_BIAS_HI = 64
AUG_BIAS_LO = 65
AUG_PAD = 66
N_BLOCK_LANES = 64
VMEM_LIMIT = 56 * 1024 * 1024

F32 = jnp.float32
BF16 = jnp.bfloat16


def _cparams(*sem):
    return pltpu.CompilerParams(dimension_semantics=sem, vmem_limit_bytes=VMEM_LIMIT)


def _dot(a, b):
    return jnp.dot(a, b, preferred_element_type=F32)


def _dot_nt(a, b):
    return lax.dot_general(a, b, (((1,), (1,)), ((), ())), preferred_element_type=F32)


def _gelu_tanh(x):
    return 0.5 * x * (1.0 + jnp.tanh(math.sqrt(2.0 / math.pi) * (x + 0.044715 * (x * x * x))))


def _rms(x, gain):
    return x * lax.rsqrt(jnp.mean(x * x, axis=-1, keepdims=True) + EPS) * gain


def _rows(vec_ref, bb, ts):
    v = vec_ref[...]
    d = v.shape[-1]
    return jnp.broadcast_to(v, (bb, ts, d)).reshape(bb * ts, d)


def _vec_spec(bb, tiles_per_seq, col, ngrid):
    if ngrid == 1:
        return pl.BlockSpec((bb, 1, D_MODEL), lambda i: (i // tiles_per_seq, 0, col))
    return pl.BlockSpec((bb, 1, D_MODEL), lambda j, i: (i // tiles_per_seq, 0, col))


def _row_tiling(bn, s, tm=512):
    if s >= tm:
        assert s % tm == 0
        return tm, 1, tm, s // tm
    assert tm % s == 0 and s % 8 == 0
    bb = min(bn, tm // s)
    assert bn % bb == 0
    return bb * s, bb, s, 1


def _mm_body(*refs, act, has_bias):
    if has_bias:
        a_ref, w_ref, b_ref, o_ref = refs
    else:
        a_ref, w_ref, o_ref = refs
    a = a_ref[...]
    if act == "silu":
        a = a.astype(F32)
        a = a * jax.nn.sigmoid(a)
    y = _dot(a.astype(BF16), w_ref[...])
    if has_bias:
        y = y + b_ref[...]
    o_ref[...] = y.astype(o_ref.dtype)


def mm_plain(a, w, bias=None, act=None, tm=512, tn=None, out_dtype=F32):
    m, k = a.shape
    n = w.shape[1]
    tm = min(tm, m)
    tn = tn or n
    assert m % tm == 0 and n % tn == 0
    in_specs = [pl.BlockSpec((tm, k), lambda j, i: (i, 0)), pl.BlockSpec((k, tn), lambda j, i: (0, j))]
    args = [a, w]
    if bias is not None:
        in_specs.append(pl.BlockSpec((1, tn), lambda j, i: (0, j)))
        args.append(bias.reshape(1, n))
    return pl.pallas_call(
        functools.partial(_mm_body, act=act, has_bias=bias is not None),
        grid=(n // tn, m // tm),
        in_specs=in_specs,
        out_specs=pl.BlockSpec((tm, tn), lambda j, i: (i, j)),
        out_shape=jax.ShapeDtypeStruct((m, n), out_dtype),
        compiler_params=_cparams("arbitrary", "arbitrary"),
        name="mm_plain",
    )(*args)


def _swiglu_body(h_ref, wg_ref, wu_ref, o_ref):
    h = h_ref[...]
    g = _dot(h, wg_ref[...])
    u = _dot(h, wu_ref[...])
    o_ref[...] = (g * jax.nn.sigmoid(g) * u).astype(o_ref.dtype)


def mm_swiglu(h, w_in, tm=512, tn=1408):
    m, k = h.shape
    nf = w_in.shape[1] // 2
    tm = min(tm, m)
    assert nf % tn == 0 and m % tm == 0
    nj = nf // tn
    return pl.pallas_call(
        _swiglu_body,
        grid=(nj, m // tm),
        in_specs=[pl.BlockSpec((tm, k), lambda j, i: (i, 0)),
                  pl.BlockSpec((k, tn), lambda j, i: (0, j)),
                  pl.BlockSpec((k, tn), lambda j, i: (0, j + nj))],
        out_specs=pl.BlockSpec((tm, tn), lambda j, i: (i, j)),
        out_shape=jax.ShapeDtypeStruct((m, nf), BF16),
        compiler_params=_cparams("arbitrary", "arbitrary"),
        name="mm_swiglu",
    )(h, w_in, w_in)


def _normmod_body(x_ref, g_ref, sh_ref, sc_ref, o_ref, *, bb, ts):
    y = _rms(x_ref[...], g_ref[...])
    o_ref[...] = (y * (1.0 + _rows(sc_ref, bb, ts)) + _rows(sh_ref, bb, ts)).astype(o_ref.dtype)


def normmod(x, gain, ada, col_shift, col_scale, bn, s):
    m = x.shape[0]
    tm, bb, ts, tps = _row_tiling(bn, s)
    return pl.pallas_call(
        functools.partial(_normmod_body, bb=bb, ts=ts),
        grid=(m // tm,),
        in_specs=[pl.BlockSpec((tm, D_MODEL), lambda i: (i, 0)),
                  pl.BlockSpec((1, D_MODEL), lambda i: (0, 0)),
                  _vec_spec(bb, tps, col_shift, 1),
                  _vec_spec(bb, tps, col_scale, 1)],
        out_specs=pl.BlockSpec((tm, D_MODEL), lambda i: (i, 0)),
        out_shape=jax.ShapeDtypeStruct((m, D_MODEL), BF16),
        compiler_params=_cparams("arbitrary"),
        name="normmod",
    )(x, gain.reshape(1, D_MODEL), ada, ada)


def _resnorm_body(*refs, bb, ts, n_heads):
    a_ref, w_ref, x_ref, gate_ref, ngain_ref = refs[:5]
    head_refs = refs[5:5 + 3 * n_heads]
    xo_ref = refs[5 + 3 * n_heads]
    ho_refs = refs[6 + 3 * n_heads:]
    y = _dot(a_ref[...].astype(BF16), w_ref[...])
    x = x_ref[...] + _rows(gate_ref, bb, ts) * _rms(y, ngain_ref[...])
    xo_ref[...] = x
    for k in range(n_heads):
        hg_ref, sh_ref, sc_ref = head_refs[3 * k:3 * k + 3]
        h = _rms(x, hg_ref[...]) * (1.0 + _rows(sc_ref, bb, ts)) + _rows(sh_ref, bb, ts)
        ho_refs[k][...] = h.astype(ho_refs[k].dtype)


def mm_resnorm(a, w, x, gate, ngain, heads, bn, s):
    m, k = a.shape
    tm, bb, ts, tps = _row_tiling(bn, s)
    row = lambda i: (i, 0)
    const = lambda i: (0, 0)
    in_specs = [pl.BlockSpec((tm, k), row), pl.BlockSpec((k, D_MODEL), const),
                pl.BlockSpec((tm, D_MODEL), row), _vec_spec(bb, tps, gate[1], 1),
                pl.BlockSpec((1, D_MODEL), const)]
    args = [a, w, x, gate[0], ngain.reshape(1, D_MODEL)]
    for hg, sh_arr, sh_col, sc_arr, sc_col in heads:
        in_specs += [pl.BlockSpec((1, D_MODEL), const), _vec_spec(bb, tps, sh_col, 1), _vec_spec(bb, tps, sc_col, 1)]
        args += [hg.reshape(1, D_MODEL), sh_arr, sc_arr]
    out_shape = [jax.ShapeDtypeStruct((m, D_MODEL), F32)] + [jax.ShapeDtypeStruct((m, D_MODEL), BF16)] * len(heads)
    out_specs = [pl.BlockSpec((tm, D_MODEL), row)] * (1 + len(heads))
    outs = pl.pallas_call(
        functools.partial(_resnorm_body, bb=bb, ts=ts, n_heads=len(heads)),
        grid=(m // tm,),
        in_specs=in_specs,
        out_specs=out_specs,
        out_shape=out_shape,
        compiler_params=_cparams("arbitrary"),
        name="mm_resnorm",
    )(*args)
    return outs[0], list(outs[1:])


def _rglru_body(u_ref, conv8_ref, hprev_ref, cw_ref, cb_ref, wa_ref, ba_ref, wx_ref, bx_ref, lam_ref,
                y_ref, convo_ref, ho_ref, carry_h, carry_conv, *, bb, tt):
    j = pl.program_id(1)

    @pl.when(j == 0)
    def _():
        carry_h[...] = hprev_ref[...]
        carry_conv[...] = conv8_ref[...]

    u = u_ref[...]
    xb = u[:, :, :D_RNN]
    yb = u[:, :, D_RNN:]
    ext = jnp.concatenate([carry_conv[...], xb], axis=1)
    cw = cw_ref[...]
    xc = cb_ref[...] + ext[:, 5:5 + tt] * cw[0:1]
    for k in range(1, CONV_W):
        xc = xc + ext[:, 5 + k:5 + k + tt] * cw[k:k + 1]
    last8 = xb[:, tt - 8:tt]
    carry_conv[...] = last8
    convo_ref[...] = last8

    xc2 = xc.reshape(bb * tt, D_RNN)
    xcb = xc2.astype(BF16)

    def gate(w_ref, b_ref):
        cols = [_dot(xcb[:, n * LRU_BW:(n + 1) * LRU_BW], w_ref[n]) for n in range(LRU_BLOCKS)]
        return jax.nn.sigmoid(jnp.concatenate(cols, axis=1) + b_ref[...])

    r = gate(wa_ref, ba_ref)
    ig = gate(wx_ref, bx_ref)
    nl = -lam_ref[...]
    z = jnp.exp(-jnp.abs(nl))
    w1 = 1.0 + z
    log1p_z = jnp.where(w1 == 1.0, z, jnp.log(w1) * (z / jnp.where(w1 == 1.0, 1.0, w1 - 1.0)))
    softplus = jnp.maximum(nl, 0.0) + log1p_z
    log_a = -LRU_C * r * softplus
    a = jnp.exp(log_a)
    b = jnp.sqrt(1.0 - a * a) * (ig * xc2)

    row8 = lax.broadcasted_iota(jnp.int32, (bb * tt, D_RNN), 0) % 8
    for d in (1, 2, 4):
        a_s = pltpu.roll(a, d, 0)
        b_s = pltpu.roll(b, d, 0)
        keep = row8 >= d
        b = jnp.where(keep, a * b_s + b, b)
        a = jnp.where(keep, a * a_s, a)

    a3 = a.reshape(bb, tt, D_RNN)
    b3 = b.reshape(bb, tt, D_RNN)
    carry = carry_h[...]
    groups = []
    for g in range(tt // 8):
        hg = a3[:, 8 * g:8 * g + 8] * carry + b3[:, 8 * g:8 * g + 8]
        carry = hg[:, 7:8]
        groups.append(hg)
    hs = groups[0] if len(groups) == 1 else jnp.concatenate(groups, axis=1)
    carry_h[...] = carry
    ho_ref[...] = carry
    y_ref[...] = (hs * _gelu_tanh(yb)).astype(y_ref.dtype)


def rglru(u, conv8, hprev, cw, cb, wa, ba, wx, bx, lam):
    bn, s, _ = u.shape
    if s >= 256:
        bb, tt = 1, 256
    else:
        bb, tt = min(bn, 512 // s), s
    assert s % tt == 0 and bn % bb == 0 and tt % 8 == 0
    vec = lambda a: a.reshape(1, D_RNN)
    c2 = lambda i, j: (0, 0)
    c3 = lambda i, j: (0, 0, 0)
    return pl.pallas_call(
        functools.partial(_rglru_body, bb=bb, tt=tt),
        grid=(bn // bb, s // tt),
        in_specs=[pl.BlockSpec((bb, tt, 2 * D_RNN), lambda i, j: (i, j, 0)),
                  pl.BlockSpec((bb, 8, D_RNN), lambda i, j: (i, 0, 0)),
                  pl.BlockSpec((bb, 1, D_RNN), lambda i, j: (i, 0, 0)),
                  pl.BlockSpec((CONV_W, D_RNN), c2),
                  pl.BlockSpec((1, D_RNN), c2),
                  pl.BlockSpec((LRU_BLOCKS, LRU_BW, LRU_BW), c3),
                  pl.BlockSpec((1, D_RNN), c2),
                  pl.BlockSpec((LRU_BLOCKS, LRU_BW, LRU_BW), c3),
                  pl.BlockSpec((1, D_RNN), c2),
                  pl.BlockSpec((1, D_RNN), c2)],
        out_specs=[pl.BlockSpec((bb, tt, D_RNN), lambda i, j: (i, j, 0)),
                   pl.BlockSpec((bb, 8, D_RNN), lambda i, j: (i, 0, 0)),
                   pl.BlockSpec((bb, 1, D_RNN), lambda i, j: (i, 0, 0))],
        out_shape=[jax.ShapeDtypeStruct((bn, s, D_RNN), BF16),
                   jax.ShapeDtypeStruct((bn, 8, D_RNN), F32),
                   jax.ShapeDtypeStruct((bn, 1, D_RNN), F32)],
        scratch_shapes=[pltpu.VMEM((bb, 1, D_RNN), F32), pltpu.VMEM((bb, 8, D_RNN), F32)],
        compiler_params=_cparams("arbitrary", "arbitrary"),
        name="rglru",
    )(u, conv8, hprev, cw, vec(cb), wa, vec(ba), wx, vec(bx), vec(lam))


def _bias_body(tb_ref, d_ref, o_ref):
    h = pl.program_id(0)
    d = d_ref[...]
    dc = jnp.maximum(d, 0)
    max_exact = N_BUCKETS // 2
    scaled = jnp.log(jnp.maximum(dc, 1).astype(F32) / max_exact) / math.log(MAX_DIST / max_exact)
    large = jnp.minimum(max_exact + (scaled * (N_BUCKETS - max_exact)).astype(jnp.int32), N_BUCKETS - 1)
    bucket = jnp.where(dc < max_exact, dc, large)
    acc = jnp.full(d.shape, tb_ref[0, h], F32)
    for k in range(1, N_BUCKETS):
        acc = jnp.where(bucket == k, tb_ref[k, h], acc)
    o_ref[...] = jnp.where(d < 0, NEG, acc)


def rel_bias_tiles(table, dist):
    rows, cols = dist.shape
    return pl.pallas_call(
        _bias_body,
        grid=(N_HEADS,),
        in_specs=[pl.BlockSpec(memory_space=pltpu.SMEM), pl.BlockSpec((rows, cols), lambda h: (0, 0))],
        out_specs=pl.BlockSpec((None, rows, cols), lambda h: (h, 0, 0)),
        out_shape=jax.ShapeDtypeStruct((N_HEADS, rows, cols), F32),
        compiler_params=_cparams("arbitrary"),
        name="rel_bias_tiles",
    )(table, dist)


def _reorder_emit(sources, stage, poslo_ref, poshi_ref, lo_ref, hi_ref):
    n_col = stage.shape[0]
    row = 0
    for src in sources:
        n = src.shape[0]
        for c in range(n_col):
            stage[c, row:row + n, :] = src[:, c * LANES:(c + 1) * LANES]
        row += n
    n16 = row // CMP_STRIDE
    for r in range(CMP_STRIDE):
        x = jnp.concatenate([stage[c, pl.ds(r, n16, stride=CMP_STRIDE), :] for c in range(n_col)], axis=1)
        lo_ref[r] = (x + poslo_ref[r:r + 1]).astype(lo_ref.dtype)
        hi_ref[r] = (x + poshi_ref[r:r + 1]).astype(hi_ref.dtype)


def _reorder_dense_body(x_ref, poslo_ref, poshi_ref, lo_ref, hi_ref, stage):
    _reorder_emit([x_ref], stage, poslo_ref, poshi_ref, lo_ref, hi_ref)


def _reorder_paged_body(pt_ref, *refs, n_pages):
    del pt_ref
    pages = refs[:n_pages]
    poslo_ref, poshi_ref, lo_ref, hi_ref, stage = refs[n_pages:]
    _reorder_emit(pages, stage, poslo_ref, poshi_ref, lo_ref, hi_ref)


def reorder_dense(kv, poslo, poshi):
    bn, s, _ = kv.shape
    n16 = s // CMP_STRIDE
    shp = jax.ShapeDtypeStruct((bn, CMP_STRIDE, n16, 512), BF16)
    ospec = pl.BlockSpec((None, CMP_STRIDE, n16, 512), lambda b: (b, 0, 0, 0))
    pspec = pl.BlockSpec((CMP_STRIDE, 512), lambda b: (0, 0))
    return pl.pallas_call(
        _reorder_dense_body,
        grid=(bn,),
        in_specs=[pl.BlockSpec((None, s, 512), lambda b: (b, 0, 0)), pspec, pspec],
        out_specs=[ospec, ospec],
        out_shape=[shp, shp],
        scratch_shapes=[pltpu.VMEM((512 // LANES, s, LANES), F32)],
        compiler_params=_cparams("arbitrary"),
        name="reorder_dense",
    )(kv, poslo, poshi)


def reorder_paged(cache, page_table, poslo, poshi):
    bn, n_pages = page_table.shape
    page = cache.shape[1]
    per_page = page // CMP_STRIDE
    n16 = n_pages * per_page
    shp = jax.ShapeDtypeStruct((bn, CMP_STRIDE, n16, 512), BF16)
    ospec = pl.BlockSpec((None, CMP_STRIDE, n16, 512), lambda b, pt: (b, 0, 0, 0))
    pspec = pl.BlockSpec((CMP_STRIDE, 512), lambda b, pt: (0, 0))
    page_specs = [pl.BlockSpec((None, page, 512), functools.partial(lambda b, pt, k: (pt[b * n_pages + k], 0, 0), k=k))
                  for k in range(n_pages)]
    return pl.pallas_call(
        functools.partial(_reorder_paged_body, n_pages=n_pages),
        grid_spec=pltpu.PrefetchScalarGridSpec(
            num_scalar_prefetch=1, grid=(bn,),
            in_specs=page_specs + [pspec, pspec],
            out_specs=[ospec, ospec],
            scratch_shapes=[pltpu.VMEM((512 // LANES, n_pages * page, LANES), F32)]),
        out_shape=[shp, shp],
        compiler_params=_cparams("arbitrary"),
        name="reorder_paged",
    )(page_table.reshape(-1), *([cache] * n_pages), poslo, poshi)


def _compress_body(lo_ref, hi_ref, wlo_ref, whi_ref, w2_ref, o_ref, acc1, acc2, *, rows):
    r = pl.program_id(2)

    @pl.when(r == 0)
    def _():
        acc1[...] = jnp.zeros_like(acc1)
        acc2[...] = jnp.zeros_like(acc2)

    acc1[...] += _dot(lo_ref[...].reshape(rows, 256), wlo_ref[...])
    acc2[...] += _dot(hi_ref[...].reshape(rows, 256), whi_ref[...])

    @pl.when(r == CMP_STRIDE - 1)
    def _():
        hidden = _gelu_tanh(acc1[...] + pltpu.roll(acc2[...], rows - 1, 0))
        o_ref[...] = _dot(hidden.astype(BF16), w2_ref[...]).reshape(o_ref.shape).astype(o_ref.dtype)


def compress(xlo, xhi, wlo, whi, w2bd, sb):
    bn, _, n16, _ = xlo.shape
    assert bn % sb == 0
    rows = sb * n16
    xspec = pl.BlockSpec((sb, None, n16, 256), lambda g, s, r: (g, r, 0, s))
    wspec = pl.BlockSpec((None, None, 256, 512), lambda g, s, r: (s, r, 0, 0))
    return pl.pallas_call(
        functools.partial(_compress_body, rows=rows),
        grid=(bn // sb, 2, CMP_STRIDE),
        in_specs=[xspec, xspec, wspec, wspec, pl.BlockSpec((None, 512, 256), lambda g, s, r: (s, 0, 0))],
        out_specs=pl.BlockSpec((sb, n16, 256), lambda g, s, r: (g, 0, s)),
        out_shape=jax.ShapeDtypeStruct((bn, n16, 512), BF16),
        scratch_shapes=[pltpu.VMEM((rows, 512), F32), pltpu.VMEM((rows, 512), F32)],
        compiler_params=_cparams("arbitrary", "arbitrary", "arbitrary"),
        name="compress",
    )(xlo, xhi, wlo, whi, w2bd)


def _assemble_body(pt_ref, *refs, n_pages, page, s_new, win_len):
    del pt_ref
    pages = refs[:n_pages]
    new_slc_ref, new_win_ref, cwin_ref, slc_ref, win_ref = refs[n_pages:]
    for k in range(n_pages):
        slc_ref[k * page:(k + 1) * page, :] = pages[k][...].astype(slc_ref.dtype)
    pad = jnp.zeros((SLC_LEN - s_new, 512), F32)
    slc_ref[n_pages * page:n_pages * page + SLC_LEN, :] = jnp.concatenate(
        [new_slc_ref[...], pad], axis=0).astype(slc_ref.dtype)
    win_ref[0:win_len, :] = cwin_ref[...].astype(win_ref.dtype)
    win_ref[win_len:win_len + SLC_LEN, :] = jnp.concatenate([new_win_ref[...], pad], axis=0).astype(win_ref.dtype)


def assemble_sample_kv(cache, page_table, kv_new, cache_win):
    bn, n_pages = page_table.shape
    page = cache.shape[1]
    s_new = kv_new.shape[1]
    win_len = cache_win.shape[1]
    past = n_pages * page
    page_specs = [pl.BlockSpec((None, page, 512), functools.partial(lambda b, pt, k: (pt[b * n_pages + k], 0, 1), k=k))
                  for k in range(n_pages)]
    return pl.pallas_call(
        functools.partial(_assemble_body, n_pages=n_pages, page=page, s_new=s_new, win_len=win_len),
        grid_spec=pltpu.PrefetchScalarGridSpec(
            num_scalar_prefetch=1, grid=(bn,),
            in_specs=page_specs + [pl.BlockSpec((None, s_new, 512), lambda b, pt: (b, 0, 1)),
                                   pl.BlockSpec((None, s_new, 512), lambda b, pt: (b, 0, 2)),
                                   pl.BlockSpec((None, win_len, 512), lambda b, pt: (b, 0, 0))],
            out_specs=[pl.BlockSpec((None, past + SLC_LEN, 512), lambda b, pt: (b, 0, 0)),
                       pl.BlockSpec((None, win_len + SLC_LEN, 512), lambda b, pt: (b, 0, 0))]),
        out_shape=[jax.ShapeDtypeStruct((bn, past + SLC_LEN, 512), BF16),
                   jax.ShapeDtypeStruct((bn, win_len + SLC_LEN, 512), BF16)],
        compiler_params=_cparams("arbitrary"),
        name="assemble_sample_kv",
    )(page_table.reshape(-1), *([cache] * n_pages), kv_new, kv_new, cache_win)


def _softmax_update(state, s, v):
    m, l, acc = state
    mn = jnp.maximum(m, jnp.max(s, axis=-1, keepdims=True))
    alpha = jnp.exp(m - mn)
    e = jnp.exp(s - mn)
    return mn, alpha * l + jnp.sum(e, axis=-1, keepdims=True), alpha * acc + _dot(e.astype(BF16), v)


def _attn_body(q_ref, gl_ref, kc_ref, vc_ref, ks_ref, vs_ref, kw_ref, vw_ref, augk_ref, augw_ref,
               ov_ref, cbe_ref, nb_ref, wb_ref, augc_ref, ex_ref, o_ref,
               *, qb, nc, n_slc, top_n, pad_s, t0, multi_block):
    qi = pl.program_id(2)
    rr = GQA * qb
    lane = lax.broadcasted_iota(jnp.int32, (rr, LANES), 1)
    is_a = lane < HEAD_DIM
    if multi_block:
        cur = t0 // SLC_LEN + qi
        tau = t0 // CMP_STRIDE + qi * (qb // CMP_STRIDE)
        win_start = pl.multiple_of(qi * qb, SLC_LEN)
    else:
        cur = t0 // SLC_LEN
        win_start = 0

    qfull = q_ref[...]
    qs = jnp.concatenate([qfull[:, g * LANES:(g + 1) * LANES] for g in range(GQA)], axis=0) * (HEAD_DIM ** -0.5)
    q_heads = (jnp.where(is_a, qs, 0.0).astype(BF16), jnp.where(is_a, 0.0, qs).astype(BF16))

    kcp = kc_ref[...]
    vcp = vc_ref[...]
    ov = ov_ref[...]
    jl = lax.broadcasted_iota(jnp.int32, (qb, LANES), 1)
    forced = (jl == 0) | (jl == cur) | (jl == cur - 1)

    o_cmp, lhs_far, lhs_near, lhs_win = [], [], [], []
    for x in range(2):
        cb = cbe_ref[x]
        if multi_block:
            cb = pltpu.roll(cb, tau, 1)[:, nc:]
        s = _dot_nt(q_heads[x], kcp) + cb
        m = jnp.max(s, axis=-1, keepdims=True)
        e = jnp.exp(s - m)
        p = e / jnp.sum(e, axis=-1, keepdims=True)
        p = jnp.where(m > 0.1 * NEG, p, 0.0).astype(BF16)
        o_cmp.append(_dot(p, vcp))
        imp4 = _dot(p, ov)
        imp = imp4[0:qb] + imp4[qb:2 * qb] + imp4[2 * qb:3 * qb] + imp4[3 * qb:4 * qb]
        v = jnp.where(forced, FORCE, jnp.where((jl > cur) | (jl >= n_slc), NEG, imp))
        rank = jnp.zeros((qb, LANES), F32)
        for k in range(n_slc):
            vk = v[:, k:k + 1]
            tie = jnp.where(jl > k, 1.0, 0.0)
            rank = rank + jnp.where(vk > v, 1.0, jnp.where(vk == v, tie, 0.0))
        notsel = jnp.where((rank < top_n) | (jl >= N_BLOCK_LANES), 0.0, 1.0)
        ns = jnp.concatenate([notsel] * GQA, axis=0)
        ns_far = jnp.where((lane >= cur - 2) & (lane < N_BLOCK_LANES), 1.0, ns)
        lhs_far.append(jnp.concatenate([q_heads[x], (ns_far + augc_ref[x, 0]).astype(BF16)], axis=1))
        lhs_near.append(jnp.concatenate([q_heads[x], (ns + augc_ref[x, 1]).astype(BF16)], axis=1))
        lhs_win.append(jnp.concatenate([q_heads[x], augc_ref[x, 1].astype(BF16)], axis=1))

    def init():
        return (jnp.full((rr, 1), -jnp.inf, F32), jnp.zeros((rr, 1), F32), jnp.zeros((rr, LANES), F32))

    def far_step(c, states):
        st = pl.multiple_of(pad_s + c * 256, 256)
        kk = jnp.concatenate([ks_ref[pl.ds(st, 256), :], augk_ref[pl.ds(st, 256), :]], axis=1)
        vv = vs_ref[pl.ds(st, 256), :]
        return tuple(_softmax_update(states[x], _dot_nt(lhs_far[x], kk), vv) for x in range(2))

    n_far = (jnp.maximum(cur - 2, 0) + 3) // 4
    states = lax.fori_loop(0, n_far, far_step, (init(), init()))
    st = pad_s + (cur - 2) * SLC_LEN
    if multi_block:
        st = pl.multiple_of(st, SLC_LEN)
    kk = jnp.concatenate([ks_ref[pl.ds(st, 3 * SLC_LEN), :], augk_ref[pl.ds(st, 3 * SLC_LEN), :]], axis=1)
    vv = vs_ref[pl.ds(st, 3 * SLC_LEN), :]
    o_slc = []
    for x in range(2):
        _, l, acc = _softmax_update(states[x], _dot_nt(lhs_near[x], kk) + nb_ref[x], vv)
        o_slc.append(acc / l)

    wlen = WINDOW + SLC_LEN
    kk = jnp.concatenate([kw_ref[pl.ds(win_start, wlen), :], augw_ref[pl.ds(win_start, wlen), :]], axis=1)
    vv = vw_ref[pl.ds(win_start, wlen), :]
    o_win = []
    for x in range(2):
        _, l, acc = _softmax_update(init(), _dot_nt(lhs_win[x], kk) + wb_ref[x], vv)
        o_win.append(acc / l)

    glt = jnp.concatenate([gl_ref[...]] * GQA, axis=0)
    slab = lax.broadcasted_iota(jnp.int32, (rr, LANES), 0) // qb
    gsrc = jnp.concatenate([jnp.where(slab == g, glt, 0.0) for g in range(GQA)], axis=1)
    ghi = gsrc.astype(BF16)
    glo = (gsrc - ghi.astype(F32)).astype(BF16)
    ex = ex_ref[...]
    gates = jax.nn.sigmoid(_dot(ghi, ex) + _dot(glo, ex))
    out = None
    for br, pair in enumerate((o_cmp, o_slc, o_win)):
        term = gates[:, br * LANES:(br + 1) * LANES] * jnp.where(is_a, pair[0], pair[1])
        out = term if out is None else out + term
    o_ref[...] = jnp.concatenate([out[g * qb:(g + 1) * qb] for g in range(GQA)], axis=1).astype(o_ref.dtype)


def nsa_attention(u, cmp_kv, slc_kv, slc_cols, win_kv, win_cols, augk, augw, ov, tiles, ex,
                  *, qb, n_slc, pad_s, t0):
    bn, s, _ = u.shape
    nqb = s // qb
    nc = cmp_kv.shape[1]
    rr = GQA * qb
    cbe, nb, wb, augc = tiles
    srows = slc_kv.shape[1]
    wrows = win_kv.shape[1]
    kcol, vcol = slc_cols
    kwcol, vwcol = win_cols
    assert qb == SLC_LEN or nqb == 1
    assert t0 % SLC_LEN == 0
    body = functools.partial(_attn_body, qb=qb, nc=nc, n_slc=n_slc, top_n=min(N_SELECT, n_slc),
                             pad_s=pad_s, t0=t0, multi_block=nqb > 1)
    pair_tile = lambda a: pl.BlockSpec((None,) + a.shape[1:], lambda b, p, i: (p,) + (0,) * (a.ndim - 1))
    return pl.pallas_call(
        body,
        grid=(bn, N_PAIRS, nqb),
        in_specs=[pl.BlockSpec((None, qb, 4 * LANES), lambda b, p, i: (b, i, p)),
                  pl.BlockSpec((None, qb, LANES), lambda b, p, i: (b, i, N_HEADS * HEAD_DIM // LANES)),
                  pl.BlockSpec((None, nc, LANES), lambda b, p, i: (b, 0, p)),
                  pl.BlockSpec((None, nc, LANES), lambda b, p, i: (b, 0, N_PAIRS + p)),
                  pl.BlockSpec((None, srows, LANES), lambda b, p, i: (b, 0, kcol + p)),
                  pl.BlockSpec((None, srows, LANES), lambda b, p, i: (b, 0, vcol + p)),
                  pl.BlockSpec((None, wrows, LANES), lambda b, p, i: (b, 0, kwcol + p)),
                  pl.BlockSpec((None, wrows, LANES), lambda b, p, i: (b, 0, vwcol + p)),
                  pl.BlockSpec(augk.shape, lambda b, p, i: (0, 0)),
                  pl.BlockSpec(augw.shape, lambda b, p, i: (0, 0)),
                  pl.BlockSpec(ov.shape, lambda b, p, i: (0, 0)),
                  pair_tile(cbe), pair_tile(nb), pair_tile(wb), pair_tile(augc), pair_tile(ex)],
        out_specs=pl.BlockSpec((None, qb, 4 * LANES), lambda b, p, i: (b, i, p)),
        out_shape=jax.ShapeDtypeStruct((bn, s, N_HEADS * HEAD_DIM), F32),
        compiler_params=_cparams("arbitrary", "arbitrary", "arbitrary"),
        name="nsa_attention",
    )(u, u, cmp_kv, cmp_kv, slc_kv, slc_kv, win_kv, win_kv, augk, augw, ov, cbe, nb, wb, augc, ex)


def _head_perm():
    idx = np.empty((N_PAIRS, GQA, 2, HEAD_DIM), np.int32)
    for p in range(N_PAIRS):
        for g in range(GQA):
            for half in range(2):
                head = (2 * p + half) * GQA + g
                idx[p, g, half] = head * HEAD_DIM + np.arange(HEAD_DIM)
    return idx.reshape(-1)


def _gate_expand():
    ex = np.zeros((N_PAIRS, GQA * LANES, 3 * LANES), np.float32)
    for p in range(N_PAIRS):
        for g in range(GQA):
            for half in range(2):
                head = (2 * p + half) * GQA + g
                for br in range(3):
                    ex[p, g * LANES + head * 3 + br, br * LANES + half * HEAD_DIM:br * LANES + (half + 1) * HEAD_DIM] = 1.0
    return ex


def _overlap(nc, n_cmp, n_slc):
    i = np.arange(nc)[:, None] * CMP_STRIDE
    j = np.arange(LANES)[None, :] * SLC_LEN
    ov = (i < j + SLC_LEN) & (i + CMP_LEN > j) & (np.arange(nc)[:, None] < n_cmp) & (np.arange(LANES)[None, :] < n_slc)
    return ov.astype(np.float32)


def _aug_keys(pad, n_keys):
    a = np.zeros((pad + n_keys, LANES), np.float32)
    a[:pad, AUG_PAD] = NEG
    k = np.arange(n_keys)
    a[pad + k, k // SLC_LEN] = NEG
    a[pad:, AUG_BIAS_HI] = 1.0
    a[pad:, AUG_BIAS_LO] = 1.0
    return a


def _attn_tiles(rel_bias, qb, t0, nc, rolled):
    i = np.arange(qb)[:, None]
    near = np.arange(3 * SLC_LEN)[None, :]
    d_near = SLC_LEN * (2 - near // SLC_LEN) + i - near % SLC_LEN
    c = np.arange(WINDOW + SLC_LEN)[None, :]
    d_win = i + WINDOW - c
    d_win = np.where(d_win < WINDOW, d_win, -1)
    if rolled:
        e = np.arange(2 * nc)[None, :] - nc
        d_cmp = i - (CMP_LEN - 1) - CMP_STRIDE * e
    else:
        d_cmp = t0 + i - CMP_STRIDE * np.arange(nc)[None, :] - (CMP_LEN - 1)
    widths = [d_near.shape[1], d_win.shape[1], d_cmp.shape[1]]
    padded = [-(-w // LANES) * LANES for w in widths]
    dist = np.full((qb, sum(padded)), -1, np.int32)
    off = 0
    offs = []
    for d, w, pw in zip((d_near, d_win, d_cmp), widths, padded):
        dist[:, off:off + w] = np.clip(d, -1, 4 * MAX_DIST)
        offs.append(off)
        off += pw
    tiles = rel_bias_tiles(rel_bias, jnp.asarray(dist))
    tiles = tiles.reshape(N_PAIRS, 2, GQA * qb, dist.shape[1])
    nb, wb, cbe = (tiles[..., o:o + w] for o, w in zip(offs, widths))
    far = rel_bias[N_BUCKETS - 1].reshape(N_PAIRS, 2, GQA)
    hi = far.astype(BF16).astype(F32)
    lo = far - hi
    rows = jnp.repeat(jnp.stack([hi, lo], axis=-1), qb, axis=2)
    augc = jnp.zeros((N_PAIRS, 2, 2, GQA * qb, LANES), F32)
    augc = augc.at[:, :, 0, :, AUG_BIAS_HI].set(rows[..., 0]).at[:, :, 0, :, AUG_BIAS_LO].set(rows[..., 1])
    augc = augc.at[:, :, 1, :, AUG_PAD].set(1.0)
    return cbe, nb, wb, augc


def _forward(x, ada, ada_kv, conv_state, h_state, past, w, shared):
    bn, s, _ = x.shape
    m = bn * s
    xf = x.reshape(m, D_MODEL)
    new_conv, new_h = [], []
    h = normmod(xf, w["norm_g"][0, 0], ada[0], 0, 1, bn, s)
    kv = None
    attn_args = None
    for l in range(DEPTH):
        if l < N_A_LAYERS:
            u = mm_plain(h, w["w_in_a"][l]).reshape(bn, s, 2 * D_RNN)
            conv8 = jnp.pad(conv_state[l], ((0, 0), (8 - (CONV_W - 1), 0), (0, 0)))
            y, conv_o, h_o = rglru(u, conv8, h_state[l][:, None, :], w["conv_w"][l], w["conv_b"][l],
                                   w["w_rg_a"][l], w["b_rg_a"][l], w["w_rg_x"][l], w["b_rg_x"][l], w["lru_lambda"][l])
            new_conv.append(conv_o[:, 8 - (CONV_W - 1):])
            new_h.append(h_o[:, 0])
            a_in = y.reshape(m, D_RNN)
            w_out = w["w_out_a"][l]
        else:
            lb = l - N_A_LAYERS
            if lb == 0:
                kv = mm_plain(h_kv, w["w_kv"]).reshape(bn, s, 6 * N_KV_HEADS * HEAD_DIM)
                attn_args = _prepare_attention(kv, past, w, shared, bn, s)
            u = mm_plain(h, w["w_in_b"][lb]).reshape(bn, s, -1)
            a_in = nsa_attention(u, *attn_args[0], **attn_args[1]).reshape(m, N_HEADS * HEAD_DIM)
            w_out = w["w_out_b"][lb]
        x_new, (h2,) = mm_resnorm(a_in, w_out, xf, (ada[l], 2), w["norm_g"][l, 1],
                                  [(w["norm_g"][l, 2], ada[l], 3, ada[l], 4)], bn, s)
        f = mm_swiglu(h2, w["w_ffn_in"][l])
        heads = []
        if l + 1 < DEPTH:
            heads.append((w["norm_g"][l + 1, 0], ada[l + 1], 0, ada[l + 1], 1))
        if l + 1 == N_A_LAYERS:
            heads.append((w["norm_kv"], ada_kv, 0, ada_kv, 1))
        xf, hs = mm_resnorm(f, w["w_ffn_out"][l], x_new, (ada[l], 5), w["norm_g"][l, 3], heads, bn, s)
        if heads:
            h = hs[0]
        if l + 1 == N_A_LAYERS:
            h_kv = hs[1]
    kv_rows = kv[:, :, :4 * N_KV_HEADS * HEAD_DIM].reshape(bn, s, 4, N_KV_HEADS, HEAD_DIM)
    new_win = kv[:, :, 4 * N_KV_HEADS * HEAD_DIM:].reshape(bn, s, 2, N_KV_HEADS, HEAD_DIM)
    if past is None:
        win_state = new_win[:, -min(WINDOW, s):]
    else:
        cache_win = past[2]
        win_state = jnp.concatenate([cache_win, new_win], axis=1)[:, -cache_win.shape[1]:]
    return xf.reshape(bn, s, D_MODEL), kv_rows, win_state, jnp.stack(new_conv), jnp.stack(new_h)


def _prepare_attention(kv, past, w, shared, bn, s):
    rel_bias = w["rel_bias"]
    if past is None:
        t0, qb, pad = 0, Q_BLOCK, WINDOW
        n_keys = s
        xlo, xhi = reorder_dense(kv, shared["poslo"], shared["poshi"])
        kvb = jnp.pad(kv.astype(BF16), ((0, 0), (pad, 0), (0, 0)))
        slc_kv = win_kv = kvb
        slc_cols, win_cols = (2 * N_PAIRS, 3 * N_PAIRS), (4 * N_PAIRS, 5 * N_PAIRS)
        augk = jnp.asarray(_aug_keys(pad, n_keys), BF16)
        augw = augk
        sb = bn
    else:
        cache, page_table, cache_win = past
        n_pool, page = cache.shape[:2]
        cache2 = cache.reshape(n_pool, page, -1)
        t0 = page_table.shape[1] * page
        qb, pad = s, 0
        n_keys = t0 + SLC_LEN
        assert cache_win.shape[1] == WINDOW and t0 >= WINDOW
        xlo, xhi = reorder_paged(cache2, page_table, shared["poslo"], shared["poshi"])
        slc_kv, win_kv = assemble_sample_kv(cache2, page_table, kv, cache_win.reshape(bn, WINDOW, -1))
        slc_cols, win_cols = (0, N_PAIRS), (0, N_PAIRS)
        augk = jnp.asarray(_aug_keys(0, n_keys), BF16)
        augw = jnp.zeros((win_kv.shape[1], LANES), BF16)
        sb = 8
    n16 = (t0 + s) // CMP_STRIDE
    n_cmp = n16 - CMP_LEN // CMP_STRIDE + 1
    nc = xlo.shape[2]
    assert nc >= n_cmp and nc % LANES == 0
    n_slc = -(-(t0 + s) // SLC_LEN)
    assert n_slc <= N_BLOCK_LANES
    cmp_kv = compress(xlo, xhi, shared["wlo"], shared["whi"], shared["w2bd"], sb)
    tiles = _attn_tiles(rel_bias, qb, t0, nc, rolled=past is None)
    ov = jnp.asarray(_overlap(nc, n_cmp, n_slc), BF16)
    args = (cmp_kv, slc_kv, slc_cols, win_kv, win_cols, augk, augw, ov, tiles, shared["ex"])
    return args, dict(qb=qb, n_slc=n_slc, pad_s=pad, t0=t0)


def kernel(x_prompt, x_sample, c_prompt, c_sample, cache_kv, cache_win, state_conv, state_h, page_table, w_ada, b_ada, norm_g, w_in_a, conv_w, conv_b, w_rg_a, b_rg_a, w_rg_x, b_rg_x, lru_lambda, w_out_a, w_ada_kv, b_ada_kv, norm_kv, w_kv, cmp_pos, cmp_w1, cmp_w2, w_in_b, w_out_b, rel_bias, w_ffn_in, w_ffn_out):
    bp = x_prompt.shape[0]
    bs = x_sample.shape[0]

    perm = _head_perm()
    n_q = N_HEADS * HEAD_DIM
    w_in_b_p = jnp.concatenate([w_in_b[:, :, :n_q][:, :, perm], w_in_b[:, :, n_q:],
                                jnp.zeros(w_in_b.shape[:2] + (LANES - 3 * N_HEADS,), F32)], axis=-1)
    w = dict(norm_g=norm_g, norm_kv=norm_kv, conv_w=conv_w, conv_b=conv_b, b_rg_a=b_rg_a, b_rg_x=b_rg_x,
             lru_lambda=lru_lambda, rel_bias=rel_bias,
             w_in_a=w_in_a.astype(BF16), w_rg_a=w_rg_a.astype(BF16), w_rg_x=w_rg_x.astype(BF16),
             w_out_a=w_out_a.astype(BF16), w_kv=w_kv.astype(BF16), w_in_b=w_in_b_p.astype(BF16),
             w_out_b=w_out_b[:, perm, :].astype(BF16), w_ffn_in=w_ffn_in.astype(BF16), w_ffn_out=w_ffn_out.astype(BF16))

    eye = jnp.eye(N_KV_HEADS, dtype=F32)
    w1r = cmp_w1.reshape(2, CMP_LEN, HEAD_DIM, CMP_HIDDEN)
    w1bd = jnp.einsum("hk,srdj->srhdkj", eye, w1r).reshape(2, CMP_LEN, N_KV_HEADS * HEAD_DIM, N_KV_HEADS * CMP_HIDDEN)
    w2bd = jnp.einsum("hk,sjd->shjkd", eye, cmp_w2).reshape(2, N_KV_HEADS * CMP_HIDDEN, N_KV_HEADS * HEAD_DIM)
    pos = jnp.broadcast_to(cmp_pos[:, :, None, :], (2, CMP_LEN, N_KV_HEADS, HEAD_DIM)).reshape(2, CMP_LEN, -1)
    pos = jnp.concatenate([pos[0], pos[1]], axis=-1)
    shared = dict(wlo=w1bd[:, :CMP_STRIDE].astype(BF16), whi=w1bd[:, CMP_STRIDE:].astype(BF16),
                  w2bd=w2bd.astype(BF16), poslo=pos[:CMP_STRIDE], poshi=pos[CMP_STRIDE:],
                  ex=jnp.asarray(_gate_expand(), BF16))

    n_c = bp + bs
    c_all = jnp.concatenate([c_prompt, c_sample, jnp.zeros((-n_c % 8, D_MODEL), F32)], axis=0)
    ada_all = [mm_plain(c_all, w_ada[l].astype(BF16), bias=b_ada[l], act="silu", tn=2048) for l in range(DEPTH)]
    ada_kv_all = mm_plain(c_all, w_ada_kv.astype(BF16), bias=b_ada_kv, act="silu", tn=2048)
    ada_p = [a[:bp, None, :] for a in ada_all]
    ada_s = [a[bp:n_c, None, :] for a in ada_all]

    zero_conv = jnp.zeros((N_A_LAYERS, bp, CONV_W - 1, D_RNN), F32)
    zero_h = jnp.zeros((N_A_LAYERS, bp, D_RNN), F32)
    y_p, kv_p, win_p, conv_p, h_p = _forward(x_prompt, ada_p, ada_kv_all[:bp, None, :], zero_conv, zero_h, None, w, shared)
    y_s, kv_s, win_s, conv_s, h_s = _forward(x_sample, ada_s, ada_kv_all[bp:n_c, None, :], state_conv, state_h,
                                             (cache_kv, page_table, cache_win), w, shared)
    return (y_p, y_s, kv_p, kv_s, win_p, win_s, conv_p, conv_s, h_p, h_s)
```

```python
import functools
import math

import numpy as np
import jax
import jax.numpy as jnp
from jax import lax
from jax.experimental import pallas as pl
from jax.experimental.pallas import tpu as pltpu

D_MODEL = 1024
DEPTH = 4
N_A_LAYERS = 2
D_RNN = 1024
LRU_BLOCKS = 8
LRU_BW = 128
CONV_W = 4
LRU_C = 8.0
N_HEADS = 16
HEAD_DIM = 64
N_KV_HEADS = 4
GQA = 4
CMP_LEN = 32
CMP_STRIDE = 16
CMP_HIDDEN = 128
SLC_LEN = 64
N_SELECT = 16
WINDOW = 512
Q_BLOCK = 64
N_BUCKETS = 32
MAX_DIST = 128
D_FF = 2816
EPS = 1e-6
NEG = -1e30
FORCE = 1e9

LANES = 128
PAIR_W = 2 * HEAD_DIM
N_PAIRS = N_KV_HEADS // 2
AUG_BIAS_HI = 64
AUG_BIAS_LO = 65
AUG_PAD = 66
N_BLOCK_LANES = 64
FAR_CHUNK = 256
CMP_RSTEP = 4
LOG2E = 1.4426950408889634
VMEM_LIMIT = 56 * 1024 * 1024

F32 = jnp.float32
BF16 = jnp.bfloat16


def _cparams(*sem):
    return pltpu.CompilerParams(dimension_semantics=sem, vmem_limit_bytes=VMEM_LIMIT)


def _dot(a, b):
    return jnp.dot(a, b, preferred_element_type=F32)


def _dot_nt(a, b):
    return lax.dot_general(a, b, (((1,), (1,)), ((), ())), preferred_element_type=F32)


def _gelu_tanh(x):
    return 0.5 * x * (1.0 + jnp.tanh(math.sqrt(2.0 / math.pi) * (x + 0.044715 * (x * x * x))))


def _rms(x, gain):
    return x * lax.rsqrt(jnp.mean(x * x, axis=-1, keepdims=True) + EPS) * gain


def _rows(vec_ref, bb, ts):
    v = vec_ref[...]
    d = v.shape[-1]
    return jnp.broadcast_to(v, (bb, ts, d)).reshape(bb * ts, d)


def _vec_spec(bb, tiles_per_seq, col, ngrid):
    if ngrid == 1:
        return pl.BlockSpec((bb, 1, D_MODEL), lambda i: (i // tiles_per_seq, 0, col))
    return pl.BlockSpec((bb, 1, D_MODEL), lambda j, i: (i // tiles_per_seq, 0, col))


def _row_tiling(bn, s, tm=512):
    if s >= tm:
        assert s % tm == 0
        return tm, 1, tm, s // tm
    assert tm % s == 0 and s % 8 == 0
    bb = min(bn, tm // s)
    assert bn % bb == 0
    return bb * s, bb, s, 1


def _mm_body(*refs, act, has_bias):
    if has_bias:
        a_ref, w_ref, b_ref, o_ref = refs
    else:
        a_ref, w_ref, o_ref = refs
    a = a_ref[...]
    if act == "silu":
        a = a.astype(F32)
        a = a * jax.nn.sigmoid(a)
    y = _dot(a.astype(BF16), w_ref[...])
    if has_bias:
        y = y + b_ref[...]
    o_ref[...] = y.astype(o_ref.dtype)


def mm_plain(a, w, bias=None, act=None, tm=512, tn=None, out_dtype=F32):
    m, k = a.shape
    n = w.shape[1]
    tm = min(tm, m)
    tn = tn or n
    assert m % tm == 0 and n % tn == 0
    in_specs = [pl.BlockSpec((tm, k), lambda j, i: (i, 0)), pl.BlockSpec((k, tn), lambda j, i: (0, j))]
    args = [a, w]
    if bias is not None:
        in_specs.append(pl.BlockSpec((1, tn), lambda j, i: (0, j)))
        args.append(bias.reshape(1, n))
    return pl.pallas_call(
        functools.partial(_mm_body, act=act, has_bias=bias is not None),
        grid=(n // tn, m // tm),
        in_specs=in_specs,
        out_specs=pl.BlockSpec((tm, tn), lambda j, i: (i, j)),
        out_shape=jax.ShapeDtypeStruct((m, n), out_dtype),
        compiler_params=_cparams("arbitrary", "arbitrary"),
        name="mm_plain",
    )(*args)


def _swiglu_body(h_ref, wg_ref, wu_ref, o_ref):
    h = h_ref[...]
    g = _dot(h, wg_ref[...])
    u = _dot(h, wu_ref[...])
    o_ref[...] = (g * jax.nn.sigmoid(g) * u).astype(o_ref.dtype)


def mm_swiglu(h, w_in, tm=512, tn=1408):
    m, k = h.shape
    nf = w_in.shape[1] // 2
    tm = min(tm, m)
    assert nf % tn == 0 and m % tm == 0
    nj = nf // tn
    return pl.pallas_call(
        _swiglu_body,
        grid=(nj, m // tm),
        in_specs=[pl.BlockSpec((tm, k), lambda j, i: (i, 0)),
                  pl.BlockSpec((k, tn), lambda j, i: (0, j)),
                  pl.BlockSpec((k, tn), lambda j, i: (0, j + nj))],
        out_specs=pl.BlockSpec((tm, tn), lambda j, i: (i, j)),
        out_shape=jax.ShapeDtypeStruct((m, nf), BF16),
        compiler_params=_cparams("arbitrary", "arbitrary"),
        name="mm_swiglu",
    )(h, w_in, w_in)


def _normmod_body(x_ref, g_ref, sh_ref, sc_ref, o_ref, *, bb, ts):
    y = _rms(x_ref[...], g_ref[...])
    o_ref[...] = (y * (1.0 + _rows(sc_ref, bb, ts)) + _rows(sh_ref, bb, ts)).astype(o_ref.dtype)


def normmod(x, gain, ada, col_shift, col_scale, bn, s):
    m = x.shape[0]
    tm, bb, ts, tps = _row_tiling(bn, s)
    return pl.pallas_call(
        functools.partial(_normmod_body, bb=bb, ts=ts),
        grid=(m // tm,),
        in_specs=[pl.BlockSpec((tm, D_MODEL), lambda i: (i, 0)),
                  pl.BlockSpec((1, D_MODEL), lambda i: (0, 0)),
                  _vec_spec(bb, tps, col_shift, 1),
                  _vec_spec(bb, tps, col_scale, 1)],
        out_specs=pl.BlockSpec((tm, D_MODEL), lambda i: (i, 0)),
        out_shape=jax.ShapeDtypeStruct((m, D_MODEL), BF16),
        compiler_params=_cparams("arbitrary"),
        name="normmod",
    )(x, gain.reshape(1, D_MODEL), ada, ada)


def _resnorm_body(*refs, bb, ts, n_heads):
    a_ref, w_ref, x_ref, gate_ref, ngain_ref = refs[:5]
    head_refs = refs[5:5 + 3 * n_heads]
    xo_ref = refs[5 + 3 * n_heads]
    ho_refs = refs[6 + 3 * n_heads:]
    y = _dot(a_ref[...].astype(BF16), w_ref[...])
    x = x_ref[...] + _rows(gate_ref, bb, ts) * _rms(y, ngain_ref[...])
    xo_ref[...] = x
    for k in range(n_heads):
        hg_ref, sh_ref, sc_ref = head_refs[3 * k:3 * k + 3]
        h = _rms(x, hg_ref[...]) * (1.0 + _rows(sc_ref, bb, ts)) + _rows(sh_ref, bb, ts)
        ho_refs[k][...] = h.astype(ho_refs[k].dtype)


def mm_resnorm(a, w, x, gate, ngain, heads, bn, s):
    m, k = a.shape
    tm, bb, ts, tps = _row_tiling(bn, s)
    row = lambda i: (i, 0)
    const = lambda i: (0, 0)
    in_specs = [pl.BlockSpec((tm, k), row), pl.BlockSpec((k, D_MODEL), const),
                pl.BlockSpec((tm, D_MODEL), row), _vec_spec(bb, tps, gate[1], 1),
                pl.BlockSpec((1, D_MODEL), const)]
    args = [a, w, x, gate[0], ngain.reshape(1, D_MODEL)]
    for hg, sh_arr, sh_col, sc_arr, sc_col in heads:
        in_specs += [pl.BlockSpec((1, D_MODEL), const), _vec_spec(bb, tps, sh_col, 1), _vec_spec(bb, tps, sc_col, 1)]
        args += [hg.reshape(1, D_MODEL), sh_arr, sc_arr]
    out_shape = [jax.ShapeDtypeStruct((m, D_MODEL), F32)] + [jax.ShapeDtypeStruct((m, D_MODEL), BF16)] * len(heads)
    out_specs = [pl.BlockSpec((tm, D_MODEL), row)] * (1 + len(heads))
    outs = pl.pallas_call(
        functools.partial(_resnorm_body, bb=bb, ts=ts, n_heads=len(heads)),
        grid=(m // tm,),
        in_specs=in_specs,
        out_specs=out_specs,
        out_shape=out_shape,
        compiler_params=_cparams("arbitrary"),
        name="mm_resnorm",
    )(*args)
    return outs[0], list(outs[1:])


def _rglru_body(u_ref, conv8_ref, hprev_ref, cw_ref, cb_ref, wa_ref, ba_ref, wx_ref, bx_ref, lam_ref,
                y_ref, convo_ref, ho_ref, carry_h, carry_conv, *, bb, tt):
    j = pl.program_id(1)

    @pl.when(j == 0)
    def _():
        carry_h[...] = hprev_ref[...]
        carry_conv[...] = conv8_ref[...]

    u = u_ref[...]
    xb = u[:, :, :D_RNN]
    yb = u[:, :, D_RNN:]
    ext = jnp.concatenate([carry_conv[...], xb], axis=1)
    cw = cw_ref[...]
    xc = cb_ref[...] + ext[:, 5:5 + tt] * cw[0:1]
    for k in range(1, CONV_W):
        xc = xc + ext[:, 5 + k:5 + k + tt] * cw[k:k + 1]
    last8 = xb[:, tt - 8:tt]
    carry_conv[...] = last8
    convo_ref[...] = last8

    xc2 = xc.reshape(bb * tt, D_RNN)
    xcb = xc2.astype(BF16)

    def gate(w_ref, b_ref):
        cols = [_dot(xcb[:, n * LRU_BW:(n + 1) * LRU_BW], w_ref[n]) for n in range(LRU_BLOCKS)]
        return jax.nn.sigmoid(jnp.concatenate(cols, axis=1) + b_ref[...])

    r = gate(wa_ref, ba_ref)
    ig = gate(wx_ref, bx_ref)
    nl = -lam_ref[...]
    z = jnp.exp(-jnp.abs(nl))
    w1 = 1.0 + z
    log1p_z = jnp.where(w1 == 1.0, z, jnp.log(w1) * (z / jnp.where(w1 == 1.0, 1.0, w1 - 1.0)))
    softplus = jnp.maximum(nl, 0.0) + log1p_z
    log_a = -LRU_C * r * softplus
    a = jnp.exp(log_a)
    b = jnp.sqrt(1.0 - a * a) * (ig * xc2)

    row8 = lax.broadcasted_iota(jnp.int32, (bb * tt, D_RNN), 0) % 8
    for d in (1, 2, 4):
        a_s = pltpu.roll(a, d, 0)
        b_s = pltpu.roll(b, d, 0)
        keep = row8 >= d
        b = jnp.where(keep, a * b_s + b, b)
        a = jnp.where(keep, a * a_s, a)

    a3 = a.reshape(bb, tt, D_RNN)
    b3 = b.reshape(bb, tt, D_RNN)
    carry = carry_h[...]
    groups = []
    for g in range(tt // 8):
        hg = a3[:, 8 * g:8 * g + 8] * carry + b3[:, 8 * g:8 * g + 8]
        carry = hg[:, 7:8]
        groups.append(hg)
    hs = groups[0] if len(groups) == 1 else jnp.concatenate(groups, axis=1)
    carry_h[...] = carry
    ho_ref[...] = carry
    y_ref[...] = (hs * _gelu_tanh(yb)).astype(y_ref.dtype)


def rglru(u, conv8, hprev, cw, cb, wa, ba, wx, bx, lam):
    bn, s, _ = u.shape
    if s >= 256:
        bb, tt = 1, 256
    else:
        bb, tt = min(bn, 512 // s), s
    assert s % tt == 0 and bn % bb == 0 and tt % 8 == 0
    vec = lambda a: a.reshape(1, D_RNN)
    c2 = lambda i, j: (0, 0)
    c3 = lambda i, j: (0, 0, 0)
    return pl.pallas_call(
        functools.partial(_rglru_body, bb=bb, tt=tt),
        grid=(bn // bb, s // tt),
        in_specs=[pl.BlockSpec((bb, tt, 2 * D_RNN), lambda i, j: (i, j, 0)),
                  pl.BlockSpec((bb, 8, D_RNN), lambda i, j: (i, 0, 0)),
                  pl.BlockSpec((bb, 1, D_RNN), lambda i, j: (i, 0, 0)),
                  pl.BlockSpec((CONV_W, D_RNN), c2),
                  pl.BlockSpec((1, D_RNN), c2),
                  pl.BlockSpec((LRU_BLOCKS, LRU_BW, LRU_BW), c3),
                  pl.BlockSpec((1, D_RNN), c2),
                  pl.BlockSpec((LRU_BLOCKS, LRU_BW, LRU_BW), c3),
                  pl.BlockSpec((1, D_RNN), c2),
                  pl.BlockSpec((1, D_RNN), c2)],
        out_specs=[pl.BlockSpec((bb, tt, D_RNN), lambda i, j: (i, j, 0)),
                   pl.BlockSpec((bb, 8, D_RNN), lambda i, j: (i, 0, 0)),
                   pl.BlockSpec((bb, 1, D_RNN), lambda i, j: (i, 0, 0))],
        out_shape=[jax.ShapeDtypeStruct((bn, s, D_RNN), BF16),
                   jax.ShapeDtypeStruct((bn, 8, D_RNN), F32),
                   jax.ShapeDtypeStruct((bn, 1, D_RNN), F32)],
        scratch_shapes=[pltpu.VMEM((bb, 1, D_RNN), F32), pltpu.VMEM((bb, 8, D_RNN), F32)],
        compiler_params=_cparams("arbitrary", "arbitrary"),
        name="rglru",
    )(u, conv8, hprev, cw, vec(cb), wa, vec(ba), wx, vec(bx), vec(lam))


def _bias_body(tb_ref, d_ref, o_ref):
    h = pl.program_id(0)
    d = d_ref[...]
    dc = jnp.maximum(d, 0)
    max_exact = N_BUCKETS // 2
    scaled = jnp.log(jnp.maximum(dc, 1).astype(F32) / max_exact) / math.log(MAX_DIST / max_exact)
    large = jnp.minimum(max_exact + (scaled * (N_BUCKETS - max_exact)).astype(jnp.int32), N_BUCKETS - 1)
    bucket = jnp.where(dc < max_exact, dc, large)
    acc = jnp.full(d.shape, tb_ref[0, h], F32)
    for k in range(1, N_BUCKETS):
        acc = jnp.where(bucket == k, tb_ref[k, h], acc)
    o_ref[...] = jnp.where(d < 0, NEG, acc)


def rel_bias_tiles(table, dist):
    rows, cols = dist.shape
    return pl.pallas_call(
        _bias_body,
        grid=(N_HEADS,),
        in_specs=[pl.BlockSpec(memory_space=pltpu.SMEM), pl.BlockSpec((rows, cols), lambda h: (0, 0))],
        out_specs=pl.BlockSpec((None, rows, cols), lambda h: (h, 0, 0)),
        out_shape=jax.ShapeDtypeStruct((N_HEADS, rows, cols), F32),
        compiler_params=_cparams("arbitrary"),
        name="rel_bias_tiles",
    )(table, dist)


def _reorder_emit(sources, stage, poslo_ref, poshi_ref, lo_ref, hi_ref):
    n_col = stage.shape[0]
    row = 0
    for src in sources:
        n = src.shape[0]
        for c in range(n_col):
            stage[c, row:row + n, :] = src[:, c * LANES:(c + 1) * LANES]
        row += n
    n16 = row // CMP_STRIDE
    for r in range(CMP_STRIDE):
        x = jnp.concatenate([stage[c, pl.ds(r, n16, stride=CMP_STRIDE), :] for c in range(n_col)], axis=1)
        lo_ref[r] = (x + poslo_ref[r:r + 1]).astype(lo_ref.dtype)
        hi_ref[r] = (x + poshi_ref[r:r + 1]).astype(hi_ref.dtype)


def _reorder_dense_body(x_ref, poslo_ref, poshi_ref, lo_ref, hi_ref, stage):
    _reorder_emit([x_ref], stage, poslo_ref, poshi_ref, lo_ref, hi_ref)


def _reorder_paged_body(pt_ref, *refs, n_pages):
    del pt_ref
    pages = refs[:n_pages]
    poslo_ref, poshi_ref, lo_ref, hi_ref, stage = refs[n_pages:]
    _reorder_emit(pages, stage, poslo_ref, poshi_ref, lo_ref, hi_ref)


def reorder_dense(kv, poslo, poshi):
    bn, s, _ = kv.shape
    n16 = s // CMP_STRIDE
    shp = jax.ShapeDtypeStruct((bn, CMP_STRIDE, n16, 512), BF16)
    ospec = pl.BlockSpec((None, CMP_STRIDE, n16, 512), lambda b: (b, 0, 0, 0))
    pspec = pl.BlockSpec((CMP_STRIDE, 512), lambda b: (0, 0))
    return pl.pallas_call(
        _reorder_dense_body,
        grid=(bn,),
        in_specs=[pl.BlockSpec((None, s, 512), lambda b: (b, 0, 0)), pspec, pspec],
        out_specs=[ospec, ospec],
        out_shape=[shp, shp],
        scratch_shapes=[pltpu.VMEM((512 // LANES, s, LANES), F32)],
        compiler_params=_cparams("arbitrary"),
        name="reorder_dense",
    )(kv, poslo, poshi)


def reorder_paged(cache, page_table, poslo, poshi):
    bn, n_pages = page_table.shape
    page = cache.shape[1]
    per_page = page // CMP_STRIDE
    n16 = n_pages * per_page
    shp = jax.ShapeDtypeStruct((bn, CMP_STRIDE, n16, 512), BF16)
    ospec = pl.BlockSpec((None, CMP_STRIDE, n16, 512), lambda b, pt: (b, 0, 0, 0))
    pspec = pl.BlockSpec((CMP_STRIDE, 512), lambda b, pt: (0, 0))
    page_specs = [pl.BlockSpec((None, page, 512), functools.partial(lambda b, pt, k: (pt[b * n_pages + k], 0, 0), k=k))
                  for k in range(n_pages)]
    return pl.pallas_call(
        functools.partial(_reorder_paged_body, n_pages=n_pages),
        grid_spec=pltpu.PrefetchScalarGridSpec(
            num_scalar_prefetch=1, grid=(bn,),
            in_specs=page_specs + [pspec, pspec],
            out_specs=[ospec, ospec],
            scratch_shapes=[pltpu.VMEM((512 // LANES, n_pages * page, LANES), F32)]),
        out_shape=[shp, shp],
        compiler_params=_cparams("arbitrary"),
        name="reorder_paged",
    )(page_table.reshape(-1), *([cache] * n_pages), poslo, poshi)


def _compress_body(lo_ref, hi_ref, wlo_ref, whi_ref, w2_ref, o_ref, acc1, acc2, *, rows):
    r = pl.program_id(2)

    @pl.when(r == 0)
    def _():
        acc1[...] = jnp.zeros_like(acc1)
        acc2[...] = jnp.zeros_like(acc2)

    a1 = acc1[...]
    a2 = acc2[...]
    for k in range(CMP_RSTEP):
        a1 = a1 + _dot(lo_ref[:, k].reshape(rows, 256), wlo_ref[k])
        a2 = a2 + _dot(hi_ref[:, k].reshape(rows, 256), whi_ref[k])
    acc1[...] = a1
    acc2[...] = a2

    @pl.when(r == CMP_STRIDE // CMP_RSTEP - 1)
    def _():
        hidden = _gelu_tanh(acc1[...] + pltpu.roll(acc2[...], rows - 1, 0))
        o_ref[...] = _dot(hidden.astype(BF16), w2_ref[...]).reshape(o_ref.shape).astype(o_ref.dtype)


def compress(xlo, xhi, wlo, whi, w2bd, sb):
    bn, _, n16, _ = xlo.shape
    assert bn % sb == 0
    rows = sb * n16
    xspec = pl.BlockSpec((sb, CMP_RSTEP, n16, 256), lambda g, s, r: (g, r, 0, s))
    wspec = pl.BlockSpec((None, CMP_RSTEP, 256, 512), lambda g, s, r: (s, r, 0, 0))
    return pl.pallas_call(
        functools.partial(_compress_body, rows=rows),
        grid=(bn // sb, 2, CMP_STRIDE // CMP_RSTEP),
        in_specs=[xspec, xspec, wspec, wspec, pl.BlockSpec((None, 512, 256), lambda g, s, r: (s, 0, 0))],
        out_specs=pl.BlockSpec((sb, n16, 256), lambda g, s, r: (g, 0, s)),
        out_shape=jax.ShapeDtypeStruct((bn, n16, 512), BF16),
        scratch_shapes=[pltpu.VMEM((rows, 512), F32), pltpu.VMEM((rows, 512), F32)],
        compiler_params=_cparams("arbitrary", "arbitrary", "arbitrary"),
        name="compress",
    )(xlo, xhi, wlo, whi, w2bd)


def _assemble_body(pt_ref, *refs, n_pages, page, s_new, win_len):
    del pt_ref
    pages = refs[:n_pages]
    new_slc_ref, new_win_ref, cwin_ref, slc_ref, win_ref = refs[n_pages:]
    for k in range(n_pages):
        slc_ref[k * page:(k + 1) * page, :] = pages[k][...].astype(slc_ref.dtype)
    pad = jnp.zeros((SLC_LEN - s_new, 512), F32)
    slc_ref[n_pages * page:n_pages * page + SLC_LEN, :] = jnp.concatenate(
        [new_slc_ref[...], pad], axis=0).astype(slc_ref.dtype)
    win_ref[0:win_len, :] = cwin_ref[...].astype(win_ref.dtype)
    win_ref[win_len:win_len + SLC_LEN, :] = jnp.concatenate([new_win_ref[...], pad], axis=0).astype(win_ref.dtype)


def assemble_sample_kv(cache, page_table, kv_new, cache_win):
    bn, n_pages = page_table.shape
    page = cache.shape[1]
    s_new = kv_new.shape[1]
    win_len = cache_win.shape[1]
    past = n_pages * page
    page_specs = [pl.BlockSpec((None, page, 512), functools.partial(lambda b, pt, k: (pt[b * n_pages + k], 0, 1), k=k))
                  for k in range(n_pages)]
    return pl.pallas_call(
        functools.partial(_assemble_body, n_pages=n_pages, page=page, s_new=s_new, win_len=win_len),
        grid_spec=pltpu.PrefetchScalarGridSpec(
            num_scalar_prefetch=1, grid=(bn,),
            in_specs=page_specs + [pl.BlockSpec((None, s_new, 512), lambda b, pt: (b, 0, 1)),
                                   pl.BlockSpec((None, s_new, 512), lambda b, pt: (b, 0, 2)),
                                   pl.BlockSpec((None, win_len, 512), lambda b, pt: (b, 0, 0))],
            out_specs=[pl.BlockSpec((None, past + SLC_LEN, 512), lambda b, pt: (b, 0, 0)),
                       pl.BlockSpec((None, win_len + SLC_LEN, 512), lambda b, pt: (b, 0, 0))]),
        out_shape=[jax.ShapeDtypeStruct((bn, past + SLC_LEN, 512), BF16),
                   jax.ShapeDtypeStruct((bn, win_len + SLC_LEN, 512), BF16)],
        compiler_params=_cparams("arbitrary"),
        name="assemble_sample_kv",
    )(page_table.reshape(-1), *([cache] * n_pages), kv_new, kv_new, cache_win)


def _value_heads(vv):
    own_a = lax.broadcasted_iota(jnp.int32, vv.shape, 1) < HEAD_DIM
    return jnp.where(own_a, vv, 1.0), jnp.where(own_a, 1.0, vv)


def _softmax_update(state, s, vv, rr):
    m, acc = state
    mn = jnp.maximum(m, jnp.max(s, axis=-1, keepdims=True))
    e = jnp.exp2(s - mn).astype(BF16)
    va, vb = _value_heads(vv)
    pv = jnp.concatenate([_dot(e[:rr], va), _dot(e[rr:], vb)], axis=0)
    return mn, jnp.exp2(m - mn) * acc + pv


def _softmax_finish(state):
    _, acc = state
    return acc / pltpu.roll(acc, HEAD_DIM, 1)


def _attn_seq(qi, q_ref, gl_ref, kc_ref, vc_ref, ks_ref, vs_ref, kw_ref, vw_ref, augk_ref, augw_ref,
              ovt_ref, cb_ref, nb_ref, wb_ref, augc_ref, ex_ref, o_ref,
              *, qb, n_slc, top_n, pad_s, t0, multi_block, n_streams):
    rr = GQA * qb
    r2 = 2 * rr
    pairs = range(N_PAIRS)
    is_a = lax.broadcasted_iota(jnp.int32, (rr, LANES), 1) < HEAD_DIM
    if multi_block:
        cur = t0 // SLC_LEN + qi
        win_start = pl.multiple_of(qi * qb, SLC_LEN)
    else:
        cur = t0 // SLC_LEN
        win_start = 0

    def cols(ref, p, start, size):
        return ref[pl.ds(start, size), p * LANES:(p + 1) * LANES]

    def init():
        return jnp.full((r2, 1), -jnp.inf, F32), jnp.zeros((r2, LANES), F32)

    wlen = WINDOW + SLC_LEN
    augw = augw_ref[pl.ds(win_start, wlen), :]
    glt = jnp.concatenate([gl_ref[...]] * GQA, axis=0)
    slab = lax.broadcasted_iota(jnp.int32, (rr, LANES), 0) // qb
    gsrc = jnp.concatenate([jnp.where(slab == g, glt, 0.0) for g in range(GQA)], axis=1)
    ghi = gsrc.astype(BF16)
    glo = (gsrc - ghi.astype(F32)).astype(BF16)

    q2, o_win, gates, o_cmp, imp_t = [], [], [], [], []
    for p in pairs:
        qfull = q_ref[:, p * GQA * LANES:(p + 1) * GQA * LANES]
        qs = jnp.concatenate([qfull[:, g * LANES:(g + 1) * LANES] for g in range(GQA)], axis=0)
        qs = qs * (HEAD_DIM ** -0.5 * LOG2E)
        q2.append(jnp.concatenate([jnp.where(is_a, qs, 0.0), jnp.where(is_a, 0.0, qs)], axis=0).astype(BF16))

        lhs_win = jnp.concatenate([q2[p], augc_ref[p, 1].astype(BF16)], axis=1)
        kk = jnp.concatenate([cols(kw_ref, p, win_start, wlen), augw], axis=1)
        o_win.append(_softmax_finish(_softmax_update(init(), _dot_nt(lhs_win, kk) + wb_ref[p],
                                                     cols(vw_ref, p, win_start, wlen), rr)))

        gates.append(jax.nn.sigmoid(_dot(ghi, ex_ref[p]) + _dot(glo, ex_ref[p])))

        s = _dot_nt(q2[p], kc_ref[:, p * LANES:(p + 1) * LANES]) + cb_ref[p]
        m = jnp.max(s, axis=-1, keepdims=True)
        e = jnp.exp2(s - m)
        pr = e / jnp.sum(e, axis=-1, keepdims=True)
        pr = jnp.where(m > 0.1 * NEG, pr, 0.0)
        o_cmp.append(_dot(pr.astype(BF16), vc_ref[:, p * LANES:(p + 1) * LANES]))

        pcat = jnp.concatenate(
            [jnp.concatenate([pr[x * rr + g * qb:x * rr + (g + 1) * qb] for g in range(GQA)], axis=1)
             for x in range(2)], axis=0)
        if 2 * qb < LANES:
            pcat = jnp.concatenate([pcat, jnp.zeros((LANES - 2 * qb, pcat.shape[1]), F32)], axis=0)
        imp_t.append(_dot_nt(ovt_ref[...], pcat.astype(BF16)))

    def ranked():
        jr = lax.broadcasted_iota(jnp.int32, (N_BLOCK_LANES, LANES), 0)
        forced = (jr == 0) | (jr == cur) | (jr == cur - 1)
        res = []
        for p in pairs:
            v = jnp.where(forced, FORCE, jnp.where((jr > cur) | (jr >= n_slc), NEG, imp_t[p][:N_BLOCK_LANES]))
            rank = jnp.zeros((N_BLOCK_LANES, LANES), F32)
            for k in range(n_slc):
                vk = v[k:k + 1, :]
                rank = rank + jnp.where(vk > v, 1.0, jnp.where(vk == v, jnp.where(jr > k, 1.0, 0.0), 0.0))
            res.append(jnp.where(rank < top_n, 0.0, 1.0))
        return tuple(res)

    if multi_block:
        all_selected = lambda: tuple(jnp.zeros((N_BLOCK_LANES, LANES), F32) for _ in pairs)
        notsel_t = lax.cond(cur >= top_n, ranked, all_selected)
    else:
        notsel_t = ranked()

    st = pad_s + (cur - 2) * SLC_LEN
    if multi_block:
        st = pl.multiple_of(st, SLC_LEN)
    augk_near = augk_ref[pl.ds(st, 3 * SLC_LEN), :]
    lane2 = lax.broadcasted_iota(jnp.int32, (r2, LANES), 1)
    lhs_far, near = [], []
    for p in pairs:
        notsel = jnp.concatenate([notsel_t[p], jnp.zeros((LANES - N_BLOCK_LANES, LANES), F32)], axis=0).T
        ns = jnp.concatenate([notsel[x * qb:(x + 1) * qb] for x in range(2) for _ in range(GQA)], axis=0)
        ns_far = jnp.where((lane2 >= cur - 2) & (lane2 < N_BLOCK_LANES), 1.0, ns)
        lhs_far.append(jnp.concatenate([q2[p], (ns_far + augc_ref[p, 0]).astype(BF16)], axis=1))
        lhs_near = jnp.concatenate([q2[p], (ns + augc_ref[p, 1]).astype(BF16)], axis=1)
        kk = jnp.concatenate([cols(ks_ref, p, st, 3 * SLC_LEN), augk_near], axis=1)
        near.append(_softmax_update(init(), _dot_nt(lhs_near, kk) + nb_ref[p], cols(vs_ref, p, st, 3 * SLC_LEN), rr))

    def far_step(it, states):
        new = []
        for k in range(n_streams):
            st = pl.multiple_of(pad_s + (it * n_streams + k) * FAR_CHUNK, FAR_CHUNK)
            augk = augk_ref[pl.ds(st, FAR_CHUNK), :]
            for p in pairs:
                kk = jnp.concatenate([cols(ks_ref, p, st, FAR_CHUNK), augk], axis=1)
                new.append(_softmax_update(states[k * N_PAIRS + p], _dot_nt(lhs_far[p], kk),
                                           cols(vs_ref, p, st, FAR_CHUNK), rr))
        return tuple(new)

    per_iter = n_streams * (FAR_CHUNK // SLC_LEN)
    states = tuple(init() for _ in range(n_streams * N_PAIRS))
    if multi_block:
        states = lax.fori_loop(0, (jnp.maximum(cur - 2, 0) + per_iter - 1) // per_iter, far_step, states)
    else:
        for it in range((max(cur - 2, 0) + per_iter - 1) // per_iter):
            states = far_step(it, states)

    outs = []
    for p in pairs:
        mine = states[p::N_PAIRS] + (near[p],)
        m_all = functools.reduce(jnp.maximum, [m for m, _ in mine])
        o_slc = _softmax_finish((m_all, sum(acc * jnp.exp2(m - m_all) for m, acc in mine)))
        out = None
        for br, o2 in enumerate((o_cmp[p], o_slc, o_win[p])):
            term = gates[p][:, br * LANES:(br + 1) * LANES] * jnp.where(is_a, o2[:rr], o2[rr:])
            out = term if out is None else out + term
        outs += [out[g * qb:(g + 1) * qb] for g in range(GQA)]
    o_ref[...] = jnp.concatenate(outs, axis=1).astype(o_ref.dtype)


def _attn_body(*refs, sbq, **kw):
    qi = pl.program_id(1)
    per_seq, shared = refs[:8], refs[8:16]
    for i in range(sbq):
        _attn_seq(qi, *[r.at[i] for r in per_seq], *shared, refs[16].at[i], **kw)


def _cmp_bias_body(cbe_ref, o_ref, *, nc, step):
    tau = pl.program_id(0) * step
    for p in range(N_PAIRS):
        o_ref[p] = pltpu.roll(cbe_ref[p], tau, 1)[:, nc:]


def cmp_bias_per_block(cbe, nqb, step):
    n_p, rows, nc2 = cbe.shape
    nc = nc2 // 2
    return pl.pallas_call(
        functools.partial(_cmp_bias_body, nc=nc, step=step),
        grid=(nqb,),
        in_specs=[pl.BlockSpec(cbe.shape, lambda i: (0, 0, 0))],
        out_specs=pl.BlockSpec((None, n_p, rows, nc), lambda i: (i, 0, 0, 0)),
        out_shape=jax.ShapeDtypeStruct((nqb, n_p, rows, nc), F32),
        compiler_params=_cparams("arbitrary"),
        name="cmp_bias_per_block",
    )(cbe)


def nsa_attention(u, cmp_kv, slc_kv, slc_cols, win_kv, win_cols, augk, augw, ovt, tiles, ex,
                  *, qb, n_slc, pad_s, t0, sbq):
    bn, s, _ = u.shape
    nqb = s // qb
    nc = cmp_kv.shape[1]
    r2 = 2 * GQA * qb
    cb, nb, wb, augc = tiles
    srows = slc_kv.shape[1]
    wrows = win_kv.shape[1]
    kcol, vcol = slc_cols
    kwcol, vwcol = win_cols
    assert qb == SLC_LEN or nqb == 1
    assert t0 % SLC_LEN == 0 and bn % sbq == 0
    per_chunk = FAR_CHUNK // SLC_LEN
    max_chunks = -(-max(n_slc - 3, 0) // per_chunk)
    n_streams = min(4, max_chunks) if nqb > 1 else max_chunks
    n_streams = max(n_streams, 1)
    assert pad_s + -(-max_chunks // n_streams) * n_streams * FAR_CHUNK <= srows
    body = functools.partial(_attn_body, sbq=sbq, qb=qb, n_slc=n_slc, top_n=min(N_SELECT, n_slc),
                             pad_s=pad_s, t0=t0, multi_block=nqb > 1, n_streams=n_streams)
    whole = lambda a: pl.BlockSpec(a.shape, lambda b, i: (0,) * a.ndim)
    kvw = N_PAIRS * LANES
    nq = N_HEADS * HEAD_DIM
    return pl.pallas_call(
        body,
        grid=(bn // sbq, nqb),
        in_specs=[pl.BlockSpec((sbq, qb, nq), lambda b, i: (b, i, 0)),
                  pl.BlockSpec((sbq, qb, LANES), lambda b, i: (b, i, nq // LANES)),
                  pl.BlockSpec((sbq, nc, kvw), lambda b, i: (b, 0, 0)),
                  pl.BlockSpec((sbq, nc, kvw), lambda b, i: (b, 0, 1)),
                  pl.BlockSpec((sbq, srows, kvw), lambda b, i: (b, 0, kcol)),
                  pl.BlockSpec((sbq, srows, kvw), lambda b, i: (b, 0, vcol)),
                  pl.BlockSpec((sbq, wrows, kvw), lambda b, i: (b, 0, kwcol)),
                  pl.BlockSpec((sbq, wrows, kvw), lambda b, i: (b, 0, vwcol)),
                  whole(augk), whole(augw), whole(ovt),
                  pl.BlockSpec((None, N_PAIRS, r2, nc), lambda b, i: (i, 0, 0, 0)),
                  whole(nb), whole(wb), whole(augc), whole(ex)],
        out_specs=pl.BlockSpec((sbq, qb, nq), lambda b, i: (b, i, 0)),
        out_shape=jax.ShapeDtypeStruct((bn, s, nq), F32),
        compiler_params=_cparams("arbitrary", "arbitrary"),
        name="nsa_attention",
    )(u, u, cmp_kv, cmp_kv, slc_kv, slc_kv, win_kv, win_kv, augk, augw, ovt, cb, nb, wb, augc, ex)


def _head_perm():
    idx = np.empty((N_PAIRS, GQA, 2, HEAD_DIM), np.int32)
    for p in range(N_PAIRS):
        for g in range(GQA):
            for half in range(2):
                head = (2 * p + half) * GQA + g
                idx[p, g, half] = head * HEAD_DIM + np.arange(HEAD_DIM)
    return idx.reshape(-1)


def _gate_expand():
    ex = np.zeros((N_PAIRS, GQA * LANES, 3 * LANES), np.float32)
    for p in range(N_PAIRS):
        for g in range(GQA):
            for half in range(2):
                head = (2 * p + half) * GQA + g
                for br in range(3):
                    ex[p, g * LANES + head * 3 + br, br * LANES + half * HEAD_DIM:br * LANES + (half + 1) * HEAD_DIM] = 1.0
    return ex


def _overlap_t(nc, n_cmp, n_slc):
    i = np.arange(nc)[None, :] * CMP_STRIDE
    j = np.arange(LANES)[:, None] * SLC_LEN
    ov = (i < j + SLC_LEN) & (i + CMP_LEN > j) & (np.arange(nc)[None, :] < n_cmp) & (np.arange(LANES)[:, None] < n_slc)
    return np.tile(ov.astype(np.float32), (1, GQA))


def _aug_keys(pad, n_keys):
    a = np.zeros((pad + n_keys, LANES), np.float32)
    a[:pad, AUG_PAD] = NEG
    k = np.arange(n_keys)
    a[pad + k, k // SLC_LEN] = NEG
    a[pad:, AUG_BIAS_HI] = 1.0
    a[pad:, AUG_BIAS_LO] = 1.0
    return a


def _attn_tiles(rel_bias, qb, t0, nc, nqb):
    rolled = nqb > 1
    i = np.arange(qb)[:, None]
    near = np.arange(3 * SLC_LEN)[None, :]
    d_near = SLC_LEN * (2 - near // SLC_LEN) + i - near % SLC_LEN
    c = np.arange(WINDOW + SLC_LEN)[None, :]
    d_win = i + WINDOW - c
    d_win = np.where(d_win < WINDOW, d_win, -1)
    if rolled:
        e = np.arange(2 * nc)[None, :] - nc
        d_cmp = i - (CMP_LEN - 1) - CMP_STRIDE * e
    else:
        d_cmp = t0 + i - CMP_STRIDE * np.arange(nc)[None, :] - (CMP_LEN - 1)
    widths = [d_near.shape[1], d_win.shape[1], d_cmp.shape[1]]
    padded = [-(-w // LANES) * LANES for w in widths]
    dist = np.full((qb, sum(padded)), -1, np.int32)
    off = 0
    offs = []
    for d, w, pw in zip((d_near, d_win, d_cmp), widths, padded):
        dist[:, off:off + w] = np.clip(d, -1, 4 * MAX_DIST)
        offs.append(off)
        off += pw
    tiles = rel_bias_tiles(rel_bias, jnp.asarray(dist))
    r2 = 2 * GQA * qb
    tiles = tiles.reshape(N_PAIRS, r2, dist.shape[1]) * LOG2E
    nb, wb, cbe = (tiles[..., o:o + w] for o, w in zip(offs, widths))
    if rolled:
        cb = cmp_bias_per_block(cbe, nqb, (t0 + qb) // CMP_STRIDE - t0 // CMP_STRIDE)
    else:
        cb = cbe[None]
    far = rel_bias[N_BUCKETS - 1].reshape(N_PAIRS, 2 * GQA) * LOG2E
    hi = far.astype(BF16).astype(F32)
    lo = far - hi
    rows = jnp.repeat(jnp.stack([hi, lo], axis=-1), qb, axis=1)
    augc = jnp.zeros((N_PAIRS, 2, r2, LANES), F32)
    augc = augc.at[:, 0, :, AUG_BIAS_HI].set(rows[..., 0]).at[:, 0, :, AUG_BIAS_LO].set(rows[..., 1])
    augc = augc.at[:, 1, :, AUG_PAD].set(1.0)
    return cb, nb, wb, augc


def _forward(x, ada, ada_kv, conv_state, h_state, past, w, shared):
    bn, s, _ = x.shape
    m = bn * s
    xf = x.reshape(m, D_MODEL)
    new_conv, new_h = [], []
    h = normmod(xf, w["norm_g"][0, 0], ada[0], 0, 1, bn, s)
    kv = None
    attn_args = None
    for l in range(DEPTH):
        if l < N_A_LAYERS:
            u = mm_plain(h, w["w_in_a"][l]).reshape(bn, s, 2 * D_RNN)
            conv8 = jnp.pad(conv_state[l], ((0, 0), (8 - (CONV_W - 1), 0), (0, 0)))
            y, conv_o, h_o = rglru(u, conv8, h_state[l][:, None, :], w["conv_w"][l], w["conv_b"][l],
                                   w["w_rg_a"][l], w["b_rg_a"][l], w["w_rg_x"][l], w["b_rg_x"][l], w["lru_lambda"][l])
            new_conv.append(conv_o[:, 8 - (CONV_W - 1):])
            new_h.append(h_o[:, 0])
            a_in = y.reshape(m, D_RNN)
            w_out = w["w_out_a"][l]
        else:
            lb = l - N_A_LAYERS
            if lb == 0:
                kv = mm_plain(h_kv, w["w_kv"]).reshape(bn, s, 6 * N_KV_HEADS * HEAD_DIM)
                attn_args = _prepare_attention(kv, past, w, shared, bn, s)
            u = mm_plain(h, w["w_in_b"][lb]).reshape(bn, s, -1)
            a_in = nsa_attention(u, *attn_args[0], **attn_args[1]).reshape(m, N_HEADS * HEAD_DIM)
            w_out = w["w_out_b"][lb]
        x_new, (h2,) = mm_resnorm(a_in, w_out, xf, (ada[l], 2), w["norm_g"][l, 1],
                                  [(w["norm_g"][l, 2], ada[l], 3, ada[l], 4)], bn, s)
        f = mm_swiglu(h2, w["w_ffn_in"][l])
        heads = []
        if l + 1 < DEPTH:
            heads.append((w["norm_g"][l + 1, 0], ada[l + 1], 0, ada[l + 1], 1))
        if l + 1 == N_A_LAYERS:
            heads.append((w["norm_kv"], ada_kv, 0, ada_kv, 1))
        xf, hs = mm_resnorm(f, w["w_ffn_out"][l], x_new, (ada[l], 5), w["norm_g"][l, 3], heads, bn, s)
        if heads:
            h = hs[0]
        if l + 1 == N_A_LAYERS:
            h_kv = hs[1]
    kv_rows = kv[:, :, :4 * N_KV_HEADS * HEAD_DIM].reshape(bn, s, 4, N_KV_HEADS, HEAD_DIM)
    new_win = kv[:, :, 4 * N_KV_HEADS * HEAD_DIM:].reshape(bn, s, 2, N_KV_HEADS, HEAD_DIM)
    if past is None:
        win_state = new_win[:, -min(WINDOW, s):]
    else:
        cache_win = past[2]
        win_state = jnp.concatenate([cache_win, new_win], axis=1)[:, -cache_win.shape[1]:]
    return xf.reshape(bn, s, D_MODEL), kv_rows, win_state, jnp.stack(new_conv), jnp.stack(new_h)


def _prepare_attention(kv, past, w, shared, bn, s):
    rel_bias = w["rel_bias"]
    if past is None:
        t0, qb, pad = 0, Q_BLOCK, WINDOW
        n_keys = s
        xlo, xhi = reorder_dense(kv, shared["poslo"], shared["poshi"])
        kvb = jnp.pad(kv.astype(BF16), ((0, 0), (pad, 0), (0, 0)))
        slc_kv = win_kv = kvb
        slc_cols, win_cols = (2, 3), (4, 5)
        augk = jnp.asarray(_aug_keys(pad, n_keys), BF16)
        augw = augk
        sb = bn
    else:
        cache, page_table, cache_win = past
        n_pool, page = cache.shape[:2]
        cache2 = cache.reshape(n_pool, page, -1)
        t0 = page_table.shape[1] * page
        qb, pad = s, 0
        n_keys = t0 + SLC_LEN
        assert cache_win.shape[1] == WINDOW and t0 >= WINDOW
        xlo, xhi = reorder_paged(cache2, page_table, shared["poslo"], shared["poshi"])
        slc_kv, win_kv = assemble_sample_kv(cache2, page_table, kv, cache_win.reshape(bn, WINDOW, -1))
        slc_cols, win_cols = (0, 1), (0, 1)
        augk = jnp.asarray(_aug_keys(0, n_keys), BF16)
        augw = jnp.zeros((win_kv.shape[1], LANES), BF16)
        sb = 8
    n16 = (t0 + s) // CMP_STRIDE
    n_cmp = n16 - CMP_LEN // CMP_STRIDE + 1
    nc = xlo.shape[2]
    assert nc >= n_cmp and nc % LANES == 0
    n_slc = -(-(t0 + s) // SLC_LEN)
    assert n_slc <= N_BLOCK_LANES
    cmp_kv = compress(xlo, xhi, shared["wlo"], shared["whi"], shared["w2bd"], sb)
    nqb = s // qb
    assert nqb == 1 or t0 == 0
    tiles = _attn_tiles(rel_bias, qb, t0, nc, nqb)
    ovt = jnp.asarray(_overlap_t(nc, n_cmp, n_slc), BF16)
    args = (cmp_kv, slc_kv, slc_cols, win_kv, win_cols, augk, augw, ovt, tiles, shared["ex"])
    return args, dict(qb=qb, n_slc=n_slc, pad_s=pad, t0=t0, sbq=1 if nqb > 1 else min(bn, 2))


def kernel(x_prompt, x_sample, c_prompt, c_sample, cache_kv, cache_win, state_conv, state_h, page_table, w_ada, b_ada, norm_g, w_in_a, conv_w, conv_b, w_rg_a, b_rg_a, w_rg_x, b_rg_x, lru_lambda, w_out_a, w_ada_kv, b_ada_kv, norm_kv, w_kv, cmp_pos, cmp_w1, cmp_w2, w_in_b, w_out_b, rel_bias, w_ffn_in, w_ffn_out):
    bp = x_prompt.shape[0]
    bs = x_sample.shape[0]

    perm = _head_perm()
    n_q = N_HEADS * HEAD_DIM
    w_in_b_p = jnp.concatenate([w_in_b[:, :, :n_q][:, :, perm], w_in_b[:, :, n_q:],
                                jnp.zeros(w_in_b.shape[:2] + (LANES - 3 * N_HEADS,), F32)], axis=-1)
    w = dict(norm_g=norm_g, norm_kv=norm_kv, conv_w=conv_w, conv_b=conv_b, b_rg_a=b_rg_a, b_rg_x=b_rg_x,
             lru_lambda=lru_lambda, rel_bias=rel_bias,
             w_in_a=w_in_a.astype(BF16), w_rg_a=w_rg_a.astype(BF16), w_rg_x=w_rg_x.astype(BF16),
             w_out_a=w_out_a.astype(BF16), w_kv=w_kv.astype(BF16), w_in_b=w_in_b_p.astype(BF16),
             w_out_b=w_out_b[:, perm, :].astype(BF16), w_ffn_in=w_ffn_in.astype(BF16), w_ffn_out=w_ffn_out.astype(BF16))

    eye = jnp.eye(N_KV_HEADS, dtype=F32)
    w1r = cmp_w1.reshape(2, CMP_LEN, HEAD_DIM, CMP_HIDDEN)
    w1bd = jnp.einsum("hk,srdj->srhdkj", eye, w1r).reshape(2, CMP_LEN, N_KV_HEADS * HEAD_DIM, N_KV_HEADS * CMP_HIDDEN)
    w2bd = jnp.einsum("hk,sjd->shjkd", eye, cmp_w2).reshape(2, N_KV_HEADS * CMP_HIDDEN, N_KV_HEADS * HEAD_DIM)
    pos = jnp.broadcast_to(cmp_pos[:, :, None, :], (2, CMP_LEN, N_KV_HEADS, HEAD_DIM)).reshape(2, CMP_LEN, -1)
    pos = jnp.concatenate([pos[0], pos[1]], axis=-1)
    shared = dict(wlo=w1bd[:, :CMP_STRIDE].astype(BF16), whi=w1bd[:, CMP_STRIDE:].astype(BF16),
                  w2bd=w2bd.astype(BF16), poslo=pos[:CMP_STRIDE], poshi=pos[CMP_STRIDE:],
                  ex=jnp.asarray(_gate_expand(), BF16))

    n_c = bp + bs
    c_all = jnp.concatenate([c_prompt, c_sample, jnp.zeros((-n_c % 8, D_MODEL), F32)], axis=0)
    ada_all = [mm_plain(c_all, w_ada[l].astype(BF16), bias=b_ada[l], act="silu", tn=2048) for l in range(DEPTH)]
    ada_kv_all = mm_plain(c_all, w_ada_kv.astype(BF16), bias=b_ada_kv, act="silu", tn=2048)
    ada_p = [a[:bp, None, :] for a in ada_all]
    ada_s = [a[bp:n_c, None, :] for a in ada_all]

    zero_conv = jnp.zeros((N_A_LAYERS, bp, CONV_W - 1, D_RNN), F32)
    zero_h = jnp.zeros((N_A_LAYERS, bp, D_RNN), F32)
    y_p, kv_p, win_p, conv_p, h_p = _forward(x_prompt, ada_p, ada_kv_all[:bp, None, :], zero_conv, zero_h, None, w, shared)
    y_s, kv_s, win_s, conv_s, h_s = _forward(x_sample, ada_s, ada_kv_all[bp:n_c, None, :], state_conv, state_h,
                                             (cache_kv, page_table, cache_win), w, shared)
    return (y_p, y_s, kv_p, kv_s, win_p, win_s, conv_p, conv_s, h_p, h_s)
```

```python
import functools
import math

import numpy as np
import jax
import jax.numpy as jnp
from jax import lax
from jax.experimental import pallas as pl
from jax.experimental.pallas import tpu as pltpu

D_MODEL = 1024
DEPTH = 4
N_A_LAYERS = 2
D_RNN = 1024
LRU_BLOCKS = 8
LRU_BW = 128
CONV_W = 4
LRU_C = 8.0
N_HEADS = 16
HEAD_DIM = 64
N_KV_HEADS = 4
GQA = 4
CMP_LEN = 32
CMP_STRIDE = 16
CMP_HIDDEN = 128
SLC_LEN = 64
N_SELECT = 16
WINDOW = 512
Q_BLOCK = 64
N_BUCKETS = 32
MAX_DIST = 128
D_FF = 2816
EPS = 1e-6
NEG = -1e30
FORCE = 1e9

LANES = 128
PAIR_W = 2 * HEAD_DIM
N_PAIRS = N_KV_HEADS // 2
AUG_BIAS_HI = 64
AUG_BIAS_LO = 65
AUG_PAD = 66
N_BLOCK_LANES = 64
FAR_CHUNK = 1024
CMP_RSTEP = 4
LOG2E = 1.4426950408889634
VMEM_LIMIT = 56 * 1024 * 1024

F32 = jnp.float32
BF16 = jnp.bfloat16


def _cparams(*sem):
    return pltpu.CompilerParams(dimension_semantics=sem, vmem_limit_bytes=VMEM_LIMIT)


def _dot(a, b):
    return jnp.dot(a, b, preferred_element_type=F32)


def _dot_nt(a, b):
    return lax.dot_general(a, b, (((1,), (1,)), ((), ())), preferred_element_type=F32)


def _gelu_tanh(x):
    return 0.5 * x * (1.0 + jnp.tanh(math.sqrt(2.0 / math.pi) * (x + 0.044715 * (x * x * x))))


def _rms(x, gain):
    return x * lax.rsqrt(jnp.mean(x * x, axis=-1, keepdims=True) + EPS) * gain


def _rows(vec_ref, bb, ts):
    v = vec_ref[...]
    d = v.shape[-1]
    return jnp.broadcast_to(v, (bb, ts, d)).reshape(bb * ts, d)


def _vec_spec(bb, tiles_per_seq, col, ngrid):
    if ngrid == 1:
        return pl.BlockSpec((bb, 1, D_MODEL), lambda i: (i // tiles_per_seq, 0, col))
    return pl.BlockSpec((bb, 1, D_MODEL), lambda j, i: (i // tiles_per_seq, 0, col))


def _row_tiling(bn, s, tm=512):
    if s >= tm:
        assert s % tm == 0
        return tm, 1, tm, s // tm
    assert tm % s == 0 and s % 8 == 0
    bb = min(bn, tm // s)
    assert bn % bb == 0
    return bb * s, bb, s, 1


def _bf16_weight(w_ref, cache_ref, first_row_tile):
    if cache_ref is None:
        return w_ref[...].astype(BF16)

    @pl.when(first_row_tile)
    def _():
        cache_ref[...] = w_ref[...].astype(BF16)

    return cache_ref[...]


def _weight_cache(w, block, n_row_tiles):
    return [pltpu.VMEM(block, BF16)] if (w.dtype != BF16 and n_row_tiles > 1) else []


def _mm_body(*refs, act, has_bias, cached):
    cache_ref = refs[-1] if cached else None
    refs = refs[:-1] if cached else refs
    if has_bias:
        a_ref, w_ref, b_ref, o_ref = refs
    else:
        a_ref, w_ref, o_ref = refs
    a = a_ref[...]
    if act == "silu":
        a = a.astype(F32)
        a = a * jax.nn.sigmoid(a)
    y = _dot(a.astype(BF16), _bf16_weight(w_ref, cache_ref, pl.program_id(1) == 0))
    if has_bias:
        y = y + b_ref[...]
    o_ref[...] = y.astype(o_ref.dtype)


def mm_plain(a, w, bias=None, act=None, tm=512, tn=None, out_dtype=F32):
    m, k = a.shape
    n = w.shape[1]
    tm = min(tm, m)
    tn = tn or n
    assert m % tm == 0 and n % tn == 0
    in_specs = [pl.BlockSpec((tm, k), lambda j, i: (i, 0)), pl.BlockSpec((k, tn), lambda j, i: (0, j))]
    args = [a, w]
    if bias is not None:
        in_specs.append(pl.BlockSpec((1, tn), lambda j, i: (0, j)))
        args.append(bias.reshape(1, n))
    cache = _weight_cache(w, (k, tn), m // tm)
    return pl.pallas_call(
        functools.partial(_mm_body, act=act, has_bias=bias is not None, cached=bool(cache)),
        grid=(n // tn, m // tm),
        in_specs=in_specs,
        out_specs=pl.BlockSpec((tm, tn), lambda j, i: (i, j)),
        out_shape=jax.ShapeDtypeStruct((m, n), out_dtype),
        scratch_shapes=cache,
        compiler_params=_cparams("arbitrary", "arbitrary"),
        name="mm_plain",
    )(*args)


def _swiglu_body(h_ref, wg_ref, wu_ref, o_ref, *caches):
    h = h_ref[...]
    first = pl.program_id(1) == 0
    g = _dot(h, _bf16_weight(wg_ref, caches[0] if caches else None, first))
    u = _dot(h, _bf16_weight(wu_ref, caches[1] if caches else None, first))
    o_ref[...] = (g * jax.nn.sigmoid(g) * u).astype(o_ref.dtype)


def mm_swiglu(h, w_in, tm=512, tn=1408):
    m, k = h.shape
    nf = w_in.shape[1] // 2
    tm = min(tm, m)
    assert nf % tn == 0 and m % tm == 0
    nj = nf // tn
    return pl.pallas_call(
        _swiglu_body,
        grid=(nj, m // tm),
        in_specs=[pl.BlockSpec((tm, k), lambda j, i: (i, 0)),
                  pl.BlockSpec((k, tn), lambda j, i: (0, j)),
                  pl.BlockSpec((k, tn), lambda j, i: (0, j + nj))],
        out_specs=pl.BlockSpec((tm, tn), lambda j, i: (i, j)),
        out_shape=jax.ShapeDtypeStruct((m, nf), BF16),
        scratch_shapes=_weight_cache(w_in, (k, tn), m // tm) * 2,
        compiler_params=_cparams("arbitrary", "arbitrary"),
        name="mm_swiglu",
    )(h, w_in, w_in)


def _normmod_body(x_ref, g_ref, sh_ref, sc_ref, o_ref, *, bb, ts):
    y = _rms(x_ref[...], g_ref[...])
    o_ref[...] = (y * (1.0 + _rows(sc_ref, bb, ts)) + _rows(sh_ref, bb, ts)).astype(o_ref.dtype)


def normmod(x, gain, ada, col_shift, col_scale, bn, s):
    m = x.shape[0]
    tm, bb, ts, tps = _row_tiling(bn, s)
    return pl.pallas_call(
        functools.partial(_normmod_body, bb=bb, ts=ts),
        grid=(m // tm,),
        in_specs=[pl.BlockSpec((tm, D_MODEL), lambda i: (i, 0)),
                  pl.BlockSpec((1, D_MODEL), lambda i: (0, 0)),
                  _vec_spec(bb, tps, col_shift, 1),
                  _vec_spec(bb, tps, col_scale, 1)],
        out_specs=pl.BlockSpec((tm, D_MODEL), lambda i: (i, 0)),
        out_shape=jax.ShapeDtypeStruct((m, D_MODEL), BF16),
        compiler_params=_cparams("arbitrary"),
        name="normmod",
    )(x, gain.reshape(1, D_MODEL), ada, ada)


def _resnorm_body(*refs, bb, ts, n_heads, cached):
    cache_ref = refs[-1] if cached else None
    refs = refs[:-1] if cached else refs
    a_ref, w_ref, x_ref, gate_ref, ngain_ref = refs[:5]
    head_refs = refs[5:5 + 3 * n_heads]
    xo_ref = refs[5 + 3 * n_heads]
    ho_refs = refs[6 + 3 * n_heads:]
    y = _dot(a_ref[...].astype(BF16), _bf16_weight(w_ref, cache_ref, pl.program_id(0) == 0))
    x = x_ref[...] + _rows(gate_ref, bb, ts) * _rms(y, ngain_ref[...])
    xo_ref[...] = x
    for k in range(n_heads):
        hg_ref, sh_ref, sc_ref = head_refs[3 * k:3 * k + 3]
        h = _rms(x, hg_ref[...]) * (1.0 + _rows(sc_ref, bb, ts)) + _rows(sh_ref, bb, ts)
        ho_refs[k][...] = h.astype(ho_refs[k].dtype)


def mm_resnorm(a, w, x, gate, ngain, heads, bn, s):
    m, k = a.shape
    tm, bb, ts, tps = _row_tiling(bn, s)
    row = lambda i: (i, 0)
    const = lambda i: (0, 0)
    in_specs = [pl.BlockSpec((tm, k), row), pl.BlockSpec((k, D_MODEL), const, pipeline_mode=pl.Buffered(1)),
                pl.BlockSpec((tm, D_MODEL), row), _vec_spec(bb, tps, gate[1], 1),
                pl.BlockSpec((1, D_MODEL), const)]
    args = [a, w, x, gate[0], ngain.reshape(1, D_MODEL)]
    for hg, sh_arr, sh_col, sc_arr, sc_col in heads:
        in_specs += [pl.BlockSpec((1, D_MODEL), const), _vec_spec(bb, tps, sh_col, 1), _vec_spec(bb, tps, sc_col, 1)]
        args += [hg.reshape(1, D_MODEL), sh_arr, sc_arr]
    out_shape = [jax.ShapeDtypeStruct((m, D_MODEL), F32)] + [jax.ShapeDtypeStruct((m, D_MODEL), BF16)] * len(heads)
    out_specs = [pl.BlockSpec((tm, D_MODEL), row)] * (1 + len(heads))
    cache = _weight_cache(w, (k, D_MODEL), m // tm)
    outs = pl.pallas_call(
        functools.partial(_resnorm_body, bb=bb, ts=ts, n_heads=len(heads), cached=bool(cache)),
        grid=(m // tm,),
        in_specs=in_specs,
        out_specs=out_specs,
        out_shape=out_shape,
        scratch_shapes=cache,
        compiler_params=_cparams("arbitrary"),
        name="mm_resnorm",
    )(*args)
    return outs[0], list(outs[1:])


def _rglru_body(u_ref, conv8_ref, hprev_ref, cw_ref, cb_ref, wa_ref, ba_ref, wx_ref, bx_ref, lam_ref,
                y_ref, convo_ref, ho_ref, carry_h, carry_conv, *, bb, tt):
    j = pl.program_id(1)

    @pl.when(j == 0)
    def _():
        carry_h[...] = hprev_ref[...]
        carry_conv[...] = conv8_ref[...]

    u = u_ref[...]
    xb = u[:, :, :D_RNN]
    yb = u[:, :, D_RNN:]
    ext = jnp.concatenate([carry_conv[...], xb], axis=1)
    cw = cw_ref[...]
    xc = cb_ref[...] + ext[:, 5:5 + tt] * cw[0:1]
    for k in range(1, CONV_W):
        xc = xc + ext[:, 5 + k:5 + k + tt] * cw[k:k + 1]
    last8 = xb[:, tt - 8:tt]
    carry_conv[...] = last8
    convo_ref[...] = last8

    xc2 = xc.reshape(bb * tt, D_RNN)
    xcb = xc2.astype(BF16)

    def gate(w_ref, b_ref):
        cols = [_dot(xcb[:, n * LRU_BW:(n + 1) * LRU_BW], w_ref[n]) for n in range(LRU_BLOCKS)]
        return jax.nn.sigmoid(jnp.concatenate(cols, axis=1) + b_ref[...])

    r = gate(wa_ref, ba_ref)
    ig = gate(wx_ref, bx_ref)
    nl = -lam_ref[...]
    z = jnp.exp(-jnp.abs(nl))
    w1 = 1.0 + z
    log1p_z = jnp.where(w1 == 1.0, z, jnp.log(w1) * (z / jnp.where(w1 == 1.0, 1.0, w1 - 1.0)))
    softplus = jnp.maximum(nl, 0.0) + log1p_z
    log_a = -LRU_C * r * softplus
    a = jnp.exp(log_a)
    b = jnp.sqrt(1.0 - a * a) * (ig * xc2)

    a = a.reshape(bb * tt // 8, 8, D_RNN)
    b = b.reshape(bb * tt // 8, 8, D_RNN)
    row8 = lax.broadcasted_iota(jnp.int32, a.shape, 1)
    for d in (1, 2, 4):
        a_s = pltpu.roll(a, d, 1)
        b_s = pltpu.roll(b, d, 1)
        keep = row8 >= d
        b = jnp.where(keep, a * b_s + b, b)
        a = jnp.where(keep, a * a_s, a)

    a3 = a.reshape(bb, tt, D_RNN)
    b3 = b.reshape(bb, tt, D_RNN)
    carry = carry_h[...]
    groups = []
    for g in range(tt // 8):
        hg = a3[:, 8 * g:8 * g + 8] * carry + b3[:, 8 * g:8 * g + 8]
        carry = hg[:, 7:8]
        groups.append(hg)
    hs = groups[0] if len(groups) == 1 else jnp.concatenate(groups, axis=1)
    carry_h[...] = carry
    ho_ref[...] = carry
    y_ref[...] = (hs * _gelu_tanh(yb)).astype(y_ref.dtype)


def rglru(u, conv8, hprev, cw, cb, wa, ba, wx, bx, lam):
    bn, s, _ = u.shape
    if s >= 256:
        bb, tt = 1, 256
    else:
        bb, tt = min(bn, 512 // s), s
    assert s % tt == 0 and bn % bb == 0 and tt % 8 == 0
    vec = lambda a: a.reshape(1, D_RNN)
    c2 = lambda i, j: (0, 0)
    c3 = lambda i, j: (0, 0, 0)
    return pl.pallas_call(
        functools.partial(_rglru_body, bb=bb, tt=tt),
        grid=(bn // bb, s // tt),
        in_specs=[pl.BlockSpec((bb, tt, 2 * D_RNN), lambda i, j: (i, j, 0)),
                  pl.BlockSpec((bb, 8, D_RNN), lambda i, j: (i, 0, 0)),
                  pl.BlockSpec((bb, 1, D_RNN), lambda i, j: (i, 0, 0)),
                  pl.BlockSpec((CONV_W, D_RNN), c2),
                  pl.BlockSpec((1, D_RNN), c2),
                  pl.BlockSpec((LRU_BLOCKS, LRU_BW, LRU_BW), c3),
                  pl.BlockSpec((1, D_RNN), c2),
                  pl.BlockSpec((LRU_BLOCKS, LRU_BW, LRU_BW), c3),
                  pl.BlockSpec((1, D_RNN), c2),
                  pl.BlockSpec((1, D_RNN), c2)],
        out_specs=[pl.BlockSpec((bb, tt, D_RNN), lambda i, j: (i, j, 0)),
                   pl.BlockSpec((bb, 8, D_RNN), lambda i, j: (i, 0, 0)),
                   pl.BlockSpec((bb, 1, D_RNN), lambda i, j: (i, 0, 0))],
        out_shape=[jax.ShapeDtypeStruct((bn, s, D_RNN), BF16),
                   jax.ShapeDtypeStruct((bn, 8, D_RNN), F32),
                   jax.ShapeDtypeStruct((bn, 1, D_RNN), F32)],
        scratch_shapes=[pltpu.VMEM((bb, 1, D_RNN), F32), pltpu.VMEM((bb, 8, D_RNN), F32)],
        compiler_params=_cparams("arbitrary", "arbitrary"),
        name="rglru",
    )(u, conv8, hprev, cw, vec(cb), wa, vec(ba), wx, vec(bx), vec(lam))


def _bias_body(tb_ref, d_ref, o_ref):
    h = pl.program_id(0)
    d = d_ref[...]
    dc = jnp.maximum(d, 0)
    max_exact = N_BUCKETS // 2
    scaled = jnp.log(jnp.maximum(dc, 1).astype(F32) / max_exact) / math.log(MAX_DIST / max_exact)
    large = jnp.minimum(max_exact + (scaled * (N_BUCKETS - max_exact)).astype(jnp.int32), N_BUCKETS - 1)
    bucket = jnp.where(dc < max_exact, dc, large)
    acc = jnp.full(d.shape, tb_ref[0, h], F32)
    for k in range(1, N_BUCKETS):
        acc = jnp.where(bucket == k, tb_ref[k, h], acc)
    o_ref[...] = jnp.where(d < 0, NEG, acc)


def rel_bias_tiles(table, dist):
    rows, cols = dist.shape
    return pl.pallas_call(
        _bias_body,
        grid=(N_HEADS,),
        in_specs=[pl.BlockSpec(memory_space=pltpu.SMEM), pl.BlockSpec((rows, cols), lambda h: (0, 0))],
        out_specs=pl.BlockSpec((None, rows, cols), lambda h: (h, 0, 0)),
        out_shape=jax.ShapeDtypeStruct((N_HEADS, rows, cols), F32),
        compiler_params=_cparams("arbitrary"),
        name="rel_bias_tiles",
    )(table, dist)


def _reorder_emit(sources, stage, poslo_ref, poshi_ref, lo_ref, hi_ref):
    n_col = stage.shape[0]
    row = 0
    for src in sources:
        n = src.shape[0]
        for c in range(n_col):
            stage[c, row:row + n, :] = src[:, c * LANES:(c + 1) * LANES]
        row += n
    n16 = row // CMP_STRIDE
    for r in range(CMP_STRIDE):
        x = jnp.concatenate([stage[c, pl.ds(r, n16, stride=CMP_STRIDE), :] for c in range(n_col)], axis=1)
        lo_ref[r] = (x + poslo_ref[r:r + 1]).astype(lo_ref.dtype)
        hi_ref[r] = (x + poshi_ref[r:r + 1]).astype(hi_ref.dtype)


def _reorder_dense_body(x_ref, poslo_ref, poshi_ref, lo_ref, hi_ref, stage):
    _reorder_emit([x_ref], stage, poslo_ref, poshi_ref, lo_ref, hi_ref)


def _reorder_paged_body(pt_ref, *refs, n_pages):
    del pt_ref
    pages = refs[:n_pages]
    poslo_ref, poshi_ref, lo_ref, hi_ref, stage = refs[n_pages:]
    _reorder_emit(pages, stage, poslo_ref, poshi_ref, lo_ref, hi_ref)


def reorder_dense(kv, poslo, poshi):
    bn, s, _ = kv.shape
    n16 = s // CMP_STRIDE
    shp = jax.ShapeDtypeStruct((bn, CMP_STRIDE, n16, 512), BF16)
    ospec = pl.BlockSpec((None, CMP_STRIDE, n16, 512), lambda b: (b, 0, 0, 0))
    pspec = pl.BlockSpec((CMP_STRIDE, 512), lambda b: (0, 0))
    return pl.pallas_call(
        _reorder_dense_body,
        grid=(bn,),
        in_specs=[pl.BlockSpec((None, s, 512), lambda b: (b, 0, 0)), pspec, pspec],
        out_specs=[ospec, ospec],
        out_shape=[shp, shp],
        scratch_shapes=[pltpu.VMEM((512 // LANES, s, LANES), F32)],
        compiler_params=_cparams("arbitrary"),
        name="reorder_dense",
    )(kv, poslo, poshi)


def reorder_paged(cache, page_table, poslo, poshi):
    bn, n_pages = page_table.shape
    page = cache.shape[1]
    per_page = page // CMP_STRIDE
    n16 = n_pages * per_page
    shp = jax.ShapeDtypeStruct((bn, CMP_STRIDE, n16, 512), BF16)
    ospec = pl.BlockSpec((None, CMP_STRIDE, n16, 512), lambda b, pt: (b, 0, 0, 0))
    pspec = pl.BlockSpec((CMP_STRIDE, 512), lambda b, pt: (0, 0))
    page_specs = [pl.BlockSpec((None, page, 512), functools.partial(lambda b, pt, k: (pt[b * n_pages + k], 0, 0), k=k))
                  for k in range(n_pages)]
    return pl.pallas_call(
        functools.partial(_reorder_paged_body, n_pages=n_pages),
        grid_spec=pltpu.PrefetchScalarGridSpec(
            num_scalar_prefetch=1, grid=(bn,),
            in_specs=page_specs + [pspec, pspec],
            out_specs=[ospec, ospec],
            scratch_shapes=[pltpu.VMEM((512 // LANES, n_pages * page, LANES), F32)]),
        out_shape=[shp, shp],
        compiler_params=_cparams("arbitrary"),
        name="reorder_paged",
    )(page_table.reshape(-1), *([cache] * n_pages), poslo, poshi)


def _compress_body(lo_ref, hi_ref, wlo_ref, whi_ref, w2_ref, o_ref, acc1, acc2, *, rows):
    r = pl.program_id(2)

    @pl.when(r == 0)
    def _():
        acc1[...] = jnp.zeros_like(acc1)
        acc2[...] = jnp.zeros_like(acc2)

    a1 = acc1[...]
    a2 = acc2[...]
    for k in range(CMP_RSTEP):
        a1 = a1 + _dot(lo_ref[:, k].reshape(rows, 256), wlo_ref[k])
        a2 = a2 + _dot(hi_ref[:, k].reshape(rows, 256), whi_ref[k])
    acc1[...] = a1
    acc2[...] = a2

    @pl.when(r == CMP_STRIDE // CMP_RSTEP - 1)
    def _():
        hidden = _gelu_tanh(acc1[...] + pltpu.roll(acc2[...], rows - 1, 0))
        o_ref[...] = _dot(hidden.astype(BF16), w2_ref[...]).reshape(o_ref.shape).astype(o_ref.dtype)


def compress(xlo, xhi, wlo, whi, w2bd, sb):
    bn, _, n16, _ = xlo.shape
    assert bn % sb == 0
    rows = sb * n16
    xspec = pl.BlockSpec((sb, CMP_RSTEP, n16, 256), lambda g, s, r: (g, r, 0, s))
    wspec = pl.BlockSpec((None, CMP_RSTEP, 256, 512), lambda g, s, r: (s, r, 0, 0))
    return pl.pallas_call(
        functools.partial(_compress_body, rows=rows),
        grid=(bn // sb, 2, CMP_STRIDE // CMP_RSTEP),
        in_specs=[xspec, xspec, wspec, wspec, pl.BlockSpec((None, 512, 256), lambda g, s, r: (s, 0, 0))],
        out_specs=pl.BlockSpec((sb, n16, 256), lambda g, s, r: (g, 0, s)),
        out_shape=jax.ShapeDtypeStruct((bn, n16, 512), BF16),
        scratch_shapes=[pltpu.VMEM((rows, 512), F32), pltpu.VMEM((rows, 512), F32)],
        compiler_params=_cparams("arbitrary", "arbitrary", "arbitrary"),
        name="compress",
    )(xlo, xhi, wlo, whi, w2bd)


def _assemble_body(pt_ref, *refs, n_pages, page, s_new, win_len):
    del pt_ref
    pages = refs[:n_pages]
    new_slc_ref, new_win_ref, cwin_ref, slc_ref, win_ref = refs[n_pages:]
    for k in range(n_pages):
        slc_ref[k * page:(k + 1) * page, :] = pages[k][...].astype(slc_ref.dtype)
    pad = jnp.zeros((SLC_LEN - s_new, 512), F32)
    slc_ref[n_pages * page:n_pages * page + SLC_LEN, :] = jnp.concatenate(
        [new_slc_ref[...], pad], axis=0).astype(slc_ref.dtype)
    win_ref[0:win_len, :] = cwin_ref[...].astype(win_ref.dtype)
    win_ref[win_len:win_len + SLC_LEN, :] = jnp.concatenate([new_win_ref[...], pad], axis=0).astype(win_ref.dtype)


def assemble_sample_kv(cache, page_table, kv_new, cache_win):
    bn, n_pages = page_table.shape
    page = cache.shape[1]
    s_new = kv_new.shape[1]
    win_len = cache_win.shape[1]
    past = n_pages * page
    page_specs = [pl.BlockSpec((None, page, 512), functools.partial(lambda b, pt, k: (pt[b * n_pages + k], 0, 1), k=k))
                  for k in range(n_pages)]
    return pl.pallas_call(
        functools.partial(_assemble_body, n_pages=n_pages, page=page, s_new=s_new, win_len=win_len),
        grid_spec=pltpu.PrefetchScalarGridSpec(
            num_scalar_prefetch=1, grid=(bn,),
            in_specs=page_specs + [pl.BlockSpec((None, s_new, 512), lambda b, pt: (b, 0, 1)),
                                   pl.BlockSpec((None, s_new, 512), lambda b, pt: (b, 0, 2)),
                                   pl.BlockSpec((None, win_len, 512), lambda b, pt: (b, 0, 0))],
            out_specs=[pl.BlockSpec((None, past + SLC_LEN, 512), lambda b, pt: (b, 0, 0)),
                       pl.BlockSpec((None, win_len + SLC_LEN, 512), lambda b, pt: (b, 0, 0))]),
        out_shape=[jax.ShapeDtypeStruct((bn, past + SLC_LEN, 512), BF16),
                   jax.ShapeDtypeStruct((bn, win_len + SLC_LEN, 512), BF16)],
        compiler_params=_cparams("arbitrary"),
        name="assemble_sample_kv",
    )(page_table.reshape(-1), *([cache] * n_pages), kv_new, kv_new, cache_win)


def _value_heads(vv):
    own_a = lax.broadcasted_iota(jnp.int32, vv.shape, 1) < HEAD_DIM
    return jnp.where(own_a, vv, 1.0), jnp.where(own_a, 1.0, vv)


def _softmax_update(state, s, vv, rr):
    smax = jnp.max(s, axis=-1, keepdims=True)
    mn = smax if state is None else jnp.maximum(state[0], smax)
    e = jnp.exp2(s - mn).astype(BF16)
    va, vb = _value_heads(vv)
    pv = jnp.concatenate([_dot(e[:rr], va), _dot(e[rr:], vb)], axis=0)
    return mn, (pv if state is None else jnp.exp2(state[0] - mn) * state[1] + pv)


def _softmax_finish(state):
    _, acc = state
    return acc / pltpu.roll(acc, HEAD_DIM, 1)


def _attn_seq(qi, q_ref, gl_ref, kc_ref, vc_ref, ks_ref, vs_ref, kw_ref, vw_ref, augk_ref, augw_ref,
              ovt_ref, cb_ref, nb_ref, wb_ref, augc_ref, ex_ref, o_ref,
              *, qb, n_slc, top_n, pad_s, t0, multi_block, n_streams):
    rr = GQA * qb
    r2 = 2 * rr
    pairs = range(N_PAIRS)
    is_a = lax.broadcasted_iota(jnp.int32, (rr, LANES), 1) < HEAD_DIM
    if multi_block:
        cur = t0 // SLC_LEN + qi
        win_start = pl.multiple_of(qi * qb, SLC_LEN)
    else:
        cur = t0 // SLC_LEN
        win_start = 0

    def cols(ref, p, start, size):
        return ref[pl.ds(start, size), p * LANES:(p + 1) * LANES]

    def init():
        return jnp.full((r2, 1), -jnp.inf, F32), jnp.zeros((r2, LANES), F32)

    wlen = WINDOW + SLC_LEN
    augw = augw_ref[pl.ds(win_start, wlen), :]
    glt = jnp.concatenate([gl_ref[...]] * GQA, axis=0)
    slab = lax.broadcasted_iota(jnp.int32, (rr, LANES), 0) // qb
    gsrc = jnp.concatenate([jnp.where(slab == g, glt, 0.0) for g in range(GQA)], axis=1)
    ghi = gsrc.astype(BF16)
    glo = (gsrc - ghi.astype(F32)).astype(BF16)

    q2, o_win, gates, o_cmp, imp_t = [], [], [], [], []
    for p in pairs:
        qfull = q_ref[:, p * GQA * LANES:(p + 1) * GQA * LANES]
        qs = jnp.concatenate([qfull[:, g * LANES:(g + 1) * LANES] for g in range(GQA)], axis=0)
        qs = qs * (HEAD_DIM ** -0.5 * LOG2E)
        q2.append(jnp.concatenate([jnp.where(is_a, qs, 0.0), jnp.where(is_a, 0.0, qs)], axis=0).astype(BF16))

        lhs_win = jnp.concatenate([q2[p], augc_ref[p, 1].astype(BF16)], axis=1)
        kk = jnp.concatenate([cols(kw_ref, p, win_start, wlen), augw], axis=1)
        o_win.append(_softmax_finish(_softmax_update(None, _dot_nt(lhs_win, kk) + wb_ref[p],
                                                     cols(vw_ref, p, win_start, wlen), rr)))

        gates.append(jax.nn.sigmoid(_dot(ghi, ex_ref[p]) + _dot(glo, ex_ref[p])))

        s = _dot_nt(q2[p], kc_ref[:, p * LANES:(p + 1) * LANES]) + cb_ref[p]
        m = jnp.max(s, axis=-1, keepdims=True)
        e = jnp.exp2(s - m)
        pr = e / jnp.sum(e, axis=-1, keepdims=True)
        pr = jnp.where(m > 0.1 * NEG, pr, 0.0)
        o_cmp.append(_dot(pr.astype(BF16), vc_ref[:, p * LANES:(p + 1) * LANES]))

        pcat = jnp.concatenate(
            [jnp.concatenate([pr[x * rr + g * qb:x * rr + (g + 1) * qb] for g in range(GQA)], axis=1)
             for x in range(2)], axis=0)
        if 2 * qb < LANES:
            pcat = jnp.concatenate([pcat, jnp.zeros((LANES - 2 * qb, pcat.shape[1]), F32)], axis=0)
        imp_t.append(_dot_nt(ovt_ref[...], pcat.astype(BF16)))

    def ranked():
        jr = lax.broadcasted_iota(jnp.int32, (N_BLOCK_LANES, LANES), 0)
        forced = (jr == 0) | (jr == cur) | (jr == cur - 1)
        res = []
        for p in pairs:
            v = jnp.where(forced, FORCE, jnp.where((jr > cur) | (jr >= n_slc), NEG, imp_t[p][:N_BLOCK_LANES]))
            rank = jnp.zeros((N_BLOCK_LANES, LANES), F32)
            for k in range(n_slc):
                vk = v[k:k + 1, :]
                rank = rank + jnp.where(vk > v, 1.0, jnp.where(vk == v, jnp.where(jr > k, 1.0, 0.0), 0.0))
            res.append(jnp.where(rank < top_n, 0.0, 1.0))
        return tuple(res)

    if multi_block:
        all_selected = lambda: tuple(jnp.zeros((N_BLOCK_LANES, LANES), F32) for _ in pairs)
        notsel_t = lax.cond(cur >= top_n, ranked, all_selected)
    else:
        notsel_t = ranked()

    st = pad_s + (cur - 2) * SLC_LEN
    if multi_block:
        st = pl.multiple_of(st, SLC_LEN)
    augk_near = augk_ref[pl.ds(st, 3 * SLC_LEN), :]
    lane2 = lax.broadcasted_iota(jnp.int32, (r2, LANES), 1)
    lhs_far, near = [], []
    for p in pairs:
        notsel = jnp.concatenate([notsel_t[p], jnp.zeros((LANES - N_BLOCK_LANES, LANES), F32)], axis=0).T
        ns = jnp.concatenate([notsel[x * qb:(x + 1) * qb] for x in range(2) for _ in range(GQA)], axis=0)
        ns_far = jnp.where((lane2 >= cur - 2) & (lane2 < N_BLOCK_LANES), 1.0, ns)
        lhs_far.append(jnp.concatenate([q2[p], (ns_far + augc_ref[p, 0]).astype(BF16)], axis=1))
        lhs_near = jnp.concatenate([q2[p], (ns + augc_ref[p, 1]).astype(BF16)], axis=1)
        kk = jnp.concatenate([cols(ks_ref, p, st, 3 * SLC_LEN), augk_near], axis=1)
        near.append(_softmax_update(None, _dot_nt(lhs_near, kk) + nb_ref[p], cols(vs_ref, p, st, 3 * SLC_LEN), rr))

    def far_step(it, states):
        new = []
        for k in range(n_streams):
            st = pl.multiple_of(pad_s + (it * n_streams + k) * FAR_CHUNK, FAR_CHUNK)
            augk = augk_ref[pl.ds(st, FAR_CHUNK), :]
            for p in pairs:
                kk = jnp.concatenate([cols(ks_ref, p, st, FAR_CHUNK), augk], axis=1)
                new.append(_softmax_update(states[k * N_PAIRS + p], _dot_nt(lhs_far[p], kk),
                                           cols(vs_ref, p, st, FAR_CHUNK), rr))
        return tuple(new)

    per_iter = n_streams * (FAR_CHUNK // SLC_LEN)
    if multi_block:
        states = tuple(init() for _ in range(n_streams * N_PAIRS))
        states = lax.fori_loop(0, (jnp.maximum(cur - 2, 0) + per_iter - 1) // per_iter, far_step, states)
    else:
        states = (None,) * (n_streams * N_PAIRS)
        for it in range((max(cur - 2, 0) + per_iter - 1) // per_iter):
            states = far_step(it, states)

    outs = []
    for p in pairs:
        mine = tuple(st for st in states[p::N_PAIRS] if st is not None) + (near[p],)
        m_all = functools.reduce(jnp.maximum, [m for m, _ in mine])
        o_slc = _softmax_finish((m_all, sum(acc * jnp.exp2(m - m_all) for m, acc in mine)))
        out = None
        for br, o2 in enumerate((o_cmp[p], o_slc, o_win[p])):
            term = gates[p][:, br * LANES:(br + 1) * LANES] * jnp.where(is_a, o2[:rr], o2[rr:])
            out = term if out is None else out + term
        outs += [out[g * qb:(g + 1) * qb] for g in range(GQA)]
    o_ref[...] = jnp.concatenate(outs, axis=1).astype(o_ref.dtype)


def _attn_body(*refs, sbq, **kw):
    qi = pl.program_id(1)
    per_seq, shared = refs[:8], refs[8:16]
    for i in range(sbq):
        _attn_seq(qi, *[r.at[i] for r in per_seq], *shared, refs[16].at[i], **kw)


def _cmp_bias_body(cbe_ref, o_ref, *, nc, step):
    tau = pl.program_id(0) * step
    for p in range(N_PAIRS):
        o_ref[p] = pltpu.roll(cbe_ref[p], tau, 1)[:, nc:]


def cmp_bias_per_block(cbe, nqb, step):
    n_p, rows, nc2 = cbe.shape
    nc = nc2 // 2
    return pl.pallas_call(
        functools.partial(_cmp_bias_body, nc=nc, step=step),
        grid=(nqb,),
        in_specs=[pl.BlockSpec(cbe.shape, lambda i: (0, 0, 0))],
        out_specs=pl.BlockSpec((None, n_p, rows, nc), lambda i: (i, 0, 0, 0)),
        out_shape=jax.ShapeDtypeStruct((nqb, n_p, rows, nc), F32),
        compiler_params=_cparams("arbitrary"),
        name="cmp_bias_per_block",
    )(cbe)


def nsa_attention(u, cmp_kv, slc_kv, slc_cols, win_kv, win_cols, augk, augw, ovt, tiles, ex,
                  *, qb, n_slc, pad_s, t0, sbq):
    bn, s, _ = u.shape
    nqb = s // qb
    nc = cmp_kv.shape[1]
    r2 = 2 * GQA * qb
    cb, nb, wb, augc = tiles
    srows = slc_kv.shape[1]
    wrows = win_kv.shape[1]
    kcol, vcol = slc_cols
    kwcol, vwcol = win_cols
    assert qb == SLC_LEN or nqb == 1
    assert t0 % SLC_LEN == 0 and bn % sbq == 0
    per_chunk = FAR_CHUNK // SLC_LEN
    max_chunks = -(-max(n_slc - 3, 0) // per_chunk)
    n_streams = min(1, max_chunks) if nqb > 1 else max_chunks
    n_streams = max(n_streams, 1)
    assert pad_s + -(-max_chunks // n_streams) * n_streams * FAR_CHUNK <= srows
    body = functools.partial(_attn_body, sbq=sbq, qb=qb, n_slc=n_slc, top_n=min(N_SELECT, n_slc),
                             pad_s=pad_s, t0=t0, multi_block=nqb > 1, n_streams=n_streams)
    whole = lambda a: pl.BlockSpec(a.shape, lambda b, i: (0,) * a.ndim)
    kvw = N_PAIRS * LANES
    nq = N_HEADS * HEAD_DIM
    return pl.pallas_call(
        body,
        grid=(bn // sbq, nqb),
        in_specs=[pl.BlockSpec((sbq, qb, nq), lambda b, i: (b, i, 0)),
                  pl.BlockSpec((sbq, qb, LANES), lambda b, i: (b, i, nq // LANES)),
                  pl.BlockSpec((sbq, nc, kvw), lambda b, i: (b, 0, 0)),
                  pl.BlockSpec((sbq, nc, kvw), lambda b, i: (b, 0, 1)),
                  pl.BlockSpec((sbq, srows, kvw), lambda b, i: (b, 0, kcol)),
                  pl.BlockSpec((sbq, srows, kvw), lambda b, i: (b, 0, vcol)),
                  pl.BlockSpec((sbq, wrows, kvw), lambda b, i: (b, 0, kwcol)),
                  pl.BlockSpec((sbq, wrows, kvw), lambda b, i: (b, 0, vwcol)),
                  whole(augk), whole(augw), whole(ovt),
                  pl.BlockSpec((None, N_PAIRS, r2, nc), lambda b, i: (i, 0, 0, 0)),
                  whole(nb), whole(wb), whole(augc), whole(ex)],
        out_specs=pl.BlockSpec((sbq, qb, nq), lambda b, i: (b, i, 0)),
        out_shape=jax.ShapeDtypeStruct((bn, s, nq), F32),
        compiler_params=_cparams("arbitrary", "arbitrary"),
        name="nsa_attention",
    )(u, u, cmp_kv, cmp_kv, slc_kv, slc_kv, win_kv, win_kv, augk, augw, ovt, cb, nb, wb, augc, ex)


def _head_perm():
    idx = np.empty((N_PAIRS, GQA, 2, HEAD_DIM), np.int32)
    for p in range(N_PAIRS):
        for g in range(GQA):
            for half in range(2):
                head = (2 * p + half) * GQA + g
                idx[p, g, half] = head * HEAD_DIM + np.arange(HEAD_DIM)
    return idx.reshape(-1)


def _gate_expand():
    ex = np.zeros((N_PAIRS, GQA * LANES, 3 * LANES), np.float32)
    for p in range(N_PAIRS):
        for g in range(GQA):
            for half in range(2):
                head = (2 * p + half) * GQA + g
                for br in range(3):
                    ex[p, g * LANES + head * 3 + br, br * LANES + half * HEAD_DIM:br * LANES + (half + 1) * HEAD_DIM] = 1.0
    return ex


def _overlap_t(nc, n_cmp, n_slc):
    i = np.arange(nc)[None, :] * CMP_STRIDE
    j = np.arange(LANES)[:, None] * SLC_LEN
    ov = (i < j + SLC_LEN) & (i + CMP_LEN > j) & (np.arange(nc)[None, :] < n_cmp) & (np.arange(LANES)[:, None] < n_slc)
    return np.tile(ov.astype(np.float32), (1, GQA))


def _aug_keys(pad, n_keys):
    a = np.zeros((pad + n_keys, LANES), np.float32)
    a[:pad, AUG_PAD] = NEG
    k = np.arange(n_keys)
    a[pad + k, k // SLC_LEN] = NEG
    a[pad:, AUG_BIAS_HI] = 1.0
    a[pad:, AUG_BIAS_LO] = 1.0
    return a


def _attn_tiles(rel_bias, qb, t0, nc, nqb):
    rolled = nqb > 1
    i = np.arange(qb)[:, None]
    near = np.arange(3 * SLC_LEN)[None, :]
    d_near = SLC_LEN * (2 - near // SLC_LEN) + i - near % SLC_LEN
    c = np.arange(WINDOW + SLC_LEN)[None, :]
    d_win = i + WINDOW - c
    d_win = np.where(d_win < WINDOW, d_win, -1)
    if rolled:
        e = np.arange(2 * nc)[None, :] - nc
        d_cmp = i - (CMP_LEN - 1) - CMP_STRIDE * e
    else:
        d_cmp = t0 + i - CMP_STRIDE * np.arange(nc)[None, :] - (CMP_LEN - 1)
    widths = [d_near.shape[1], d_win.shape[1], d_cmp.shape[1]]
    padded = [-(-w // LANES) * LANES for w in widths]
    dist = np.full((qb, sum(padded)), -1, np.int32)
    off = 0
    offs = []
    for d, w, pw in zip((d_near, d_win, d_cmp), widths, padded):
        dist[:, off:off + w] = np.clip(d, -1, 4 * MAX_DIST)
        offs.append(off)
        off += pw
    tiles = rel_bias_tiles(rel_bias, jnp.asarray(dist))
    r2 = 2 * GQA * qb
    tiles = tiles.reshape(N_PAIRS, r2, dist.shape[1]) * LOG2E
    nb, wb, cbe = (tiles[..., o:o + w] for o, w in zip(offs, widths))
    if rolled:
        cb = cmp_bias_per_block(cbe, nqb, (t0 + qb) // CMP_STRIDE - t0 // CMP_STRIDE)
    else:
        cb = cbe[None]
    far = rel_bias[N_BUCKETS - 1].reshape(N_PAIRS, 2 * GQA) * LOG2E
    hi = far.astype(BF16).astype(F32)
    lo = far - hi
    rows = jnp.repeat(jnp.stack([hi, lo], axis=-1), qb, axis=1)
    augc = jnp.zeros((N_PAIRS, 2, r2, LANES), F32)
    augc = augc.at[:, 0, :, AUG_BIAS_HI].set(rows[..., 0]).at[:, 0, :, AUG_BIAS_LO].set(rows[..., 1])
    augc = augc.at[:, 1, :, AUG_PAD].set(1.0)
    return cb, nb, wb, augc


def _forward(x, ada, ada_kv, conv_state, h_state, past, w, shared):
    bn, s, _ = x.shape
    m = bn * s
    xf = x.reshape(m, D_MODEL)
    new_conv, new_h = [], []
    h = normmod(xf, w["norm_g"][0, 0], ada[0], 0, 1, bn, s)
    kv = None
    attn_args = None
    for l in range(DEPTH):
        if l < N_A_LAYERS:
            u = mm_plain(h, w["w_in_a"][l]).reshape(bn, s, 2 * D_RNN)
            conv8 = jnp.pad(conv_state[l], ((0, 0), (8 - (CONV_W - 1), 0), (0, 0)))
            y, conv_o, h_o = rglru(u, conv8, h_state[l][:, None, :], w["conv_w"][l], w["conv_b"][l],
                                   w["w_rg_a"][l], w["b_rg_a"][l], w["w_rg_x"][l], w["b_rg_x"][l], w["lru_lambda"][l])
            new_conv.append(conv_o[:, 8 - (CONV_W - 1):])
            new_h.append(h_o[:, 0])
            a_in = y.reshape(m, D_RNN)
            w_out = w["w_out_a"][l]
        else:
            lb = l - N_A_LAYERS
            if lb == 0:
                kv = mm_plain(h_kv, w["w_kv"]).reshape(bn, s, 6 * N_KV_HEADS * HEAD_DIM)
                attn_args = _prepare_attention(kv, past, w, shared, bn, s)
            u = mm_plain(h, w["w_in_b"][lb]).reshape(bn, s, -1)
            a_in = nsa_attention(u, *attn_args[0], **attn_args[1]).reshape(m, N_HEADS * HEAD_DIM)
            w_out = w["w_out_b"][lb]
        x_new, (h2,) = mm_resnorm(a_in, w_out, xf, (ada[l], 2), w["norm_g"][l, 1],
                                  [(w["norm_g"][l, 2], ada[l], 3, ada[l], 4)], bn, s)
        f = mm_swiglu(h2, w["w_ffn_in"][l])
        heads = []
        if l + 1 < DEPTH:
            heads.append((w["norm_g"][l + 1, 0], ada[l + 1], 0, ada[l + 1], 1))
        if l + 1 == N_A_LAYERS:
            heads.append((w["norm_kv"], ada_kv, 0, ada_kv, 1))
        xf, hs = mm_resnorm(f, w["w_ffn_out"][l], x_new, (ada[l], 5), w["norm_g"][l, 3], heads, bn, s)
        if heads:
            h = hs[0]
        if l + 1 == N_A_LAYERS:
            h_kv = hs[1]
    kv_rows = kv[:, :, :4 * N_KV_HEADS * HEAD_DIM].reshape(bn, s, 4, N_KV_HEADS, HEAD_DIM)
    new_win = kv[:, :, 4 * N_KV_HEADS * HEAD_DIM:].reshape(bn, s, 2, N_KV_HEADS, HEAD_DIM)
    if past is None:
        win_state = new_win[:, -min(WINDOW, s):]
    else:
        cache_win = past[2]
        win_state = jnp.concatenate([cache_win, new_win], axis=1)[:, -cache_win.shape[1]:]
    return xf.reshape(bn, s, D_MODEL), kv_rows, win_state, jnp.stack(new_conv), jnp.stack(new_h)


def _prepare_attention(kv, past, w, shared, bn, s):
    rel_bias = w["rel_bias"]
    if past is None:
        t0, qb, pad = 0, Q_BLOCK, WINDOW
        n_keys = s
        xlo, xhi = reorder_dense(kv, shared["poslo"], shared["poshi"])
        kvb = jnp.pad(kv.astype(BF16), ((0, 0), (pad, 0), (0, 0)))
        slc_kv = win_kv = kvb
        slc_cols, win_cols = (2, 3), (4, 5)
        augk = jnp.asarray(_aug_keys(pad, n_keys), BF16)
        augw = augk
        sb = bn
    else:
        cache, page_table, cache_win = past
        n_pool, page = cache.shape[:2]
        cache2 = cache.reshape(n_pool, page, -1)
        t0 = page_table.shape[1] * page
        qb, pad = s, 0
        n_keys = t0 + SLC_LEN
        assert cache_win.shape[1] == WINDOW and t0 >= WINDOW
        xlo, xhi = reorder_paged(cache2, page_table, shared["poslo"], shared["poshi"])
        slc_kv, win_kv = assemble_sample_kv(cache2, page_table, kv, cache_win.reshape(bn, WINDOW, -1))
        slc_cols, win_cols = (0, 1), (0, 1)
        augk = jnp.asarray(_aug_keys(0, n_keys), BF16)
        augw = jnp.zeros((win_kv.shape[1], LANES), BF16)
        sb = 8
    n16 = (t0 + s) // CMP_STRIDE
    n_cmp = n16 - CMP_LEN // CMP_STRIDE + 1
    nc = xlo.shape[2]
    assert nc >= n_cmp and nc % LANES == 0
    n_slc = -(-(t0 + s) // SLC_LEN)
    assert n_slc <= N_BLOCK_LANES
    cmp_kv = compress(xlo, xhi, shared["wlo"], shared["whi"], shared["w2bd"], sb)
    nqb = s // qb
    assert nqb == 1 or t0 == 0
    tiles = _attn_tiles(rel_bias, qb, t0, nc, nqb)
    ovt = jnp.asarray(_overlap_t(nc, n_cmp, n_slc), BF16)
    args = (cmp_kv, slc_kv, slc_cols, win_kv, win_cols, augk, augw, ovt, tiles, shared["ex"])
    return args, dict(qb=qb, n_slc=n_slc, pad_s=pad, t0=t0, sbq=1 if nqb > 1 else min(bn, 2))


def kernel(x_prompt, x_sample, c_prompt, c_sample, cache_kv, cache_win, state_conv, state_h, page_table, w_ada, b_ada, norm_g, w_in_a, conv_w, conv_b, w_rg_a, b_rg_a, w_rg_x, b_rg_x, lru_lambda, w_out_a, w_ada_kv, b_ada_kv, norm_kv, w_kv, cmp_pos, cmp_w1, cmp_w2, w_in_b, w_out_b, rel_bias, w_ffn_in, w_ffn_out):
    bp = x_prompt.shape[0]
    bs = x_sample.shape[0]

    perm = _head_perm()
    n_q = N_HEADS * HEAD_DIM
    w_in_b_p = jnp.concatenate([w_in_b[:, :, :n_q][:, :, perm], w_in_b[:, :, n_q:],
                                jnp.zeros(w_in_b.shape[:2] + (LANES - 3 * N_HEADS,), F32)], axis=-1)
    w = dict(norm_g=norm_g, norm_kv=norm_kv, conv_w=conv_w, conv_b=conv_b, b_rg_a=b_rg_a, b_rg_x=b_rg_x,
             lru_lambda=lru_lambda, rel_bias=rel_bias,
             w_in_a=w_in_a, w_rg_a=w_rg_a.astype(BF16), w_rg_x=w_rg_x.astype(BF16),
             w_out_a=w_out_a, w_kv=w_kv, w_in_b=w_in_b_p,
             w_out_b=w_out_b[:, perm, :], w_ffn_in=w_ffn_in, w_ffn_out=w_ffn_out)

    eye = jnp.eye(N_KV_HEADS, dtype=F32)
    w1r = cmp_w1.reshape(2, CMP_LEN, HEAD_DIM, CMP_HIDDEN)
    w1bd = jnp.einsum("hk,srdj->srhdkj", eye, w1r).reshape(2, CMP_LEN, N_KV_HEADS * HEAD_DIM, N_KV_HEADS * CMP_HIDDEN)
    w2bd = jnp.einsum("hk,sjd->shjkd", eye, cmp_w2).reshape(2, N_KV_HEADS * CMP_HIDDEN, N_KV_HEADS * HEAD_DIM)
    pos = jnp.broadcast_to(cmp_pos[:, :, None, :], (2, CMP_LEN, N_KV_HEADS, HEAD_DIM)).reshape(2, CMP_LEN, -1)
    pos = jnp.concatenate([pos[0], pos[1]], axis=-1)
    shared = dict(wlo=w1bd[:, :CMP_STRIDE].astype(BF16), whi=w1bd[:, CMP_STRIDE:].astype(BF16),
                  w2bd=w2bd.astype(BF16), poslo=pos[:CMP_STRIDE], poshi=pos[CMP_STRIDE:],
                  ex=jnp.asarray(_gate_expand(), BF16))

    n_c = bp + bs
    c_all = jnp.concatenate([c_prompt, c_sample, jnp.zeros((-n_c % 8, D_MODEL), F32)], axis=0)
    ada_all = [mm_plain(c_all, w_ada[l], bias=b_ada[l], act="silu", tn=2048) for l in range(DEPTH)]
    ada_kv_all = mm_plain(c_all, w_ada_kv, bias=b_ada_kv, act="silu", tn=2048)
    ada_p = [a[:bp, None, :] for a in ada_all]
    ada_s = [a[bp:n_c, None, :] for a in ada_all]

    zero_conv = jnp.zeros((N_A_LAYERS, bp, CONV_W - 1, D_RNN), F32)
    zero_h = jnp.zeros((N_A_LAYERS, bp, D_RNN), F32)
    y_p, kv_p, win_p, conv_p, h_p = _forward(x_prompt, ada_p, ada_kv_all[:bp, None, :], zero_conv, zero_h, None, w, shared)
    y_s, kv_s, win_s, conv_s, h_s = _forward(x_sample, ada_s, ada_kv_all[bp:n_c, None, :], state_conv, state_h,
                                             (cache_kv, page_table, cache_win), w, shared)
    return (y_p, y_s, kv_p, kv_s, win_p, win_s, conv_p, conv_s, h_p, h_s)
```

```python
import functools
import math

import numpy as np
import jax
import jax.numpy as jnp
from jax import lax
from jax.experimental import pallas as pl
from jax.experimental.pallas import tpu as pltpu

D_MODEL = 1024
DEPTH = 4
N_A_LAYERS = 2
D_RNN = 1024
LRU_BLOCKS = 8
LRU_BW = 128
CONV_W = 4
LRU_C = 8.0
N_HEADS = 16
HEAD_DIM = 64
N_KV_HEADS = 4
GQA = 4
CMP_LEN = 32
CMP_STRIDE = 16
CMP_HIDDEN = 128
SLC_LEN = 64
N_SELECT = 16
WINDOW = 512
Q_BLOCK = 64
N_BUCKETS = 32
MAX_DIST = 128
D_FF = 2816
EPS = 1e-6
NEG = -1e30
FORCE = 1e9

LANES = 128
PAIR_W = 2 * HEAD_DIM
N_PAIRS = N_KV_HEADS // 2
AUG_BIAS_HI = 64
AUG_BIAS_LO = 65
AUG_PAD = 66
N_BLOCK_LANES = 64
FAR_CHUNK = 1024
CMP_RSTEP = 4
RESNORM_PART = 512
LOG2E = 1.4426950408889634
VMEM_LIMIT = 56 * 1024 * 1024

F32 = jnp.float32
BF16 = jnp.bfloat16


def _cparams(*sem):
    return pltpu.CompilerParams(dimension_semantics=sem, vmem_limit_bytes=VMEM_LIMIT)


def _dot(a, b):
    return jnp.dot(a, b, preferred_element_type=F32)


def _dot_nt(a, b):
    return lax.dot_general(a, b, (((1,), (1,)), ((), ())), preferred_element_type=F32)


def _gelu_tanh(x):
    return 0.5 * x * (1.0 + jnp.tanh(math.sqrt(2.0 / math.pi) * (x + 0.044715 * (x * x * x))))


def _rms(x, gain):
    return x * lax.rsqrt(jnp.mean(x * x, axis=-1, keepdims=True) + EPS) * gain


def _rows(vec_ref, bb, ts):
    v = vec_ref[...]
    d = v.shape[-1]
    return jnp.broadcast_to(v, (bb, ts, d)).reshape(bb * ts, d)


def _vec_spec(bb, tiles_per_seq, col, ngrid):
    if ngrid == 1:
        return pl.BlockSpec((bb, 1, D_MODEL), lambda i: (i // tiles_per_seq, 0, col))
    return pl.BlockSpec((bb, 1, D_MODEL), lambda j, i: (i // tiles_per_seq, 0, col))


def _row_tiling(bn, s, tm=512):
    if s >= tm:
        assert s % tm == 0
        return tm, 1, tm, s // tm
    assert tm % s == 0 and s % 8 == 0
    bb = min(bn, tm // s)
    assert bn % bb == 0
    return bb * s, bb, s, 1


def _bf16_weight(w_ref, cache_ref, first_row_tile):
    if cache_ref is None:
        return w_ref[...].astype(BF16)

    @pl.when(first_row_tile)
    def _():
        cache_ref[...] = w_ref[...].astype(BF16)

    return cache_ref[...]


def _weight_cache(w, block, n_row_tiles):
    return [pltpu.VMEM(block, BF16)] if (w.dtype != BF16 and n_row_tiles > 1) else []


def _mm_body(*refs, act, has_bias, cached):
    cache_ref = refs[-1] if cached else None
    refs = refs[:-1] if cached else refs
    if has_bias:
        a_ref, w_ref, b_ref, o_ref = refs
    else:
        a_ref, w_ref, o_ref = refs
    a = a_ref[...]
    if act == "silu":
        a = a.astype(F32)
        a = a * jax.nn.sigmoid(a)
    y = _dot(a.astype(BF16), _bf16_weight(w_ref, cache_ref, pl.program_id(1) == 0))
    if has_bias:
        y = y + b_ref[...]
    o_ref[...] = y.astype(o_ref.dtype)


def _layered(w):
    return w if isinstance(w, tuple) else (w[None], 0)


def mm_plain(a, w, bias=None, act=None, tm=512, tn=None, out_dtype=F32):
    w, layer = _layered(w)
    m, k = a.shape
    n = w.shape[2]
    tm = min(tm, m)
    tn = tn or n
    assert m % tm == 0 and n % tn == 0
    in_specs = [pl.BlockSpec((tm, k), lambda j, i: (i, 0)), pl.BlockSpec((None, k, tn), lambda j, i: (layer, 0, j))]
    args = [a, w]
    if bias is not None:
        in_specs.append(pl.BlockSpec((1, tn), lambda j, i: (0, j)))
        args.append(bias.reshape(1, n))
    cache = _weight_cache(w, (k, tn), m // tm)
    return pl.pallas_call(
        functools.partial(_mm_body, act=act, has_bias=bias is not None, cached=bool(cache)),
        grid=(n // tn, m // tm),
        in_specs=in_specs,
        out_specs=pl.BlockSpec((tm, tn), lambda j, i: (i, j)),
        out_shape=jax.ShapeDtypeStruct((m, n), out_dtype),
        scratch_shapes=cache,
        compiler_params=_cparams("arbitrary", "arbitrary"),
        name="mm_plain",
    )(*args)


def _kv_padded_body(a_ref, w_ref, kv_ref, kvb_ref, cache_ref):
    b, t = pl.program_id(0), pl.program_id(1)
    w = _bf16_weight(w_ref, cache_ref, (b == 0) & (t == 0))

    @pl.when(t == 0)
    def _():
        kvb_ref[...] = jnp.zeros_like(kvb_ref)

    @pl.when(t > 0)
    def _():
        y = _dot(a_ref[...], w)
        kv_ref[...] = y
        kvb_ref[...] = y.astype(kvb_ref.dtype)


def mm_kv_padded(a, w, bn, s, pad):
    m, k = a.shape
    n = w.shape[1]
    tm = pad
    assert s % tm == 0
    tps = s // tm
    src = lambda b, t: (b * tps + jnp.maximum(t - 1, 0), 0)
    return pl.pallas_call(
        _kv_padded_body,
        grid=(bn, tps + 1),
        in_specs=[pl.BlockSpec((tm, k), src),
                  pl.BlockSpec((k, n), lambda b, t: (0, 0), pipeline_mode=pl.Buffered(1))],
        out_specs=[pl.BlockSpec((tm, n), src), pl.BlockSpec((None, tm, n), lambda b, t: (b, t, 0))],
        out_shape=[jax.ShapeDtypeStruct((m, n), F32), jax.ShapeDtypeStruct((bn, pad + s, n), BF16)],
        scratch_shapes=[pltpu.VMEM((k, n), BF16)],
        compiler_params=_cparams("arbitrary", "arbitrary"),
        name="mm_kv_padded",
    )(a, w)


def _swiglu_body(h_ref, wg_ref, wu_ref, o_ref, *caches):
    first = pl.program_id(1) == 0
    wg = _bf16_weight(wg_ref, caches[0] if caches else None, first)
    wu = _bf16_weight(wu_ref, caches[1] if caches else None, first)
    tm = h_ref.shape[0]
    n_part = tm // RESNORM_PART if tm % RESNORM_PART == 0 else 1
    rows = tm // n_part
    gu = [(_dot(h_ref[i * rows:(i + 1) * rows, :], wg), _dot(h_ref[i * rows:(i + 1) * rows, :], wu))
          for i in range(n_part)]
    for i, (g, u) in enumerate(gu):
        o_ref[i * rows:(i + 1) * rows, :] = (g * jax.nn.sigmoid(g) * u).astype(o_ref.dtype)


def mm_swiglu(h, w_in, tm=2 * RESNORM_PART, tn=1408):
    w_in, layer = _layered(w_in)
    m, k = h.shape
    nf = w_in.shape[2] // 2
    tm = min(tm, m)
    assert nf % tn == 0 and m % tm == 0
    nj = nf // tn
    return pl.pallas_call(
        _swiglu_body,
        grid=(nj, m // tm),
        in_specs=[pl.BlockSpec((tm, k), lambda j, i: (i, 0)),
                  pl.BlockSpec((None, k, tn), lambda j, i: (layer, 0, j), pipeline_mode=pl.Buffered(1)),
                  pl.BlockSpec((None, k, tn), lambda j, i: (layer, 0, j + nj), pipeline_mode=pl.Buffered(1))],
        out_specs=pl.BlockSpec((tm, tn), lambda j, i: (i, j)),
        out_shape=jax.ShapeDtypeStruct((m, nf), BF16),
        scratch_shapes=_weight_cache(w_in, (k, tn), m // tm) * 2,
        compiler_params=_cparams("arbitrary", "arbitrary"),
        name="mm_swiglu",
    )(h, w_in, w_in)


def _normmod_body(x_ref, g_ref, sh_ref, sc_ref, o_ref, *, bb, ts):
    y = _rms(x_ref[...], g_ref[...])
    o_ref[...] = (y * (1.0 + _rows(sc_ref, bb, ts)) + _rows(sh_ref, bb, ts)).astype(o_ref.dtype)


def normmod(x, gain, ada, col_shift, col_scale, bn, s):
    m = x.shape[0]
    tm, bb, ts, tps = _row_tiling(bn, s)
    return pl.pallas_call(
        functools.partial(_normmod_body, bb=bb, ts=ts),
        grid=(m // tm,),
        in_specs=[pl.BlockSpec((tm, D_MODEL), lambda i: (i, 0)),
                  pl.BlockSpec((1, D_MODEL), lambda i: (0, 0)),
                  _vec_spec(bb, tps, col_shift, 1),
                  _vec_spec(bb, tps, col_scale, 1)],
        out_specs=pl.BlockSpec((tm, D_MODEL), lambda i: (i, 0)),
        out_shape=jax.ShapeDtypeStruct((m, D_MODEL), BF16),
        compiler_params=_cparams("arbitrary"),
        name="normmod",
    )(x, gain.reshape(1, D_MODEL), ada, ada)


def _resnorm_body(*refs, bb, ts, n_heads, cached):
    cache_ref = refs[-1] if cached else None
    refs = refs[:-1] if cached else refs
    a_ref, w_ref, x_ref, gate_ref, ngain_ref = refs[:5]
    head_refs = refs[5:5 + 3 * n_heads]
    xo_ref = refs[5 + 3 * n_heads]
    ho_refs = refs[6 + 3 * n_heads:]
    w = _bf16_weight(w_ref, cache_ref, pl.program_id(0) == 0)
    tm = a_ref.shape[0]
    n_part = tm // RESNORM_PART if tm % RESNORM_PART == 0 else 1
    rows = tm // n_part
    ys = [_dot(a_ref[i * rows:(i + 1) * rows, :].astype(BF16), w) for i in range(n_part)]
    if n_part > 1:
        assert bb == 1
        part_rows = lambda ref: _rows(ref, 1, rows)
    else:
        part_rows = lambda ref: _rows(ref, bb, ts)
    gate = part_rows(gate_ref)
    mods = [(head_refs[3 * k][...], part_rows(head_refs[3 * k + 1]), part_rows(head_refs[3 * k + 2]))
            for k in range(n_heads)]
    for i, y in enumerate(ys):
        sl = slice(i * rows, (i + 1) * rows)
        x = x_ref[sl, :] + gate * _rms(y, ngain_ref[...])
        xo_ref[sl, :] = x
        for k, (hg, sh, sc) in enumerate(mods):
            ho_refs[k][sl, :] = (_rms(x, hg) * (1.0 + sc) + sh).astype(ho_refs[k].dtype)


def mm_resnorm(a, w, x, gate, ngain, heads, bn, s):
    w, layer = _layered(w)
    m, k = a.shape
    tile_bytes = 2 * RESNORM_PART * (2 * k * a.dtype.itemsize + 4 * D_MODEL * 4 + 2 * len(heads) * D_MODEL * 2)
    weight_bytes = k * D_MODEL * (w.dtype.itemsize + 2)
    two_parts = s % (2 * RESNORM_PART) == 0 and tile_bytes + weight_bytes <= (3 * VMEM_LIMIT) // 4
    tm, bb, ts, tps = _row_tiling(bn, s, tm=2 * RESNORM_PART if two_parts else RESNORM_PART)
    row = lambda i: (i, 0)
    const = lambda i: (0, 0)
    in_specs = [pl.BlockSpec((tm, k), row),
                pl.BlockSpec((None, k, D_MODEL), lambda i: (layer, 0, 0), pipeline_mode=pl.Buffered(1)),
                pl.BlockSpec((tm, D_MODEL), row), _vec_spec(bb, tps, gate[1], 1),
                pl.BlockSpec((1, D_MODEL), const)]
    args = [a, w, x, gate[0], ngain.reshape(1, D_MODEL)]
    for hg, sh_arr, sh_col, sc_arr, sc_col in heads:
        in_specs += [pl.BlockSpec((1, D_MODEL), const), _vec_spec(bb, tps, sh_col, 1), _vec_spec(bb, tps, sc_col, 1)]
        args += [hg.reshape(1, D_MODEL), sh_arr, sc_arr]
    out_shape = [jax.ShapeDtypeStruct((m, D_MODEL), F32)] + [jax.ShapeDtypeStruct((m, D_MODEL), BF16)] * len(heads)
    out_specs = [pl.BlockSpec((tm, D_MODEL), row)] * (1 + len(heads))
    cache = _weight_cache(w, (k, D_MODEL), m // tm)
    outs = pl.pallas_call(
        functools.partial(_resnorm_body, bb=bb, ts=ts, n_heads=len(heads), cached=bool(cache)),
        grid=(m // tm,),
        in_specs=in_specs,
        out_specs=out_specs,
        out_shape=out_shape,
        scratch_shapes=cache,
        compiler_params=_cparams("arbitrary"),
        name="mm_resnorm",
    )(*args)
    return outs[0], list(outs[1:])


def _rglru_body(u_ref, conv8_ref, hprev_ref, cw_ref, cb_ref, wa_ref, ba_ref, wx_ref, bx_ref, lam_ref,
                y_ref, convo_ref, ho_ref, carry_h, carry_conv, *, bb, tt):
    j = pl.program_id(1)

    @pl.when(j == 0)
    def _():
        carry_h[...] = hprev_ref[...]
        carry_conv[...] = conv8_ref[...]

    u = u_ref[...]
    xb = u[:, :, :D_RNN]
    yb = u[:, :, D_RNN:]
    ext = jnp.concatenate([carry_conv[...], xb], axis=1)
    cw = cw_ref[...]
    xc = cb_ref[...] + ext[:, 5:5 + tt] * cw[0:1]
    for k in range(1, CONV_W):
        xc = xc + ext[:, 5 + k:5 + k + tt] * cw[k:k + 1]
    last8 = xb[:, tt - 8:tt]
    carry_conv[...] = last8
    convo_ref[...] = last8

    xc2 = xc.reshape(bb * tt, D_RNN)
    xcb = xc2.astype(BF16)

    def gate(w_ref, b_ref):
        cols = [_dot(xcb[:, n * LRU_BW:(n + 1) * LRU_BW], w_ref[n]) for n in range(LRU_BLOCKS)]
        return jax.nn.sigmoid(jnp.concatenate(cols, axis=1) + b_ref[...])

    r = gate(wa_ref, ba_ref)
    ig = gate(wx_ref, bx_ref)
    nl = -lam_ref[...]
    z = jnp.exp(-jnp.abs(nl))
    w1 = 1.0 + z
    log1p_z = jnp.where(w1 == 1.0, z, jnp.log(w1) * (z / jnp.where(w1 == 1.0, 1.0, w1 - 1.0)))
    softplus = jnp.maximum(nl, 0.0) + log1p_z
    log_a = -LRU_C * r * softplus
    a = jnp.exp(log_a)
    b = jnp.sqrt(1.0 - a * a) * (ig * xc2)

    a = a.reshape(bb * tt // 8, 8, D_RNN)
    b = b.reshape(bb * tt // 8, 8, D_RNN)
    row8 = lax.broadcasted_iota(jnp.int32, a.shape, 1)
    for d in (1, 2, 4):
        a_s = pltpu.roll(a, d, 1)
        b_s = pltpu.roll(b, d, 1)
        keep = row8 >= d
        b = jnp.where(keep, a * b_s + b, b)
        a = jnp.where(keep, a * a_s, a)

    a3 = a.reshape(bb, tt, D_RNN)
    b3 = b.reshape(bb, tt, D_RNN)
    carry = carry_h[...]
    groups = []
    for g in range(tt // 8):
        hg = a3[:, 8 * g:8 * g + 8] * carry + b3[:, 8 * g:8 * g + 8]
        carry = hg[:, 7:8]
        groups.append(hg)
    hs = groups[0] if len(groups) == 1 else jnp.concatenate(groups, axis=1)
    carry_h[...] = carry
    ho_ref[...] = carry
    y_ref[...] = (hs * _gelu_tanh(yb)).astype(y_ref.dtype)


def rglru(u, conv8, hprev, cw, cb, wa, ba, wx, bx, lam):
    bn, s, _ = u.shape
    if s >= 256:
        bb, tt = 1, 256
    else:
        bb, tt = min(bn, 512 // s), s
    assert s % tt == 0 and bn % bb == 0 and tt % 8 == 0
    vec = lambda a: a.reshape(1, D_RNN)
    c2 = lambda i, j: (0, 0)
    c3 = lambda i, j: (0, 0, 0)
    return pl.pallas_call(
        functools.partial(_rglru_body, bb=bb, tt=tt),
        grid=(bn // bb, s // tt),
        in_specs=[pl.BlockSpec((bb, tt, 2 * D_RNN), lambda i, j: (i, j, 0)),
                  pl.BlockSpec((bb, 8, D_RNN), lambda i, j: (i, 0, 0)),
                  pl.BlockSpec((bb, 1, D_RNN), lambda i, j: (i, 0, 0)),
                  pl.BlockSpec((CONV_W, D_RNN), c2),
                  pl.BlockSpec((1, D_RNN), c2),
                  pl.BlockSpec((LRU_BLOCKS, LRU_BW, LRU_BW), c3),
                  pl.BlockSpec((1, D_RNN), c2),
                  pl.BlockSpec((LRU_BLOCKS, LRU_BW, LRU_BW), c3),
                  pl.BlockSpec((1, D_RNN), c2),
                  pl.BlockSpec((1, D_RNN), c2)],
        out_specs=[pl.BlockSpec((bb, tt, D_RNN), lambda i, j: (i, j, 0)),
                   pl.BlockSpec((bb, 8, D_RNN), lambda i, j: (i, 0, 0)),
                   pl.BlockSpec((bb, 1, D_RNN), lambda i, j: (i, 0, 0))],
        out_shape=[jax.ShapeDtypeStruct((bn, s, D_RNN), BF16),
                   jax.ShapeDtypeStruct((bn, 8, D_RNN), F32),
                   jax.ShapeDtypeStruct((bn, 1, D_RNN), F32)],
        scratch_shapes=[pltpu.VMEM((bb, 1, D_RNN), F32), pltpu.VMEM((bb, 8, D_RNN), F32)],
        compiler_params=_cparams("arbitrary", "arbitrary"),
        name="rglru",
    )(u, conv8, hprev, cw, vec(cb), wa, vec(ba), wx, vec(bx), vec(lam))


def _bias_body(tb_ref, d_ref, o_ref):
    h = pl.program_id(0)
    d = d_ref[...]
    dc = jnp.maximum(d, 0)
    max_exact = N_BUCKETS // 2
    scaled = jnp.log(jnp.maximum(dc, 1).astype(F32) / max_exact) / math.log(MAX_DIST / max_exact)
    large = jnp.minimum(max_exact + (scaled * (N_BUCKETS - max_exact)).astype(jnp.int32), N_BUCKETS - 1)
    bucket = jnp.where(dc < max_exact, dc, large)
    acc = jnp.full(d.shape, tb_ref[0, h], F32)
    for k in range(1, N_BUCKETS):
        acc = jnp.where(bucket == k, tb_ref[k, h], acc)
    o_ref[...] = jnp.where(d < 0, NEG, acc)


def rel_bias_tiles(table, dist):
    rows, cols = dist.shape
    return pl.pallas_call(
        _bias_body,
        grid=(N_HEADS,),
        in_specs=[pl.BlockSpec(memory_space=pltpu.SMEM), pl.BlockSpec((rows, cols), lambda h: (0, 0))],
        out_specs=pl.BlockSpec((None, rows, cols), lambda h: (h, 0, 0)),
        out_shape=jax.ShapeDtypeStruct((N_HEADS, rows, cols), F32),
        compiler_params=_cparams("arbitrary"),
        name="rel_bias_tiles",
    )(table, dist)


def _reorder_emit(sources, stage, poslo_ref, poshi_ref, lo_ref, hi_ref):
    n_col = stage.shape[0]
    row = 0
    for src in sources:
        n = src.shape[0]
        for c in range(n_col):
            stage[c, row:row + n, :] = src[:, c * LANES:(c + 1) * LANES]
        row += n
    n16 = row // CMP_STRIDE
    for r in range(CMP_STRIDE):
        x = jnp.concatenate([stage[c, pl.ds(r, n16, stride=CMP_STRIDE), :] for c in range(n_col)], axis=1)
        lo_ref[r] = (x + poslo_ref[r:r + 1]).astype(lo_ref.dtype)
        hi_ref[r] = (x + poshi_ref[r:r + 1]).astype(hi_ref.dtype)


def _reorder_dense_body(x_ref, poslo_ref, poshi_ref, lo_ref, hi_ref, stage):
    _reorder_emit([x_ref], stage, poslo_ref, poshi_ref, lo_ref, hi_ref)


def _reorder_paged_body(pt_ref, *refs, n_pages):
    del pt_ref
    pages = refs[:n_pages]
    poslo_ref, poshi_ref, lo_ref, hi_ref, stage = refs[n_pages:]
    _reorder_emit(pages, stage, poslo_ref, poshi_ref, lo_ref, hi_ref)


def reorder_dense(kv, poslo, poshi):
    bn, s, _ = kv.shape
    n16 = s // CMP_STRIDE
    shp = jax.ShapeDtypeStruct((bn, CMP_STRIDE, n16, 512), BF16)
    ospec = pl.BlockSpec((None, CMP_STRIDE, n16, 512), lambda b: (b, 0, 0, 0))
    pspec = pl.BlockSpec((CMP_STRIDE, 512), lambda b: (0, 0))
    return pl.pallas_call(
        _reorder_dense_body,
        grid=(bn,),
        in_specs=[pl.BlockSpec((None, s, 512), lambda b: (b, 0, 0)), pspec, pspec],
        out_specs=[ospec, ospec],
        out_shape=[shp, shp],
        scratch_shapes=[pltpu.VMEM((512 // LANES, s, LANES), F32)],
        compiler_params=_cparams("arbitrary"),
        name="reorder_dense",
    )(kv, poslo, poshi)


def reorder_paged(cache, page_table, poslo, poshi):
    bn, n_pages = page_table.shape
    page = cache.shape[1]
    per_page = page // CMP_STRIDE
    n16 = n_pages * per_page
    shp = jax.ShapeDtypeStruct((bn, CMP_STRIDE, n16, 512), BF16)
    ospec = pl.BlockSpec((None, CMP_STRIDE, n16, 512), lambda b, pt: (b, 0, 0, 0))
    pspec = pl.BlockSpec((CMP_STRIDE, 512), lambda b, pt: (0, 0))
    page_specs = [pl.BlockSpec((None, page, 512), functools.partial(lambda b, pt, k: (pt[b * n_pages + k], 0, 0), k=k))
                  for k in range(n_pages)]
    return pl.pallas_call(
        functools.partial(_reorder_paged_body, n_pages=n_pages),
        grid_spec=pltpu.PrefetchScalarGridSpec(
            num_scalar_prefetch=1, grid=(bn,),
            in_specs=page_specs + [pspec, pspec],
            out_specs=[ospec, ospec],
            scratch_shapes=[pltpu.VMEM((512 // LANES, n_pages * page, LANES), F32)]),
        out_shape=[shp, shp],
        compiler_params=_cparams("arbitrary"),
        name="reorder_paged",
    )(page_table.reshape(-1), *([cache] * n_pages), poslo, poshi)


def _compress_body(lo_ref, hi_ref, wlo_ref, whi_ref, w2_ref, o_ref, acc1, acc2, *, rows):
    r = pl.program_id(2)

    @pl.when(r == 0)
    def _():
        acc1[...] = jnp.zeros_like(acc1)
        acc2[...] = jnp.zeros_like(acc2)

    a1 = acc1[...]
    a2 = acc2[...]
    for k in range(CMP_RSTEP):
        a1 = a1 + _dot(lo_ref[:, k].reshape(rows, 256), wlo_ref[k])
        a2 = a2 + _dot(hi_ref[:, k].reshape(rows, 256), whi_ref[k])
    acc1[...] = a1
    acc2[...] = a2

    @pl.when(r == CMP_STRIDE // CMP_RSTEP - 1)
    def _():
        hidden = _gelu_tanh(acc1[...] + pltpu.roll(acc2[...], rows - 1, 0))
        o_ref[...] = _dot(hidden.astype(BF16), w2_ref[...]).reshape(o_ref.shape).astype(o_ref.dtype)


def compress(xlo, xhi, wlo, whi, w2bd, sb):
    bn, _, n16, _ = xlo.shape
    assert bn % sb == 0
    rows = sb * n16
    xspec = pl.BlockSpec((sb, CMP_RSTEP, n16, 256), lambda g, s, r: (g, r, 0, s))
    wspec = pl.BlockSpec((None, CMP_RSTEP, 256, 512), lambda g, s, r: (s, r, 0, 0))
    return pl.pallas_call(
        functools.partial(_compress_body, rows=rows),
        grid=(bn // sb, 2, CMP_STRIDE // CMP_RSTEP),
        in_specs=[xspec, xspec, wspec, wspec, pl.BlockSpec((None, 512, 256), lambda g, s, r: (s, 0, 0))],
        out_specs=pl.BlockSpec((sb, n16, 256), lambda g, s, r: (g, 0, s)),
        out_shape=jax.ShapeDtypeStruct((bn, n16, 512), BF16),
        scratch_shapes=[pltpu.VMEM((rows, 512), F32), pltpu.VMEM((rows, 512), F32)],
        compiler_params=_cparams("arbitrary", "arbitrary", "arbitrary"),
        name="compress",
    )(xlo, xhi, wlo, whi, w2bd)


def _assemble_body(pt_ref, *refs, n_pages, page, s_new, win_len):
    del pt_ref
    pages = refs[:n_pages]
    new_slc_ref, new_win_ref, cwin_ref, slc_ref, win_ref = refs[n_pages:]
    for k in range(n_pages):
        slc_ref[k * page:(k + 1) * page, :] = pages[k][...].astype(slc_ref.dtype)
    pad = jnp.zeros((SLC_LEN - s_new, 512), F32)
    slc_ref[n_pages * page:n_pages * page + SLC_LEN, :] = jnp.concatenate(
        [new_slc_ref[...], pad], axis=0).astype(slc_ref.dtype)
    win_ref[0:win_len, :] = cwin_ref[...].astype(win_ref.dtype)
    win_ref[win_len:win_len + SLC_LEN, :] = jnp.concatenate([new_win_ref[...], pad], axis=0).astype(win_ref.dtype)


def assemble_sample_kv(cache, page_table, kv_new, cache_win):
    bn, n_pages = page_table.shape
    page = cache.shape[1]
    s_new = kv_new.shape[1]
    win_len = cache_win.shape[1]
    past = n_pages * page
    page_specs = [pl.BlockSpec((None, page, 512), functools.partial(lambda b, pt, k: (pt[b * n_pages + k], 0, 1), k=k))
                  for k in range(n_pages)]
    return pl.pallas_call(
        functools.partial(_assemble_body, n_pages=n_pages, page=page, s_new=s_new, win_len=win_len),
        grid_spec=pltpu.PrefetchScalarGridSpec(
            num_scalar_prefetch=1, grid=(bn,),
            in_specs=page_specs + [pl.BlockSpec((None, s_new, 512), lambda b, pt: (b, 0, 1)),
                                   pl.BlockSpec((None, s_new, 512), lambda b, pt: (b, 0, 2)),
                                   pl.BlockSpec((None, win_len, 512), lambda b, pt: (b, 0, 0))],
            out_specs=[pl.BlockSpec((None, past + SLC_LEN, 512), lambda b, pt: (b, 0, 0)),
                       pl.BlockSpec((None, win_len + SLC_LEN, 512), lambda b, pt: (b, 0, 0))]),
        out_shape=[jax.ShapeDtypeStruct((bn, past + SLC_LEN, 512), BF16),
                   jax.ShapeDtypeStruct((bn, win_len + SLC_LEN, 512), BF16)],
        compiler_params=_cparams("arbitrary"),
        name="assemble_sample_kv",
    )(page_table.reshape(-1), *([cache] * n_pages), kv_new, kv_new, cache_win)


def _value_heads(vv):
    own_a = lax.broadcasted_iota(jnp.int32, vv.shape, 1) < HEAD_DIM
    return jnp.where(own_a, vv, 1.0), jnp.where(own_a, 1.0, vv)


def _softmax_update(state, s, vv, rr):
    smax = jnp.max(s, axis=-1, keepdims=True)
    mn = smax if state is None else jnp.maximum(state[0], smax)
    e = jnp.exp2(s - mn).astype(BF16)
    va, vb = _value_heads(vv)
    pv = jnp.concatenate([_dot(e[:rr], va), _dot(e[rr:], vb)], axis=0)
    return mn, (pv if state is None else jnp.exp2(state[0] - mn) * state[1] + pv)


def _softmax_finish(state):
    _, acc = state
    return acc / pltpu.roll(acc, HEAD_DIM, 1)


def _attn_seq(qi, q_ref, gl_ref, kc_ref, vc_ref, ks_ref, vs_ref, kw_ref, vw_ref, augk_ref, augw_ref,
              ovt_ref, cb_ref, nb_ref, wb_ref, augc_ref, ex_ref, o_ref,
              *, qb, n_slc, top_n, pad_s, t0, multi_block, n_streams):
    rr = GQA * qb
    r2 = 2 * rr
    pairs = range(N_PAIRS)
    is_a = lax.broadcasted_iota(jnp.int32, (rr, LANES), 1) < HEAD_DIM
    if multi_block:
        cur = t0 // SLC_LEN + qi
        win_start = pl.multiple_of(qi * qb, SLC_LEN)
    else:
        cur = t0 // SLC_LEN
        win_start = 0

    def cols(ref, p, start, size):
        return ref[pl.ds(start, size), p * LANES:(p + 1) * LANES]

    def init():
        return jnp.full((r2, 1), -jnp.inf, F32), jnp.zeros((r2, LANES), F32)

    wlen = WINDOW + SLC_LEN
    augw = augw_ref[pl.ds(win_start, wlen), :]
    glt = jnp.concatenate([gl_ref[...]] * GQA, axis=0)
    slab = lax.broadcasted_iota(jnp.int32, (rr, LANES), 0) // qb
    gsrc = jnp.concatenate([jnp.where(slab == g, glt, 0.0) for g in range(GQA)], axis=1)
    ghi = gsrc.astype(BF16)
    glo = (gsrc - ghi.astype(F32)).astype(BF16)

    q2, s_win, s_cmp, o_win, gates, o_cmp, imp_t = [], [], [], [], [], [], []
    for p in pairs:
        qfull = q_ref[:, p * GQA * LANES:(p + 1) * GQA * LANES]
        qs = jnp.concatenate([qfull[:, g * LANES:(g + 1) * LANES] for g in range(GQA)], axis=0)
        qs = qs * (HEAD_DIM ** -0.5 * LOG2E)
        q2.append(jnp.concatenate([jnp.where(is_a, qs, 0.0), jnp.where(is_a, 0.0, qs)], axis=0).astype(BF16))
        lhs_win = jnp.concatenate([q2[p], augc_ref[p, 1].astype(BF16)], axis=1)
        kk = jnp.concatenate([cols(kw_ref, p, win_start, wlen), augw], axis=1)
        s_win.append(_dot_nt(lhs_win, kk) + wb_ref[p])
        s_cmp.append(_dot_nt(q2[p], kc_ref[:, p * LANES:(p + 1) * LANES]) + cb_ref[p])

    for p in pairs:
        o_win.append(_softmax_finish(_softmax_update(None, s_win[p], cols(vw_ref, p, win_start, wlen), rr)))

        gates.append(jax.nn.sigmoid(_dot(ghi, ex_ref[p]) + _dot(glo, ex_ref[p])))

        s = s_cmp[p]
        m = jnp.max(s, axis=-1, keepdims=True)
        e = jnp.exp2(s - m)
        pr = e / jnp.sum(e, axis=-1, keepdims=True)
        pr = jnp.where(m > 0.1 * NEG, pr, 0.0)
        o_cmp.append(_dot(pr.astype(BF16), vc_ref[:, p * LANES:(p + 1) * LANES]))

        pcat = jnp.concatenate(
            [jnp.concatenate([pr[x * rr + g * qb:x * rr + (g + 1) * qb] for g in range(GQA)], axis=1)
             for x in range(2)], axis=0)
        if 2 * qb < LANES:
            pcat = jnp.concatenate([pcat, jnp.zeros((LANES - 2 * qb, pcat.shape[1]), F32)], axis=0)
        imp_t.append(_dot_nt(ovt_ref[...], pcat.astype(BF16)))

    def ranked():
        jr = lax.broadcasted_iota(jnp.int32, (N_BLOCK_LANES, LANES), 0)
        forced = (jr == 0) | (jr == cur) | (jr == cur - 1)
        sub8 = lax.broadcasted_iota(jnp.int32, (8, LANES), 0)
        res = []
        for p in pairs:
            v = jnp.where(forced, FORCE, jnp.where((jr > cur) | (jr >= n_slc), NEG, imp_t[p][:N_BLOCK_LANES]))
            groups = [v[8 * r:8 * r + 8] for r in range(N_BLOCK_LANES // 8)]
            ranks = [jnp.zeros((8, LANES), F32) for _ in groups]
            for k in range(n_slc):
                vk = v[k:k + 1, :]
                for r, vr in enumerate(groups):
                    if 8 * r > k:
                        ranks[r] = ranks[r] + jnp.where(vk >= vr, 1.0, 0.0)
                    elif 8 * r + 7 < k:
                        ranks[r] = ranks[r] + jnp.where(vk > vr, 1.0, 0.0)
                    else:
                        ranks[r] = ranks[r] + jnp.where(sub8 > k - 8 * r, jnp.where(vk >= vr, 1.0, 0.0),
                                                        jnp.where(vk > vr, 1.0, 0.0))
            res.append(jnp.where(jnp.concatenate(ranks, axis=0) < top_n, 0.0, 1.0))
        return tuple(res)

    if multi_block:
        all_selected = lambda: tuple(jnp.zeros((N_BLOCK_LANES, LANES), F32) for _ in pairs)
        notsel_t = lax.cond(cur >= top_n, ranked, all_selected)
    else:
        notsel_t = ranked()

    st = pad_s + (cur - 2) * SLC_LEN
    if multi_block:
        st = pl.multiple_of(st, SLC_LEN)
    augk_near = augk_ref[pl.ds(st, 3 * SLC_LEN), :]
    lane2 = lax.broadcasted_iota(jnp.int32, (r2, LANES), 1)
    lhs_far, s_near = [], []
    for p in pairs:
        notsel = jnp.concatenate([notsel_t[p], jnp.zeros((LANES - N_BLOCK_LANES, LANES), F32)], axis=0).T
        ns = jnp.concatenate([notsel[x * qb:(x + 1) * qb] for x in range(2) for _ in range(GQA)], axis=0)
        ns_far = jnp.where((lane2 >= cur - 2) & (lane2 < N_BLOCK_LANES), 1.0, ns)
        lhs_far.append(jnp.concatenate([q2[p], (ns_far + augc_ref[p, 0]).astype(BF16)], axis=1))
        lhs_near = jnp.concatenate([q2[p], (ns + augc_ref[p, 1]).astype(BF16)], axis=1)
        kk = jnp.concatenate([cols(ks_ref, p, st, 3 * SLC_LEN), augk_near], axis=1)
        s_near.append(_dot_nt(lhs_near, kk) + nb_ref[p])
    near = [_softmax_update(None, s_near[p], cols(vs_ref, p, st, 3 * SLC_LEN), rr) for p in pairs]

    def far_step(it, states):
        starts, scores = [], []
        for k in range(n_streams):
            st = pl.multiple_of(pad_s + (it * n_streams + k) * FAR_CHUNK, FAR_CHUNK)
            augk = augk_ref[pl.ds(st, FAR_CHUNK), :]
            starts.append(st)
            for p in pairs:
                kk = jnp.concatenate([cols(ks_ref, p, st, FAR_CHUNK), augk], axis=1)
                scores.append(_dot_nt(lhs_far[p], kk))
        return tuple(_softmax_update(states[k * N_PAIRS + p], scores[k * N_PAIRS + p],
                                     cols(vs_ref, p, starts[k], FAR_CHUNK), rr)
                     for k in range(n_streams) for p in pairs)

    per_iter = n_streams * (FAR_CHUNK // SLC_LEN)
    if multi_block:
        states = tuple(init() for _ in range(n_streams * N_PAIRS))
        states = lax.fori_loop(0, (jnp.maximum(cur - 2, 0) + per_iter - 1) // per_iter, far_step, states)
    else:
        states = (None,) * (n_streams * N_PAIRS)
        for it in range((max(cur - 2, 0) + per_iter - 1) // per_iter):
            states = far_step(it, states)

    outs = []
    for p in pairs:
        mine = tuple(st for st in states[p::N_PAIRS] if st is not None) + (near[p],)
        m_all = functools.reduce(jnp.maximum, [m for m, _ in mine])
        o_slc = _softmax_finish((m_all, sum(acc * jnp.exp2(m - m_all) for m, acc in mine)))
        out = None
        for br, o2 in enumerate((o_cmp[p], o_slc, o_win[p])):
            term = gates[p][:, br * LANES:(br + 1) * LANES] * jnp.where(is_a, o2[:rr], o2[rr:])
            out = term if out is None else out + term
        outs += [out[g * qb:(g + 1) * qb] for g in range(GQA)]
    o_ref[...] = jnp.concatenate(outs, axis=1).astype(o_ref.dtype)


def _attn_body(*refs, sbq, **kw):
    qi = pl.program_id(1)
    per_seq, shared = refs[:8], refs[8:16]
    for i in range(sbq):
        _attn_seq(qi, *[r.at[i] for r in per_seq], *shared, refs[16].at[i], **kw)


def _cmp_bias_body(cbe_ref, o_ref, *, nc, step):
    tau = pl.program_id(0) * step
    for p in range(N_PAIRS):
        o_ref[p] = pltpu.roll(cbe_ref[p], tau, 1)[:, nc:]


def cmp_bias_per_block(cbe, nqb, step):
    n_p, rows, nc2 = cbe.shape
    nc = nc2 // 2
    return pl.pallas_call(
        functools.partial(_cmp_bias_body, nc=nc, step=step),
        grid=(nqb,),
        in_specs=[pl.BlockSpec(cbe.shape, lambda i: (0, 0, 0))],
        out_specs=pl.BlockSpec((None, n_p, rows, nc), lambda i: (i, 0, 0, 0)),
        out_shape=jax.ShapeDtypeStruct((nqb, n_p, rows, nc), F32),
        compiler_params=_cparams("arbitrary"),
        name="cmp_bias_per_block",
    )(cbe)


def nsa_attention(u, cmp_kv, slc_kv, slc_cols, win_kv, win_cols, augk, augw, ovt, tiles, ex,
                  *, qb, n_slc, pad_s, t0, sbq):
    bn, s, _ = u.shape
    nqb = s // qb
    nc = cmp_kv.shape[1]
    r2 = 2 * GQA * qb
    cb, nb, wb, augc = tiles
    srows = slc_kv.shape[1]
    wrows = win_kv.shape[1]
    kcol, vcol = slc_cols
    kwcol, vwcol = win_cols
    assert qb == SLC_LEN or nqb == 1
    assert t0 % SLC_LEN == 0 and bn % sbq == 0
    per_chunk = FAR_CHUNK // SLC_LEN
    max_chunks = -(-max(n_slc - 3, 0) // per_chunk)
    n_streams = min(1, max_chunks) if nqb > 1 else max_chunks
    n_streams = max(n_streams, 1)
    assert pad_s + -(-max_chunks // n_streams) * n_streams * FAR_CHUNK <= srows
    body = functools.partial(_attn_body, sbq=sbq, qb=qb, n_slc=n_slc, top_n=min(N_SELECT, n_slc),
                             pad_s=pad_s, t0=t0, multi_block=nqb > 1, n_streams=n_streams)
    whole = lambda a: pl.BlockSpec(a.shape, lambda b, i: (0,) * a.ndim)
    kvw = N_PAIRS * LANES
    nq = N_HEADS * HEAD_DIM
    return pl.pallas_call(
        body,
        grid=(bn // sbq, nqb),
        in_specs=[pl.BlockSpec((sbq, qb, nq), lambda b, i: (b, i, 0)),
                  pl.BlockSpec((sbq, qb, LANES), lambda b, i: (b, i, nq // LANES)),
                  pl.BlockSpec((sbq, nc, kvw), lambda b, i: (b, 0, 0)),
                  pl.BlockSpec((sbq, nc, kvw), lambda b, i: (b, 0, 1)),
                  pl.BlockSpec((sbq, srows, kvw), lambda b, i: (b, 0, kcol)),
                  pl.BlockSpec((sbq, srows, kvw), lambda b, i: (b, 0, vcol)),
                  pl.BlockSpec((sbq, wrows, kvw), lambda b, i: (b, 0, kwcol)),
                  pl.BlockSpec((sbq, wrows, kvw), lambda b, i: (b, 0, vwcol)),
                  whole(augk), whole(augw), whole(ovt),
                  pl.BlockSpec((None, N_PAIRS, r2, nc), lambda b, i: (i, 0, 0, 0)),
                  whole(nb), whole(wb), whole(augc), whole(ex)],
        out_specs=pl.BlockSpec((sbq, qb, nq), lambda b, i: (b, i, 0)),
        out_shape=jax.ShapeDtypeStruct((bn, s, nq), F32),
        compiler_params=_cparams("arbitrary", "arbitrary"),
        name="nsa_attention",
    )(u, u, cmp_kv, cmp_kv, slc_kv, slc_kv, win_kv, win_kv, augk, augw, ovt, cb, nb, wb, augc, ex)


def _head_perm():
    idx = np.empty((N_PAIRS, GQA, 2, HEAD_DIM), np.int32)
    for p in range(N_PAIRS):
        for g in range(GQA):
            for half in range(2):
                head = (2 * p + half) * GQA + g
                idx[p, g, half] = head * HEAD_DIM + np.arange(HEAD_DIM)
    return idx.reshape(-1)


def _gate_expand():
    ex = np.zeros((N_PAIRS, GQA * LANES, 3 * LANES), np.float32)
    for p in range(N_PAIRS):
        for g in range(GQA):
            for half in range(2):
                head = (2 * p + half) * GQA + g
                for br in range(3):
                    ex[p, g * LANES + head * 3 + br, br * LANES + half * HEAD_DIM:br * LANES + (half + 1) * HEAD_DIM] = 1.0
    return ex


def _overlap_t(nc, n_cmp, n_slc):
    i = np.arange(nc)[None, :] * CMP_STRIDE
    j = np.arange(LANES)[:, None] * SLC_LEN
    ov = (i < j + SLC_LEN) & (i + CMP_LEN > j) & (np.arange(nc)[None, :] < n_cmp) & (np.arange(LANES)[:, None] < n_slc)
    return np.tile(ov.astype(np.float32), (1, GQA))


def _aug_keys(pad, n_keys):
    a = np.zeros((pad + n_keys, LANES), np.float32)
    a[:pad, AUG_PAD] = NEG
    k = np.arange(n_keys)
    a[pad + k, k // SLC_LEN] = NEG
    a[pad:, AUG_BIAS_HI] = 1.0
    a[pad:, AUG_BIAS_LO] = 1.0
    return a


def _attn_tiles(rel_bias, qb, t0, nc, nqb):
    rolled = nqb > 1
    i = np.arange(qb)[:, None]
    near = np.arange(3 * SLC_LEN)[None, :]
    d_near = SLC_LEN * (2 - near // SLC_LEN) + i - near % SLC_LEN
    c = np.arange(WINDOW + SLC_LEN)[None, :]
    d_win = i + WINDOW - c
    d_win = np.where(d_win < WINDOW, d_win, -1)
    if rolled:
        e = np.arange(2 * nc)[None, :] - nc
        d_cmp = i - (CMP_LEN - 1) - CMP_STRIDE * e
    else:
        d_cmp = t0 + i - CMP_STRIDE * np.arange(nc)[None, :] - (CMP_LEN - 1)
    widths = [d_near.shape[1], d_win.shape[1], d_cmp.shape[1]]
    padded = [-(-w // LANES) * LANES for w in widths]
    dist = np.full((qb, sum(padded)), -1, np.int32)
    off = 0
    offs = []
    for d, w, pw in zip((d_near, d_win, d_cmp), widths, padded):
        dist[:, off:off + w] = np.clip(d, -1, 4 * MAX_DIST)
        offs.append(off)
        off += pw
    tiles = rel_bias_tiles(rel_bias, jnp.asarray(dist))
    r2 = 2 * GQA * qb
    tiles = tiles.reshape(N_PAIRS, r2, dist.shape[1]) * LOG2E
    nb, wb, cbe = (tiles[..., o:o + w] for o, w in zip(offs, widths))
    if rolled:
        cb = cmp_bias_per_block(cbe, nqb, (t0 + qb) // CMP_STRIDE - t0 // CMP_STRIDE)
    else:
        cb = cbe[None]
    far = rel_bias[N_BUCKETS - 1].reshape(N_PAIRS, 2 * GQA) * LOG2E
    hi = far.astype(BF16).astype(F32)
    lo = far - hi
    rows = jnp.repeat(jnp.stack([hi, lo], axis=-1), qb, axis=1)
    augc = jnp.zeros((N_PAIRS, 2, r2, LANES), F32)
    augc = augc.at[:, 0, :, AUG_BIAS_HI].set(rows[..., 0]).at[:, 0, :, AUG_BIAS_LO].set(rows[..., 1])
    augc = augc.at[:, 1, :, AUG_PAD].set(1.0)
    return cb, nb, wb, augc


def _forward(x, ada, ada_kv, conv_state, h_state, past, w, shared):
    bn, s, _ = x.shape
    m = bn * s
    xf = x.reshape(m, D_MODEL)
    new_conv, new_h = [], []
    h = normmod(xf, w["norm_g"][0, 0], ada[0], 0, 1, bn, s)
    kv = None
    attn_args = None
    for l in range(DEPTH):
        if l < N_A_LAYERS:
            u = mm_plain(h, (w["w_in_a"], l)).reshape(bn, s, 2 * D_RNN)
            conv8 = jnp.pad(conv_state[l], ((0, 0), (8 - (CONV_W - 1), 0), (0, 0)))
            y, conv_o, h_o = rglru(u, conv8, h_state[l][:, None, :], w["conv_w"][l], w["conv_b"][l],
                                   w["w_rg_a"][l], w["b_rg_a"][l], w["w_rg_x"][l], w["b_rg_x"][l], w["lru_lambda"][l])
            new_conv.append(conv_o[:, 8 - (CONV_W - 1):])
            new_h.append(h_o[:, 0])
            a_in = y.reshape(m, D_RNN)
            w_out = (w["w_out_a"], l)
        else:
            lb = l - N_A_LAYERS
            if lb == 0:
                if past is None:
                    kv, kvb = mm_kv_padded(h_kv, w["w_kv"], bn, s, WINDOW)
                else:
                    kv, kvb = mm_plain(h_kv, w["w_kv"]), None
                kv = kv.reshape(bn, s, 6 * N_KV_HEADS * HEAD_DIM)
                attn_args = _prepare_attention(kv, kvb, past, w, shared, bn, s)
            u = mm_plain(h, (w["w_in_b"], lb)).reshape(bn, s, -1)
            a_in = nsa_attention(u, *attn_args[0], **attn_args[1]).reshape(m, N_HEADS * HEAD_DIM)
            w_out = (w["w_out_b"], lb)
        x_new, (h2,) = mm_resnorm(a_in, w_out, xf, (ada[l], 2), w["norm_g"][l, 1],
                                  [(w["norm_g"][l, 2], ada[l], 3, ada[l], 4)], bn, s)
        f = mm_swiglu(h2, (w["w_ffn_in"], l))
        heads = []
        if l + 1 < DEPTH:
            heads.append((w["norm_g"][l + 1, 0], ada[l + 1], 0, ada[l + 1], 1))
        if l + 1 == N_A_LAYERS:
            heads.append((w["norm_kv"], ada_kv, 0, ada_kv, 1))
        xf, hs = mm_resnorm(f, (w["w_ffn_out"], l), x_new, (ada[l], 5), w["norm_g"][l, 3], heads, bn, s)
        if heads:
            h = hs[0]
        if l + 1 == N_A_LAYERS:
            h_kv = hs[1]
    kv_rows = kv[:, :, :4 * N_KV_HEADS * HEAD_DIM].reshape(bn, s, 4, N_KV_HEADS, HEAD_DIM)
    new_win = kv[:, :, 4 * N_KV_HEADS * HEAD_DIM:].reshape(bn, s, 2, N_KV_HEADS, HEAD_DIM)
    if past is None:
        win_state = new_win[:, -min(WINDOW, s):]
    else:
        cache_win = past[2]
        win_state = jnp.concatenate([cache_win, new_win], axis=1)[:, -cache_win.shape[1]:]
    return xf.reshape(bn, s, D_MODEL), kv_rows, win_state, jnp.stack(new_conv), jnp.stack(new_h)


def _prepare_attention(kv, kvb, past, w, shared, bn, s):
    rel_bias = w["rel_bias"]
    if past is None:
        t0, qb, pad = 0, Q_BLOCK, WINDOW
        n_keys = s
        xlo, xhi = reorder_dense(kv, shared["poslo"], shared["poshi"])
        slc_kv = win_kv = kvb
        slc_cols, win_cols = (2, 3), (4, 5)
        augk = jnp.asarray(_aug_keys(pad, n_keys), BF16)
        augw = augk
        sb = bn
    else:
        cache, page_table, cache_win = past
        n_pool, page = cache.shape[:2]
        cache2 = cache.reshape(n_pool, page, -1)
        t0 = page_table.shape[1] * page
        qb, pad = s, 0
        n_keys = t0 + SLC_LEN
        assert cache_win.shape[1] == WINDOW and t0 >= WINDOW
        xlo, xhi = reorder_paged(cache2, page_table, shared["poslo"], shared["poshi"])
        slc_kv, win_kv = assemble_sample_kv(cache2, page_table, kv, cache_win.reshape(bn, WINDOW, -1))
        slc_cols, win_cols = (0, 1), (0, 1)
        augk = jnp.asarray(_aug_keys(0, n_keys), BF16)
        augw = jnp.zeros((win_kv.shape[1], LANES), BF16)
        sb = 8
    n16 = (t0 + s) // CMP_STRIDE
    n_cmp = n16 - CMP_LEN // CMP_STRIDE + 1
    nc = xlo.shape[2]
    assert nc >= n_cmp and nc % LANES == 0
    n_slc = -(-(t0 + s) // SLC_LEN)
    assert n_slc <= N_BLOCK_LANES
    cmp_kv = compress(xlo, xhi, shared["wlo"], shared["whi"], shared["w2bd"], sb)
    nqb = s // qb
    assert nqb == 1 or t0 == 0
    tiles = _attn_tiles(rel_bias, qb, t0, nc, nqb)
    ovt = jnp.asarray(_overlap_t(nc, n_cmp, n_slc), BF16)
    args = (cmp_kv, slc_kv, slc_cols, win_kv, win_cols, augk, augw, ovt, tiles, shared["ex"])
    return args, dict(qb=qb, n_slc=n_slc, pad_s=pad, t0=t0, sbq=1 if nqb > 1 else min(bn, 2))


def kernel(x_prompt, x_sample, c_prompt, c_sample, cache_kv, cache_win, state_conv, state_h, page_table, w_ada, b_ada, norm_g, w_in_a, conv_w, conv_b, w_rg_a, b_rg_a, w_rg_x, b_rg_x, lru_lambda, w_out_a, w_ada_kv, b_ada_kv, norm_kv, w_kv, cmp_pos, cmp_w1, cmp_w2, w_in_b, w_out_b, rel_bias, w_ffn_in, w_ffn_out):
    bp = x_prompt.shape[0]
    bs = x_sample.shape[0]

    perm = _head_perm()
    n_q = N_HEADS * HEAD_DIM
    w_in_b_p = jnp.concatenate([w_in_b[:, :, :n_q][:, :, perm], w_in_b[:, :, n_q:],
                                jnp.zeros(w_in_b.shape[:2] + (LANES - 3 * N_HEADS,), F32)], axis=-1)
    w = dict(norm_g=norm_g, norm_kv=norm_kv, conv_w=conv_w, conv_b=conv_b, b_rg_a=b_rg_a, b_rg_x=b_rg_x,
             lru_lambda=lru_lambda, rel_bias=rel_bias,
             w_in_a=w_in_a, w_rg_a=w_rg_a.astype(BF16), w_rg_x=w_rg_x.astype(BF16),
             w_out_a=w_out_a, w_kv=w_kv, w_in_b=w_in_b_p,
             w_out_b=w_out_b[:, perm, :], w_ffn_in=w_ffn_in, w_ffn_out=w_ffn_out)

    eye = jnp.eye(N_KV_HEADS, dtype=F32)
    w1r = cmp_w1.reshape(2, CMP_LEN, HEAD_DIM, CMP_HIDDEN)
    w1bd = jnp.einsum("hk,srdj->srhdkj", eye, w1r).reshape(2, CMP_LEN, N_KV_HEADS * HEAD_DIM, N_KV_HEADS * CMP_HIDDEN)
    w2bd = jnp.einsum("hk,sjd->shjkd", eye, cmp_w2).reshape(2, N_KV_HEADS * CMP_HIDDEN, N_KV_HEADS * HEAD_DIM)
    pos = jnp.broadcast_to(cmp_pos[:, :, None, :], (2, CMP_LEN, N_KV_HEADS, HEAD_DIM)).reshape(2, CMP_LEN, -1)
    pos = jnp.concatenate([pos[0], pos[1]], axis=-1)
    shared = dict(wlo=w1bd[:, :CMP_STRIDE].astype(BF16), whi=w1bd[:, CMP_STRIDE:].astype(BF16),
                  w2bd=w2bd.astype(BF16), poslo=pos[:CMP_STRIDE], poshi=pos[CMP_STRIDE:],
                  ex=jnp.asarray(_gate_expand(), BF16))

    n_c = bp + bs
    c_all = jnp.concatenate([c_prompt, c_sample, jnp.zeros((-n_c % 8, D_MODEL), F32)], axis=0)
    ada_all = [mm_plain(c_all, (w_ada, l), bias=b_ada[l], act="silu", tn=2048) for l in range(DEPTH)]
    ada_kv_all = mm_plain(c_all, w_ada_kv, bias=b_ada_kv, act="silu", tn=2048)
    ada_p = [a[:bp, None, :] for a in ada_all]
    ada_s = [a[bp:n_c, None, :] for a in ada_all]

    zero_conv = jnp.zeros((N_A_LAYERS, bp, CONV_W - 1, D_RNN), F32)
    zero_h = jnp.zeros((N_A_LAYERS, bp, D_RNN), F32)
    y_p, kv_p, win_p, conv_p, h_p = _forward(x_prompt, ada_p, ada_kv_all[:bp, None, :], zero_conv, zero_h, None, w, shared)
    y_s, kv_s, win_s, conv_s, h_s = _forward(x_sample, ada_s, ada_kv_all[bp:n_c, None, :], state_conv, state_h,
                                             (cache_kv, page_table, cache_win), w, shared)
    return (y_p, y_s, kv_p, kv_s, win_p, win_s, conv_p, conv_s, h_p, h_s)
```

```python
import functools
import math

import numpy as np
import jax
import jax.numpy as jnp
from jax import lax
from jax.experimental import pallas as pl
from jax.experimental.pallas import tpu as pltpu

D_MODEL = 1024
DEPTH = 4
N_A_LAYERS = 2
D_RNN = 1024
LRU_BLOCKS = 8
LRU_BW = 128
CONV_W = 4
LRU_C = 8.0
N_HEADS = 16
HEAD_DIM = 64
N_KV_HEADS = 4
GQA = 4
CMP_LEN = 32
CMP_STRIDE = 16
CMP_HIDDEN = 128
SLC_LEN = 64
N_SELECT = 16
WINDOW = 512
Q_BLOCK = 64
N_BUCKETS = 32
MAX_DIST = 128
D_FF = 2816
EPS = 1e-6
NEG = -1e30
FORCE = 1e9

LANES = 128
PAIR_W = 2 * HEAD_DIM
N_PAIRS = N_KV_HEADS // 2
AUG_BIAS_HI = 64
AUG_BIAS_LO = 65
AUG_PAD = 66
N_BLOCK_LANES = 64
FAR_CHUNK = 1024
CMP_RSTEP = 4
RESNORM_PART = 512
LOG2E = 1.4426950408889634
VMEM_LIMIT = 56 * 1024 * 1024

F32 = jnp.float32
BF16 = jnp.bfloat16


def _cparams(*sem):
    return pltpu.CompilerParams(dimension_semantics=sem, vmem_limit_bytes=VMEM_LIMIT)


def _dot(a, b):
    return jnp.dot(a, b, preferred_element_type=F32)


def _dot_nt(a, b):
    return lax.dot_general(a, b, (((1,), (1,)), ((), ())), preferred_element_type=F32)


def _gelu_tanh(x):
    return 0.5 * x * (1.0 + jnp.tanh(math.sqrt(2.0 / math.pi) * (x + 0.044715 * (x * x * x))))


def _rms(x, gain):
    return x * lax.rsqrt(jnp.mean(x * x, axis=-1, keepdims=True) + EPS) * gain


def _rows(vec_ref, bb, ts):
    v = vec_ref[...]
    d = v.shape[-1]
    return jnp.broadcast_to(v, (bb, ts, d)).reshape(bb * ts, d)


def _vec_spec(bb, tiles_per_seq, col, ngrid):
    if ngrid == 1:
        return pl.BlockSpec((bb, 1, D_MODEL), lambda i: (i // tiles_per_seq, 0, col))
    return pl.BlockSpec((bb, 1, D_MODEL), lambda j, i: (i // tiles_per_seq, 0, col))


def _row_tiling(bn, s, tm=512):
    if s >= tm:
        assert s % tm == 0
        return tm, 1, tm, s // tm
    assert tm % s == 0 and s % 8 == 0
    bb = min(bn, tm // s)
    assert bn % bb == 0
    return bb * s, bb, s, 1


def _bf16_weight(w_ref, cache_ref, first_row_tile):
    if cache_ref is None:
        return w_ref[...].astype(BF16)

    @pl.when(first_row_tile)
    def _():
        cache_ref[...] = w_ref[...].astype(BF16)

    return cache_ref[...]


def _weight_cache(w, block, n_row_tiles):
    return [pltpu.VMEM(block, BF16)] if (w.dtype != BF16 and n_row_tiles > 1) else []


def _mm_body(*refs, act, has_bias, cached):
    cache_ref = refs[-1] if cached else None
    refs = refs[:-1] if cached else refs
    if has_bias:
        a_ref, w_ref, b_ref, o_ref = refs
    else:
        a_ref, w_ref, o_ref = refs
    a = a_ref[...]
    if act == "silu":
        a = a.astype(F32)
        a = a * jax.nn.sigmoid(a)
    y = _dot(a.astype(BF16), _bf16_weight(w_ref, cache_ref, pl.program_id(1) == 0))
    if has_bias:
        y = y + b_ref[...]
    o_ref[...] = y.astype(o_ref.dtype)


def _layered(w):
    return w if isinstance(w, tuple) else (w[None], 0)


def mm_plain(a, w, bias=None, act=None, tm=512, tn=None, out_dtype=F32):
    w, layer = _layered(w)
    m, k = a.shape
    n = w.shape[2]
    tm = min(tm, m)
    tn = tn or n
    assert m % tm == 0 and n % tn == 0
    in_specs = [pl.BlockSpec((tm, k), lambda j, i: (i, 0)), pl.BlockSpec((None, k, tn), lambda j, i: (layer, 0, j))]
    args = [a, w]
    if bias is not None:
        in_specs.append(pl.BlockSpec((1, tn), lambda j, i: (0, j)))
        args.append(bias.reshape(1, n))
    cache = _weight_cache(w, (k, tn), m // tm)
    return pl.pallas_call(
        functools.partial(_mm_body, act=act, has_bias=bias is not None, cached=bool(cache)),
        grid=(n // tn, m // tm),
        in_specs=in_specs,
        out_specs=pl.BlockSpec((tm, tn), lambda j, i: (i, j)),
        out_shape=jax.ShapeDtypeStruct((m, n), out_dtype),
        scratch_shapes=cache,
        compiler_params=_cparams("arbitrary", "arbitrary"),
        name="mm_plain",
    )(*args)


def _kv_padded_body(a_ref, w_ref, kv_ref, win_ref, kvb_ref, cache_ref):
    b, t = pl.program_id(0), pl.program_id(1)
    w = _bf16_weight(w_ref, cache_ref, (b == 0) & (t == 0))

    @pl.when(t == 0)
    def _():
        kvb_ref[...] = jnp.zeros_like(kvb_ref)

    @pl.when(t > 0)
    def _():
        y = _dot(a_ref[...], w)
        n_rows = kv_ref.shape[1]
        kv_ref[...] = y[:, :n_rows]
        win_ref[...] = y[:, n_rows:]
        kvb_ref[...] = y.astype(kvb_ref.dtype)


def mm_kv_padded(a, w, bn, s, pad, n_rows):
    m, k = a.shape
    n = w.shape[1]
    tm = pad
    assert s % tm == 0
    tps = s // tm
    src = lambda b, t: (b * tps + jnp.maximum(t - 1, 0), 0)
    return pl.pallas_call(
        _kv_padded_body,
        grid=(bn, tps + 1),
        in_specs=[pl.BlockSpec((tm, k), src),
                  pl.BlockSpec((k, n), lambda b, t: (0, 0), pipeline_mode=pl.Buffered(1))],
        out_specs=[pl.BlockSpec((tm, n_rows), src), pl.BlockSpec((tm, n - n_rows), src),
                   pl.BlockSpec((None, tm, n), lambda b, t: (b, t, 0))],
        out_shape=[jax.ShapeDtypeStruct((m, n_rows), F32), jax.ShapeDtypeStruct((m, n - n_rows), F32),
                   jax.ShapeDtypeStruct((bn, pad + s, n), BF16)],
        scratch_shapes=[pltpu.VMEM((k, n), BF16)],
        compiler_params=_cparams("arbitrary", "arbitrary"),
        name="mm_kv_padded",
    )(a, w)


def _swiglu_body(h_ref, wg_ref, wu_ref, o_ref, *caches):
    first = pl.program_id(1) == 0
    wg = _bf16_weight(wg_ref, caches[0] if caches else None, first)
    wu = _bf16_weight(wu_ref, caches[1] if caches else None, first)
    tm = h_ref.shape[0]
    n_part = tm // RESNORM_PART if tm % RESNORM_PART == 0 else 1
    rows = tm // n_part
    gu = [(_dot(h_ref[i * rows:(i + 1) * rows, :], wg), _dot(h_ref[i * rows:(i + 1) * rows, :], wu))
          for i in range(n_part)]
    for i, (g, u) in enumerate(gu):
        o_ref[i * rows:(i + 1) * rows, :] = (g * jax.nn.sigmoid(g) * u).astype(o_ref.dtype)


def mm_swiglu(h, w_in, tm=2 * RESNORM_PART, tn=1408):
    w_in, layer = _layered(w_in)
    m, k = h.shape
    nf = w_in.shape[2] // 2
    tm = min(tm, m)
    assert nf % tn == 0 and m % tm == 0
    nj = nf // tn
    return pl.pallas_call(
        _swiglu_body,
        grid=(nj, m // tm),
        in_specs=[pl.BlockSpec((tm, k), lambda j, i: (i, 0)),
                  pl.BlockSpec((None, k, tn), lambda j, i: (layer, 0, j), pipeline_mode=pl.Buffered(1)),
                  pl.BlockSpec((None, k, tn), lambda j, i: (layer, 0, j + nj), pipeline_mode=pl.Buffered(1))],
        out_specs=pl.BlockSpec((tm, tn), lambda j, i: (i, j)),
        out_shape=jax.ShapeDtypeStruct((m, nf), BF16),
        scratch_shapes=_weight_cache(w_in, (k, tn), m // tm) * 2,
        compiler_params=_cparams("arbitrary", "arbitrary"),
        name="mm_swiglu",
    )(h, w_in, w_in)


def _normmod_body(x_ref, g_ref, sh_ref, sc_ref, o_ref, *, bb, ts):
    y = _rms(x_ref[...], g_ref[...])
    o_ref[...] = (y * (1.0 + _rows(sc_ref, bb, ts)) + _rows(sh_ref, bb, ts)).astype(o_ref.dtype)


def normmod(x, gain, ada, col_shift, col_scale, bn, s):
    m = x.shape[0]
    tm, bb, ts, tps = _row_tiling(bn, s)
    return pl.pallas_call(
        functools.partial(_normmod_body, bb=bb, ts=ts),
        grid=(m // tm,),
        in_specs=[pl.BlockSpec((tm, D_MODEL), lambda i: (i, 0)),
                  pl.BlockSpec((1, D_MODEL), lambda i: (0, 0)),
                  _vec_spec(bb, tps, col_shift, 1),
                  _vec_spec(bb, tps, col_scale, 1)],
        out_specs=pl.BlockSpec((tm, D_MODEL), lambda i: (i, 0)),
        out_shape=jax.ShapeDtypeStruct((m, D_MODEL), BF16),
        compiler_params=_cparams("arbitrary"),
        name="normmod",
    )(x, gain.reshape(1, D_MODEL), ada, ada)


def _resnorm_body(*refs, bb, ts, n_heads, cached):
    cache_ref = refs[-1] if cached else None
    refs = refs[:-1] if cached else refs
    a_ref, w_ref, x_ref, gate_ref, ngain_ref = refs[:5]
    head_refs = refs[5:5 + 3 * n_heads]
    xo_ref = refs[5 + 3 * n_heads]
    ho_refs = refs[6 + 3 * n_heads:]
    w = _bf16_weight(w_ref, cache_ref, pl.program_id(0) == 0)
    tm = a_ref.shape[0]
    n_part = tm // RESNORM_PART if tm % RESNORM_PART == 0 else 1
    rows = tm // n_part
    ys = [_dot(a_ref[i * rows:(i + 1) * rows, :].astype(BF16), w) for i in range(n_part)]
    if n_part > 1:
        assert bb == 1
        part_rows = lambda ref: _rows(ref, 1, rows)
    else:
        part_rows = lambda ref: _rows(ref, bb, ts)
    gate = part_rows(gate_ref)
    mods = [(head_refs[3 * k][...], part_rows(head_refs[3 * k + 1]), part_rows(head_refs[3 * k + 2]))
            for k in range(n_heads)]
    for i, y in enumerate(ys):
        sl = slice(i * rows, (i + 1) * rows)
        x = x_ref[sl, :] + gate * _rms(y, ngain_ref[...])
        xo_ref[sl, :] = x
        for k, (hg, sh, sc) in enumerate(mods):
            ho_refs[k][sl, :] = (_rms(x, hg) * (1.0 + sc) + sh).astype(ho_refs[k].dtype)


def mm_resnorm(a, w, x, gate, ngain, heads, bn, s):
    w, layer = _layered(w)
    m, k = a.shape
    tile_bytes = 2 * RESNORM_PART * (2 * k * a.dtype.itemsize + 4 * D_MODEL * 4 + 2 * len(heads) * D_MODEL * 2)
    weight_bytes = k * D_MODEL * (w.dtype.itemsize + 2)
    two_parts = s % (2 * RESNORM_PART) == 0 and tile_bytes + weight_bytes <= (3 * VMEM_LIMIT) // 4
    tm, bb, ts, tps = _row_tiling(bn, s, tm=2 * RESNORM_PART if two_parts else RESNORM_PART)
    row = lambda i: (i, 0)
    const = lambda i: (0, 0)
    in_specs = [pl.BlockSpec((tm, k), row),
                pl.BlockSpec((None, k, D_MODEL), lambda i: (layer, 0, 0), pipeline_mode=pl.Buffered(1)),
                pl.BlockSpec((tm, D_MODEL), row), _vec_spec(bb, tps, gate[1], 1),
                pl.BlockSpec((1, D_MODEL), const)]
    args = [a, w, x, gate[0], ngain.reshape(1, D_MODEL)]
    for hg, sh_arr, sh_col, sc_arr, sc_col in heads:
        in_specs += [pl.BlockSpec((1, D_MODEL), const), _vec_spec(bb, tps, sh_col, 1), _vec_spec(bb, tps, sc_col, 1)]
        args += [hg.reshape(1, D_MODEL), sh_arr, sc_arr]
    out_shape = [jax.ShapeDtypeStruct((m, D_MODEL), F32)] + [jax.ShapeDtypeStruct((m, D_MODEL), BF16)] * len(heads)
    out_specs = [pl.BlockSpec((tm, D_MODEL), row)] * (1 + len(heads))
    cache = _weight_cache(w, (k, D_MODEL), m // tm)
    outs = pl.pallas_call(
        functools.partial(_resnorm_body, bb=bb, ts=ts, n_heads=len(heads), cached=bool(cache)),
        grid=(m // tm,),
        in_specs=in_specs,
        out_specs=out_specs,
        out_shape=out_shape,
        scratch_shapes=cache,
        compiler_params=_cparams("arbitrary"),
        name="mm_resnorm",
    )(*args)
    return outs[0], list(outs[1:])


def _rglru_body(u_ref, conv8_ref, hprev_ref, cw_ref, cb_ref, wa_ref, ba_ref, wx_ref, bx_ref, lam_ref,
                y_ref, convo_ref, ho_ref, carry_h, carry_conv, *, bb, tt):
    j = pl.program_id(1)

    @pl.when(j == 0)
    def _():
        carry_h[...] = hprev_ref[...]
        carry_conv[...] = conv8_ref[...]

    u = u_ref[...]
    xb = u[:, :, :D_RNN]
    yb = u[:, :, D_RNN:]
    ext = jnp.concatenate([carry_conv[...], xb], axis=1)
    cw = cw_ref[...]
    xc = cb_ref[...] + ext[:, 5:5 + tt] * cw[0:1]
    for k in range(1, CONV_W):
        xc = xc + ext[:, 5 + k:5 + k + tt] * cw[k:k + 1]
    last8 = xb[:, tt - 8:tt]
    carry_conv[...] = last8
    convo_ref[...] = last8

    xc2 = xc.reshape(bb * tt, D_RNN)
    xcb = xc2.astype(BF16)

    def gate(w_ref, b_ref):
        cols = [_dot(xcb[:, n * LRU_BW:(n + 1) * LRU_BW], w_ref[n]) for n in range(LRU_BLOCKS)]
        return jax.nn.sigmoid(jnp.concatenate(cols, axis=1) + b_ref[...])

    r = gate(wa_ref, ba_ref)
    ig = gate(wx_ref, bx_ref)
    nl = -lam_ref[...]
    z = jnp.exp(-jnp.abs(nl))
    w1 = 1.0 + z
    log1p_z = jnp.where(w1 == 1.0, z, jnp.log(w1) * (z / jnp.where(w1 == 1.0, 1.0, w1 - 1.0)))
    softplus = jnp.maximum(nl, 0.0) + log1p_z
    log_a = -LRU_C * r * softplus
    a = jnp.exp(log_a)
    b = jnp.sqrt(1.0 - a * a) * (ig * xc2)

    a = a.reshape(bb * tt // 8, 8, D_RNN)
    b = b.reshape(bb * tt // 8, 8, D_RNN)
    row8 = lax.broadcasted_iota(jnp.int32, a.shape, 1)
    for d in (1, 2, 4):
        a_s = pltpu.roll(a, d, 1)
        b_s = pltpu.roll(b, d, 1)
        keep = row8 >= d
        b = jnp.where(keep, a * b_s + b, b)
        a = jnp.where(keep, a * a_s, a)

    a3 = a.reshape(bb, tt, D_RNN)
    b3 = b.reshape(bb, tt, D_RNN)
    carry = carry_h[...]
    groups = []
    for g in range(tt // 8):
        hg = a3[:, 8 * g:8 * g + 8] * carry + b3[:, 8 * g:8 * g + 8]
        carry = hg[:, 7:8]
        groups.append(hg)
    hs = groups[0] if len(groups) == 1 else jnp.concatenate(groups, axis=1)
    carry_h[...] = carry
    ho_ref[...] = carry
    y_ref[...] = (hs * _gelu_tanh(yb)).astype(y_ref.dtype)


def rglru(u, conv8, hprev, cw, cb, wa, ba, wx, bx, lam):
    bn, s, _ = u.shape
    if s >= 256:
        bb, tt = 1, 256
    else:
        bb, tt = min(bn, 512 // s), s
    assert s % tt == 0 and bn % bb == 0 and tt % 8 == 0
    vec = lambda a: a.reshape(1, D_RNN)
    c2 = lambda i, j: (0, 0)
    c3 = lambda i, j: (0, 0, 0)
    return pl.pallas_call(
        functools.partial(_rglru_body, bb=bb, tt=tt),
        grid=(bn // bb, s // tt),
        in_specs=[pl.BlockSpec((bb, tt, 2 * D_RNN), lambda i, j: (i, j, 0)),
                  pl.BlockSpec((bb, 8, D_RNN), lambda i, j: (i, 0, 0)),
                  pl.BlockSpec((bb, 1, D_RNN), lambda i, j: (i, 0, 0)),
                  pl.BlockSpec((CONV_W, D_RNN), c2),
                  pl.BlockSpec((1, D_RNN), c2),
                  pl.BlockSpec((LRU_BLOCKS, LRU_BW, LRU_BW), c3),
                  pl.BlockSpec((1, D_RNN), c2),
                  pl.BlockSpec((LRU_BLOCKS, LRU_BW, LRU_BW), c3),
                  pl.BlockSpec((1, D_RNN), c2),
                  pl.BlockSpec((1, D_RNN), c2)],
        out_specs=[pl.BlockSpec((bb, tt, D_RNN), lambda i, j: (i, j, 0)),
                   pl.BlockSpec((bb, 8, D_RNN), lambda i, j: (i, 0, 0)),
                   pl.BlockSpec((bb, 1, D_RNN), lambda i, j: (i, 0, 0))],
        out_shape=[jax.ShapeDtypeStruct((bn, s, D_RNN), BF16),
                   jax.ShapeDtypeStruct((bn, 8, D_RNN), F32),
                   jax.ShapeDtypeStruct((bn, 1, D_RNN), F32)],
        scratch_shapes=[pltpu.VMEM((bb, 1, D_RNN), F32), pltpu.VMEM((bb, 8, D_RNN), F32)],
        compiler_params=_cparams("arbitrary", "arbitrary"),
        name="rglru",
    )(u, conv8, hprev, cw, vec(cb), wa, vec(ba), wx, vec(bx), vec(lam))


def _bias_body(tb_ref, d_ref, o_ref):
    h = pl.program_id(0)
    d = d_ref[...]
    dc = jnp.maximum(d, 0)
    max_exact = N_BUCKETS // 2
    scaled = jnp.log(jnp.maximum(dc, 1).astype(F32) / max_exact) / math.log(MAX_DIST / max_exact)
    large = jnp.minimum(max_exact + (scaled * (N_BUCKETS - max_exact)).astype(jnp.int32), N_BUCKETS - 1)
    bucket = jnp.where(dc < max_exact, dc, large)
    acc = jnp.full(d.shape, tb_ref[0, h], F32)
    for k in range(1, N_BUCKETS):
        acc = jnp.where(bucket == k, tb_ref[k, h], acc)
    o_ref[...] = jnp.where(d < 0, NEG, acc)


def rel_bias_tiles(table, dist):
    rows, cols = dist.shape
    return pl.pallas_call(
        _bias_body,
        grid=(N_HEADS,),
        in_specs=[pl.BlockSpec(memory_space=pltpu.SMEM), pl.BlockSpec((rows, cols), lambda h: (0, 0))],
        out_specs=pl.BlockSpec((None, rows, cols), lambda h: (h, 0, 0)),
        out_shape=jax.ShapeDtypeStruct((N_HEADS, rows, cols), F32),
        compiler_params=_cparams("arbitrary"),
        name="rel_bias_tiles",
    )(table, dist)


def _reorder_emit(sources, stage, o_ref):
    n_col = stage.shape[0]
    row = 0
    for src in sources:
        n = src.shape[0]
        for c in range(n_col):
            stage[c, row:row + n, :] = src[:, c * LANES:(c + 1) * LANES]
        row += n
    n16 = row // CMP_STRIDE
    for r in range(CMP_STRIDE):
        x = jnp.concatenate([stage[c, pl.ds(r, n16, stride=CMP_STRIDE), :] for c in range(n_col)], axis=1)
        o_ref[r] = x.astype(o_ref.dtype)


def _reorder_dense_body(x_ref, o_ref, stage):
    _reorder_emit([x_ref], stage, o_ref)


def _reorder_paged_body(pt_ref, *refs, n_pages):
    del pt_ref
    _reorder_emit(refs[:n_pages], refs[n_pages + 1], refs[n_pages])


def reorder_dense(kv):
    bn, s, _ = kv.shape
    n16 = s // CMP_STRIDE
    return pl.pallas_call(
        _reorder_dense_body,
        grid=(bn,),
        in_specs=[pl.BlockSpec((None, s, 512), lambda b: (b, 0, 0))],
        out_specs=pl.BlockSpec((None, CMP_STRIDE, n16, 512), lambda b: (b, 0, 0, 0)),
        out_shape=jax.ShapeDtypeStruct((bn, CMP_STRIDE, n16, 512), BF16),
        scratch_shapes=[pltpu.VMEM((512 // LANES, s, LANES), F32)],
        compiler_params=_cparams("arbitrary"),
        name="reorder_dense",
    )(kv)


def reorder_paged(cache, page_table):
    bn, n_pages = page_table.shape
    page = cache.shape[1]
    n16 = n_pages * page // CMP_STRIDE
    page_specs = [pl.BlockSpec((None, page, 512), functools.partial(lambda b, pt, k: (pt[b * n_pages + k], 0, 0), k=k))
                  for k in range(n_pages)]
    return pl.pallas_call(
        functools.partial(_reorder_paged_body, n_pages=n_pages),
        grid_spec=pltpu.PrefetchScalarGridSpec(
            num_scalar_prefetch=1, grid=(bn,),
            in_specs=page_specs,
            out_specs=pl.BlockSpec((None, CMP_STRIDE, n16, 512), lambda b, pt: (b, 0, 0, 0)),
            scratch_shapes=[pltpu.VMEM((512 // LANES, n_pages * page, LANES), F32)]),
        out_shape=jax.ShapeDtypeStruct((bn, CMP_STRIDE, n16, 512), BF16),
        compiler_params=_cparams("arbitrary"),
        name="reorder_paged",
    )(page_table.reshape(-1), *([cache] * n_pages))


def _compress_body(x_ref, w1_ref, c0_ref, w2_ref, o_ref, acc, *, rows):
    r = pl.program_id(2)

    @pl.when(r == 0)
    def _():
        acc[...] = jnp.zeros_like(acc)

    a = acc[...]
    for k in range(CMP_RSTEP):
        a = a + _dot(x_ref[:, k].reshape(rows, 256), w1_ref[k])
    acc[...] = a

    @pl.when(r == CMP_STRIDE // CMP_RSTEP - 1)
    def _():
        half = acc.shape[1] // 2
        hidden = _gelu_tanh(acc[:, :half] + pltpu.roll(acc[:, half:], rows - 1, 0) + c0_ref[...])
        o_ref[...] = _dot(hidden.astype(BF16), w2_ref[...]).reshape(o_ref.shape).astype(o_ref.dtype)


def compress(x, w1cat, c0, w2bd, sb):
    bn, _, n16, _ = x.shape
    assert bn % sb == 0
    rows = sb * n16
    nh = w1cat.shape[-1]
    return pl.pallas_call(
        functools.partial(_compress_body, rows=rows),
        grid=(bn // sb, 2, CMP_STRIDE // CMP_RSTEP),
        in_specs=[pl.BlockSpec((sb, CMP_RSTEP, n16, 256), lambda g, s, r: (g, r, 0, s)),
                  pl.BlockSpec((None, CMP_RSTEP, 256, nh), lambda g, s, r: (s, r, 0, 0)),
                  pl.BlockSpec((None, 1, nh // 2), lambda g, s, r: (s, 0, 0)),
                  pl.BlockSpec((None, nh // 2, 256), lambda g, s, r: (s, 0, 0))],
        out_specs=pl.BlockSpec((sb, n16, 256), lambda g, s, r: (g, 0, s)),
        out_shape=jax.ShapeDtypeStruct((bn, n16, 512), BF16),
        scratch_shapes=[pltpu.VMEM((rows, nh), F32)],
        compiler_params=_cparams("arbitrary", "arbitrary", "arbitrary"),
        name="compress",
    )(x, w1cat, c0, w2bd)


def _assemble_body(pt_ref, *refs, n_pages, page, s_new, win_len):
    del pt_ref
    pages = refs[:n_pages]
    new_slc_ref, new_win_ref, cwin_ref, slc_ref, win_ref = refs[n_pages:]
    for k in range(n_pages):
        slc_ref[k * page:(k + 1) * page, :] = pages[k][...].astype(slc_ref.dtype)
    pad = jnp.zeros((SLC_LEN - s_new, 512), F32)
    slc_ref[n_pages * page:n_pages * page + SLC_LEN, :] = jnp.concatenate(
        [new_slc_ref[...], pad], axis=0).astype(slc_ref.dtype)
    win_ref[0:win_len, :] = cwin_ref[...].astype(win_ref.dtype)
    win_ref[win_len:win_len + SLC_LEN, :] = jnp.concatenate([new_win_ref[...], pad], axis=0).astype(win_ref.dtype)


def assemble_sample_kv(cache, page_table, kv_new, cache_win):
    bn, n_pages = page_table.shape
    page = cache.shape[1]
    s_new = kv_new.shape[1]
    win_len = cache_win.shape[1]
    past = n_pages * page
    page_specs = [pl.BlockSpec((None, page, 512), functools.partial(lambda b, pt, k: (pt[b * n_pages + k], 0, 1), k=k))
                  for k in range(n_pages)]
    return pl.pallas_call(
        functools.partial(_assemble_body, n_pages=n_pages, page=page, s_new=s_new, win_len=win_len),
        grid_spec=pltpu.PrefetchScalarGridSpec(
            num_scalar_prefetch=1, grid=(bn,),
            in_specs=page_specs + [pl.BlockSpec((None, s_new, 512), lambda b, pt: (b, 0, 1)),
                                   pl.BlockSpec((None, s_new, 512), lambda b, pt: (b, 0, 2)),
                                   pl.BlockSpec((None, win_len, 512), lambda b, pt: (b, 0, 0))],
            out_specs=[pl.BlockSpec((None, past + SLC_LEN, 512), lambda b, pt: (b, 0, 0)),
                       pl.BlockSpec((None, win_len + SLC_LEN, 512), lambda b, pt: (b, 0, 0))]),
        out_shape=[jax.ShapeDtypeStruct((bn, past + SLC_LEN, 512), BF16),
                   jax.ShapeDtypeStruct((bn, win_len + SLC_LEN, 512), BF16)],
        compiler_params=_cparams("arbitrary"),
        name="assemble_sample_kv",
    )(page_table.reshape(-1), *([cache] * n_pages), kv_new, kv_new, cache_win)


def _value_heads(vv):
    own_a = lax.broadcasted_iota(jnp.int32, vv.shape, 1) < HEAD_DIM
    return jnp.where(own_a, vv, 1.0), jnp.where(own_a, 1.0, vv)


def _softmax_update(state, s, vv, rr):
    smax = jnp.max(s, axis=-1, keepdims=True)
    mn = smax if state is None else jnp.maximum(state[0], smax)
    e = jnp.exp2(s - mn).astype(BF16)
    va, vb = _value_heads(vv)
    pv = jnp.concatenate([_dot(e[:rr], va), _dot(e[rr:], vb)], axis=0)
    return mn, (pv if state is None else jnp.exp2(state[0] - mn) * state[1] + pv)


def _softmax_finish(state):
    _, acc = state
    return acc / pltpu.roll(acc, HEAD_DIM, 1)


def _attn_seq(qi, q_ref, gl_ref, kc_ref, vc_ref, ks_ref, vs_ref, kw_ref, vw_ref, augk_ref, augw_ref,
              ovt_ref, cb_ref, nb_ref, wb_ref, augc_ref, ex_ref, o_ref,
              *, qb, n_slc, top_n, pad_s, t0, multi_block, n_streams):
    rr = GQA * qb
    r2 = 2 * rr
    pairs = range(N_PAIRS)
    is_a = lax.broadcasted_iota(jnp.int32, (rr, LANES), 1) < HEAD_DIM
    if multi_block:
        cur = t0 // SLC_LEN + qi
        win_start = pl.multiple_of(qi * qb, SLC_LEN)
    else:
        cur = t0 // SLC_LEN
        win_start = 0

    def cols(ref, p, start, size):
        return ref[pl.ds(start, size), p * LANES:(p + 1) * LANES]

    def init():
        return jnp.full((r2, 1), -jnp.inf, F32), jnp.zeros((r2, LANES), F32)

    wlen = WINDOW + SLC_LEN
    augw = augw_ref[pl.ds(win_start, wlen), :]
    glt = jnp.concatenate([gl_ref[...]] * GQA, axis=0)
    slab = lax.broadcasted_iota(jnp.int32, (rr, LANES), 0) // qb
    gsrc = jnp.concatenate([jnp.where(slab == g, glt, 0.0) for g in range(GQA)], axis=1)
    ghi = gsrc.astype(BF16)
    glo = (gsrc - ghi.astype(F32)).astype(BF16)

    q2, s_win, s_cmp, o_win, gates, o_cmp, imp_t = [], [], [], [], [], [], []
    for p in pairs:
        qfull = q_ref[:, p * GQA * LANES:(p + 1) * GQA * LANES]
        qs = jnp.concatenate([qfull[:, g * LANES:(g + 1) * LANES] for g in range(GQA)], axis=0)
        qs = qs * (HEAD_DIM ** -0.5 * LOG2E)
        q2.append(jnp.concatenate([jnp.where(is_a, qs, 0.0), jnp.where(is_a, 0.0, qs)], axis=0).astype(BF16))
        s_cmp.append(_dot_nt(q2[p], kc_ref[:, p * LANES:(p + 1) * LANES]) + cb_ref[p])
    for p in pairs:
        lhs_win = jnp.concatenate([q2[p], augc_ref[p, 1].astype(BF16)], axis=1)
        kk = jnp.concatenate([cols(kw_ref, p, win_start, wlen), augw], axis=1)
        s_win.append(_dot_nt(lhs_win, kk) + wb_ref[p])

    for p in pairs:
        s = s_cmp[p]
        m = jnp.max(s, axis=-1, keepdims=True)
        e = jnp.exp2(s - m)
        pr = e / jnp.sum(e, axis=-1, keepdims=True)
        pr = jnp.where(m > 0.1 * NEG, pr, 0.0)
        o_cmp.append(_dot(pr.astype(BF16), vc_ref[:, p * LANES:(p + 1) * LANES]))

        pcat = jnp.concatenate(
            [jnp.concatenate([pr[x * rr + g * qb:x * rr + (g + 1) * qb] for g in range(GQA)], axis=1)
             for x in range(2)], axis=0)
        if 2 * qb < LANES:
            pcat = jnp.concatenate([pcat, jnp.zeros((LANES - 2 * qb, pcat.shape[1]), F32)], axis=0)
        imp_t.append(_dot_nt(ovt_ref[...], pcat.astype(BF16)))

    def ranked():
        jr = lax.broadcasted_iota(jnp.int32, (N_BLOCK_LANES, LANES), 0)
        forced = (jr == 0) | (jr == cur) | (jr == cur - 1)
        sub8 = lax.broadcasted_iota(jnp.int32, (8, LANES), 0)
        res = []
        for p in pairs:
            v = jnp.where(forced, FORCE, jnp.where((jr > cur) | (jr >= n_slc), NEG, imp_t[p][:N_BLOCK_LANES]))
            groups = [v[8 * r:8 * r + 8] for r in range(N_BLOCK_LANES // 8)]
            ranks = [jnp.zeros((8, LANES), F32) for _ in groups]
            for k in range(n_slc):
                vk = v[k:k + 1, :]
                for r, vr in enumerate(groups):
                    if 8 * r > k:
                        ranks[r] = ranks[r] + jnp.where(vk >= vr, 1.0, 0.0)
                    elif 8 * r + 7 < k:
                        ranks[r] = ranks[r] + jnp.where(vk > vr, 1.0, 0.0)
                    else:
                        ranks[r] = ranks[r] + jnp.where(sub8 > k - 8 * r, jnp.where(vk >= vr, 1.0, 0.0),
                                                        jnp.where(vk > vr, 1.0, 0.0))
            res.append(jnp.where(jnp.concatenate(ranks, axis=0) < top_n, 0.0, 1.0))
        return tuple(res)

    notsel_t = ranked()

    for p in pairs:
        o_win.append(_softmax_finish(_softmax_update(None, s_win[p], cols(vw_ref, p, win_start, wlen), rr)))
        gates.append(jax.nn.sigmoid(_dot(ghi, ex_ref[p]) + _dot(glo, ex_ref[p])))

    st = pad_s + (cur - 2) * SLC_LEN
    if multi_block:
        st = pl.multiple_of(st, SLC_LEN)
    augk_near = augk_ref[pl.ds(st, 3 * SLC_LEN), :]
    lane2 = lax.broadcasted_iota(jnp.int32, (r2, LANES), 1)
    lhs_far, s_near = [], []
    for p in pairs:
        notsel = jnp.concatenate([notsel_t[p], jnp.zeros((LANES - N_BLOCK_LANES, LANES), F32)], axis=0).T
        ns = jnp.concatenate([notsel[x * qb:(x + 1) * qb] for x in range(2) for _ in range(GQA)], axis=0)
        ns_far = jnp.where((lane2 >= cur - 2) & (lane2 < N_BLOCK_LANES), 1.0, ns)
        lhs_far.append(jnp.concatenate([q2[p], (ns_far + augc_ref[p, 0]).astype(BF16)], axis=1))
        lhs_near = jnp.concatenate([q2[p], (ns + augc_ref[p, 1]).astype(BF16)], axis=1)
        kk = jnp.concatenate([cols(ks_ref, p, st, 3 * SLC_LEN), augk_near], axis=1)
        s_near.append(_dot_nt(lhs_near, kk) + nb_ref[p])
    near = [_softmax_update(None, s_near[p], cols(vs_ref, p, st, 3 * SLC_LEN), rr) for p in pairs]

    def far_start(it, k):
        return pl.multiple_of(pad_s + (it * n_streams + k) * FAR_CHUNK, math.gcd(pad_s, FAR_CHUNK))

    def far_scores(it):
        scores = []
        for k in range(n_streams):
            st = far_start(it, k)
            augk = augk_ref[pl.ds(st, FAR_CHUNK), :]
            for p in pairs:
                kk = jnp.concatenate([cols(ks_ref, p, st, FAR_CHUNK), augk], axis=1)
                scores.append(_dot_nt(lhs_far[p], kk))
        return tuple(scores)

    def far_update(it, states, scores):
        return tuple(_softmax_update(states[k * N_PAIRS + p], scores[k * N_PAIRS + p],
                                     cols(vs_ref, p, far_start(it, k), FAR_CHUNK), rr)
                     for k in range(n_streams) for p in pairs)

    per_iter = n_streams * (FAR_CHUNK // SLC_LEN)
    if multi_block:
        n_iter = (jnp.maximum(cur - 2, 0) + per_iter - 1) // per_iter
        states = tuple(near) + tuple(init() for _ in range((n_streams - 1) * N_PAIRS))
        states = lax.fori_loop(0, n_iter, lambda it, states: far_update(it, states, far_scores(it)), states)
    else:
        states = tuple(near) + (None,) * ((n_streams - 1) * N_PAIRS)
        for it in range((max(cur - 2, 0) + per_iter - 1) // per_iter):
            states = far_update(it, states, far_scores(it))

    outs = []
    for p in pairs:
        mine = tuple(st for st in states[p::N_PAIRS] if st is not None)
        if len(mine) == 1:
            o_slc = _softmax_finish(mine[0])
        else:
            m_all = functools.reduce(jnp.maximum, [m for m, _ in mine])
            o_slc = _softmax_finish((m_all, sum(acc * jnp.exp2(m - m_all) for m, acc in mine)))
        out = None
        for br, o2 in enumerate((o_cmp[p], o_slc, o_win[p])):
            term = gates[p][:, br * LANES:(br + 1) * LANES] * jnp.where(is_a, o2[:rr], o2[rr:])
            out = term if out is None else out + term
        outs += [out[g * qb:(g + 1) * qb] for g in range(GQA)]
    o_ref[...] = jnp.concatenate(outs, axis=1).astype(o_ref.dtype)


def _attn_body(*refs, sbq, **kw):
    qi = pl.program_id(1)
    per_seq, shared = refs[:8], refs[8:16]
    for i in range(sbq):
        _attn_seq(qi, *[r.at[i] for r in per_seq], *shared, refs[16].at[i], **kw)


def _cmp_bias_body(cbe_ref, o_ref, *, nc, step):
    tau = pl.program_id(0) * step
    for p in range(N_PAIRS):
        o_ref[p] = pltpu.roll(cbe_ref[p], tau, 1)[:, nc:]


def cmp_bias_per_block(cbe, nqb, step):
    n_p, rows, nc2 = cbe.shape
    nc = nc2 // 2
    return pl.pallas_call(
        functools.partial(_cmp_bias_body, nc=nc, step=step),
        grid=(nqb,),
        in_specs=[pl.BlockSpec(cbe.shape, lambda i: (0, 0, 0))],
        out_specs=pl.BlockSpec((None, n_p, rows, nc), lambda i: (i, 0, 0, 0)),
        out_shape=jax.ShapeDtypeStruct((nqb, n_p, rows, nc), F32),
        compiler_params=_cparams("arbitrary"),
        name="cmp_bias_per_block",
    )(cbe)


def nsa_attention(u, cmp_kv, slc_kv, slc_cols, win_kv, win_cols, augk, augw, ovt, tiles, ex,
                  *, qb, n_slc, pad_s, t0, sbq):
    bn, s, _ = u.shape
    nqb = s // qb
    nc = cmp_kv.shape[1]
    r2 = 2 * GQA * qb
    cb, nb, wb, augc = tiles
    srows = slc_kv.shape[1]
    wrows = win_kv.shape[1]
    kcol, vcol = slc_cols
    kwcol, vwcol = win_cols
    assert qb == SLC_LEN or nqb == 1
    assert t0 % SLC_LEN == 0 and bn % sbq == 0
    per_chunk = FAR_CHUNK // SLC_LEN
    max_chunks = -(-max(n_slc - 3, 0) // per_chunk)
    n_streams = min(1, max_chunks) if nqb > 1 else max_chunks
    n_streams = max(n_streams, 1)
    assert pad_s + -(-max_chunks // n_streams) * n_streams * FAR_CHUNK <= srows
    body = functools.partial(_attn_body, sbq=sbq, qb=qb, n_slc=n_slc, top_n=min(N_SELECT, n_slc),
                             pad_s=pad_s, t0=t0, multi_block=nqb > 1, n_streams=n_streams)
    whole = lambda a: pl.BlockSpec(a.shape, lambda b, i: (0,) * a.ndim)
    kvw = N_PAIRS * LANES
    nq = N_HEADS * HEAD_DIM
    return pl.pallas_call(
        body,
        grid=(bn // sbq, nqb),
        in_specs=[pl.BlockSpec((sbq, qb, nq), lambda b, i: (b, i, 0)),
                  pl.BlockSpec((sbq, qb, LANES), lambda b, i: (b, i, nq // LANES)),
                  pl.BlockSpec((sbq, nc, kvw), lambda b, i: (b, 0, 0)),
                  pl.BlockSpec((sbq, nc, kvw), lambda b, i: (b, 0, 1)),
                  pl.BlockSpec((sbq, srows, kvw), lambda b, i: (b, 0, kcol)),
                  pl.BlockSpec((sbq, srows, kvw), lambda b, i: (b, 0, vcol)),
                  pl.BlockSpec((sbq, wrows, kvw), lambda b, i: (b, 0, kwcol)),
                  pl.BlockSpec((sbq, wrows, kvw), lambda b, i: (b, 0, vwcol)),
                  whole(augk), whole(augw), whole(ovt),
                  pl.BlockSpec((None, N_PAIRS, r2, nc), lambda b, i: (i, 0, 0, 0)),
                  whole(nb), whole(wb), whole(augc), whole(ex)],
        out_specs=pl.BlockSpec((sbq, qb, nq), lambda b, i: (b, i, 0)),
        out_shape=jax.ShapeDtypeStruct((bn, s, nq), F32),
        compiler_params=_cparams("arbitrary", "arbitrary"),
        name="nsa_attention",
    )(u, u, cmp_kv, cmp_kv, slc_kv, slc_kv, win_kv, win_kv, augk, augw, ovt, cb, nb, wb, augc, ex)


def _head_perm():
    idx = np.empty((N_PAIRS, GQA, 2, HEAD_DIM), np.int32)
    for p in range(N_PAIRS):
        for g in range(GQA):
            for half in range(2):
                head = (2 * p + half) * GQA + g
                idx[p, g, half] = head * HEAD_DIM + np.arange(HEAD_DIM)
    return idx.reshape(-1)


def _gate_expand():
    ex = np.zeros((N_PAIRS, GQA * LANES, 3 * LANES), np.float32)
    for p in range(N_PAIRS):
        for g in range(GQA):
            for half in range(2):
                head = (2 * p + half) * GQA + g
                for br in range(3):
                    ex[p, g * LANES + head * 3 + br, br * LANES + half * HEAD_DIM:br * LANES + (half + 1) * HEAD_DIM] = 1.0
    return ex


def _overlap_t(nc, n_cmp, n_slc):
    i = np.arange(nc)[None, :] * CMP_STRIDE
    j = np.arange(LANES)[:, None] * SLC_LEN
    ov = (i < j + SLC_LEN) & (i + CMP_LEN > j) & (np.arange(nc)[None, :] < n_cmp) & (np.arange(LANES)[:, None] < n_slc)
    return np.tile(ov.astype(np.float32), (1, GQA))


def _aug_keys(pad, n_keys):
    a = np.zeros((pad + n_keys, LANES), np.float32)
    a[:pad, AUG_PAD] = NEG
    k = np.arange(n_keys)
    a[pad + k, k // SLC_LEN] = NEG
    a[pad:, AUG_BIAS_HI] = 1.0
    a[pad:, AUG_BIAS_LO] = 1.0
    return a


def _attn_tiles(rel_bias, qb, t0, nc, nqb):
    rolled = nqb > 1
    i = np.arange(qb)[:, None]
    near = np.arange(3 * SLC_LEN)[None, :]
    d_near = SLC_LEN * (2 - near // SLC_LEN) + i - near % SLC_LEN
    c = np.arange(WINDOW + SLC_LEN)[None, :]
    d_win = i + WINDOW - c
    d_win = np.where(d_win < WINDOW, d_win, -1)
    if rolled:
        e = np.arange(2 * nc)[None, :] - nc
        d_cmp = i - (CMP_LEN - 1) - CMP_STRIDE * e
    else:
        d_cmp = t0 + i - CMP_STRIDE * np.arange(nc)[None, :] - (CMP_LEN - 1)
    widths = [d_near.shape[1], d_win.shape[1], d_cmp.shape[1]]
    padded = [-(-w // LANES) * LANES for w in widths]
    dist = np.full((qb, sum(padded)), -1, np.int32)
    off = 0
    offs = []
    for d, w, pw in zip((d_near, d_win, d_cmp), widths, padded):
        dist[:, off:off + w] = np.clip(d, -1, 4 * MAX_DIST)
        offs.append(off)
        off += pw
    tiles = rel_bias_tiles(rel_bias, jnp.asarray(dist))
    r2 = 2 * GQA * qb
    tiles = tiles.reshape(N_PAIRS, r2, dist.shape[1]) * LOG2E
    nb, wb, cbe = (tiles[..., o:o + w] for o, w in zip(offs, widths))
    if rolled:
        cb = cmp_bias_per_block(cbe, nqb, (t0 + qb) // CMP_STRIDE - t0 // CMP_STRIDE)
    else:
        cb = cbe[None]
    far = rel_bias[N_BUCKETS - 1].reshape(N_PAIRS, 2 * GQA) * LOG2E
    hi = far.astype(BF16).astype(F32)
    lo = far - hi
    rows = jnp.repeat(jnp.stack([hi, lo], axis=-1), qb, axis=1)
    augc = jnp.zeros((N_PAIRS, 2, r2, LANES), F32)
    augc = augc.at[:, 0, :, AUG_BIAS_HI].set(rows[..., 0]).at[:, 0, :, AUG_BIAS_LO].set(rows[..., 1])
    augc = augc.at[:, 1, :, AUG_PAD].set(1.0)
    return cb, nb, wb, augc


def _forward(x, ada, ada_kv, conv_state, h_state, past, w, shared):
    bn, s, _ = x.shape
    m = bn * s
    xf = x.reshape(m, D_MODEL)
    new_conv, new_h = [], []
    h = normmod(xf, w["norm_g"][0, 0], ada[0], 0, 1, bn, s)
    kv = None
    attn_args = None
    for l in range(DEPTH):
        if l < N_A_LAYERS:
            u = mm_plain(h, (w["w_in_a"], l)).reshape(bn, s, 2 * D_RNN)
            conv8 = jnp.pad(conv_state[l], ((0, 0), (8 - (CONV_W - 1), 0), (0, 0)))
            y, conv_o, h_o = rglru(u, conv8, h_state[l][:, None, :], w["conv_w"][l], w["conv_b"][l],
                                   w["w_rg_a"][l], w["b_rg_a"][l], w["w_rg_x"][l], w["b_rg_x"][l], w["lru_lambda"][l])
            new_conv.append(conv_o[:, 8 - (CONV_W - 1):])
            new_h.append(h_o[:, 0])
            a_in = y.reshape(m, D_RNN)
            w_out = (w["w_out_a"], l)
        else:
            lb = l - N_A_LAYERS
            if lb == 0:
                n_rows = 4 * N_KV_HEADS * HEAD_DIM
                if past is None:
                    kv_rows, new_win, kvb = mm_kv_padded(h_kv, w["w_kv"], bn, s, WINDOW, n_rows)
                    kv = kv_rows = kv_rows.reshape(bn, s, n_rows)
                else:
                    kv, kvb = mm_plain(h_kv, w["w_kv"]).reshape(bn, s, -1), None
                    kv_rows, new_win = kv[:, :, :n_rows], kv[:, :, n_rows:]
                attn_args = _prepare_attention(kv, kvb, past, w, shared, bn, s)
            u = mm_plain(h, (w["w_in_b"], lb)).reshape(bn, s, -1)
            a_in = nsa_attention(u, *attn_args[0], **attn_args[1]).reshape(m, N_HEADS * HEAD_DIM)
            w_out = (w["w_out_b"], lb)
        x_new, (h2,) = mm_resnorm(a_in, w_out, xf, (ada[l], 2), w["norm_g"][l, 1],
                                  [(w["norm_g"][l, 2], ada[l], 3, ada[l], 4)], bn, s)
        f = mm_swiglu(h2, (w["w_ffn_in"], l))
        heads = []
        if l + 1 < DEPTH:
            heads.append((w["norm_g"][l + 1, 0], ada[l + 1], 0, ada[l + 1], 1))
        if l + 1 == N_A_LAYERS:
            heads.append((w["norm_kv"], ada_kv, 0, ada_kv, 1))
        xf, hs = mm_resnorm(f, (w["w_ffn_out"], l), x_new, (ada[l], 5), w["norm_g"][l, 3], heads, bn, s)
        if heads:
            h = hs[0]
        if l + 1 == N_A_LAYERS:
            h_kv = hs[1]
    kv_rows = kv_rows.reshape(bn, s, 4, N_KV_HEADS, HEAD_DIM)
    new_win = new_win.reshape(bn, s, 2, N_KV_HEADS, HEAD_DIM)
    if past is None:
        win_state = new_win[:, -min(WINDOW, s):]
    else:
        cache_win = past[2]
        win_state = jnp.concatenate([cache_win, new_win], axis=1)[:, -cache_win.shape[1]:]
    return xf.reshape(bn, s, D_MODEL), kv_rows, win_state, jnp.stack(new_conv), jnp.stack(new_h)


def _prepare_attention(kv, kvb, past, w, shared, bn, s):
    rel_bias = w["rel_bias"]
    if past is None:
        t0, qb, pad = 0, Q_BLOCK, WINDOW
        n_keys = s
        xr = reorder_dense(kv)
        slc_kv = win_kv = kvb
        slc_cols, win_cols = (2, 3), (4, 5)
        augk = jnp.asarray(_aug_keys(pad, n_keys), BF16)
        augw = augk
        sb = bn
    else:
        cache, page_table, cache_win = past
        n_pool, page = cache.shape[:2]
        cache2 = cache.reshape(n_pool, page, -1)
        t0 = page_table.shape[1] * page
        qb, pad = s, 0
        n_keys = t0 + SLC_LEN
        assert cache_win.shape[1] == WINDOW and t0 >= WINDOW
        xr = reorder_paged(cache2, page_table)
        slc_kv, win_kv = assemble_sample_kv(cache2, page_table, kv, cache_win.reshape(bn, WINDOW, -1))
        slc_cols, win_cols = (0, 1), (0, 1)
        augk = jnp.asarray(_aug_keys(0, n_keys), BF16)
        augw = jnp.zeros((win_kv.shape[1], LANES), BF16)
        sb = 8
    n16 = (t0 + s) // CMP_STRIDE
    n_cmp = n16 - CMP_LEN // CMP_STRIDE + 1
    nc = xr.shape[2]
    assert nc >= n_cmp and nc % LANES == 0
    n_slc = -(-(t0 + s) // SLC_LEN)
    assert n_slc <= N_BLOCK_LANES
    cmp_kv = compress(xr, shared["w1cat"], shared["c0"], shared["w2bd"], sb)
    nqb = s // qb
    assert nqb == 1 or t0 == 0
    tiles = _attn_tiles(rel_bias, qb, t0, nc, nqb)
    ovt = jnp.asarray(_overlap_t(nc, n_cmp, n_slc), BF16)
    args = (cmp_kv, slc_kv, slc_cols, win_kv, win_cols, augk, augw, ovt, tiles, shared["ex"])
    return args, dict(qb=qb, n_slc=n_slc, pad_s=pad, t0=t0, sbq=1 if nqb > 1 else min(bn, 2))


def kernel(x_prompt, x_sample, c_prompt, c_sample, cache_kv, cache_win, state_conv, state_h, page_table, w_ada, b_ada, norm_g, w_in_a, conv_w, conv_b, w_rg_a, b_rg_a, w_rg_x, b_rg_x, lru_lambda, w_out_a, w_ada_kv, b_ada_kv, norm_kv, w_kv, cmp_pos, cmp_w1, cmp_w2, w_in_b, w_out_b, rel_bias, w_ffn_in, w_ffn_out):
    bp = x_prompt.shape[0]
    bs = x_sample.shape[0]

    perm = _head_perm()
    n_q = N_HEADS * HEAD_DIM
    w_in_b_p = jnp.concatenate([w_in_b[:, :, :n_q][:, :, perm], w_in_b[:, :, n_q:],
                                jnp.zeros(w_in_b.shape[:2] + (LANES - 3 * N_HEADS,), F32)], axis=-1)
    w = dict(norm_g=norm_g, norm_kv=norm_kv, conv_w=conv_w, conv_b=conv_b, b_rg_a=b_rg_a, b_rg_x=b_rg_x,
             lru_lambda=lru_lambda, rel_bias=rel_bias,
             w_in_a=w_in_a, w_rg_a=w_rg_a.astype(BF16), w_rg_x=w_rg_x.astype(BF16),
             w_out_a=w_out_a, w_kv=w_kv, w_in_b=w_in_b_p,
             w_out_b=w_out_b[:, perm, :], w_ffn_in=w_ffn_in, w_ffn_out=w_ffn_out)

    eye = jnp.eye(N_KV_HEADS, dtype=F32)
    w1r = cmp_w1.reshape(2, CMP_LEN, HEAD_DIM, CMP_HIDDEN)
    w1bd = jnp.einsum("hk,srdj->srhdkj", eye, w1r).reshape(2, CMP_LEN, N_KV_HEADS * HEAD_DIM, N_KV_HEADS * CMP_HIDDEN)
    w2bd = jnp.einsum("hk,sjd->shjkd", eye, cmp_w2).reshape(2, N_KV_HEADS * CMP_HIDDEN, N_KV_HEADS * HEAD_DIM)
    w1cat = jnp.concatenate([w1bd[:, :CMP_STRIDE], w1bd[:, CMP_STRIDE:]], axis=-1).astype(BF16)
    pos_rows = jnp.pad(cmp_pos.reshape(2, 1, CMP_LEN * HEAD_DIM), ((0, 0), (0, 7), (0, 0)))
    c0 = jnp.stack([mm_plain(pos_rows[st], (cmp_w1, st))[:1] for st in range(2)])
    shared = dict(w1cat=w1cat, c0=jnp.tile(c0, (1, 1, N_KV_HEADS)), w2bd=w2bd.astype(BF16),
                  ex=jnp.asarray(_gate_expand(), BF16))

    n_c = bp + bs
    c_all = jnp.concatenate([c_prompt, c_sample, jnp.zeros((-n_c % 8, D_MODEL), F32)], axis=0)
    ada_all = [mm_plain(c_all, (w_ada, l), bias=b_ada[l], act="silu", tn=2048) for l in range(DEPTH)]
    ada_kv_all = mm_plain(c_all, w_ada_kv, bias=b_ada_kv, act="silu", tn=2048)
    ada_p = [a[:bp, None, :] for a in ada_all]
    ada_s = [a[bp:n_c, None, :] for a in ada_all]

    zero_conv = jnp.zeros((N_A_LAYERS, bp, CONV_W - 1, D_RNN), F32)
    zero_h = jnp.zeros((N_A_LAYERS, bp, D_RNN), F32)
    y_p, kv_p, win_p, conv_p, h_p = _forward(x_prompt, ada_p, ada_kv_all[:bp, None, :], zero_conv, zero_h, None, w, shared)
    y_s, kv_s, win_s, conv_s, h_s = _forward(x_sample, ada_s, ada_kv_all[bp:n_c, None, :], state_conv, state_h,
                                             (cache_kv, page_table, cache_win), w, shared)
    return (y_p, y_s, kv_p, kv_s, win_p, win_s, conv_p, conv_s, h_p, h_s)
```

```python
import functools
import math

import numpy as np
import jax
import jax.numpy as jnp
from jax import lax
from jax.experimental import pallas as pl
from jax.experimental.pallas import tpu as pltpu

D_MODEL = 1024
DEPTH = 4
N_A_LAYERS = 2
D_RNN = 1024
LRU_BLOCKS = 8
LRU_BW = 128
CONV_W = 4
LRU_C = 8.0
N_HEADS = 16
HEAD_DIM = 64
N_KV_HEADS = 4
GQA = 4
CMP_LEN = 32
CMP_STRIDE = 16
CMP_HIDDEN = 128
SLC_LEN = 64
N_SELECT = 16
WINDOW = 512
Q_BLOCK = 64
N_BUCKETS = 32
MAX_DIST = 128
D_FF = 2816
EPS = 1e-6
NEG = -1e30
FORCE = 1e9

LANES = 128
PAIR_W = 2 * HEAD_DIM
N_PAIRS = N_KV_HEADS // 2
AUG_BIAS_HI = 64
AUG_BIAS_LO = 65
AUG_PAD = 66
N_BLOCK_LANES = 64
FAR_CHUNK = 1024
CMP_RSTEP = 4
RESNORM_PART = 512
LOG2E = 1.4426950408889634
VMEM_LIMIT = 56 * 1024 * 1024

F32 = jnp.float32
BF16 = jnp.bfloat16


def _cparams(*sem):
    return pltpu.CompilerParams(dimension_semantics=sem, vmem_limit_bytes=VMEM_LIMIT)


def _dot(a, b):
    return jnp.dot(a, b, preferred_element_type=F32)


def _dot_nt(a, b):
    return lax.dot_general(a, b, (((1,), (1,)), ((), ())), preferred_element_type=F32)


def _gelu_tanh(x):
    return 0.5 * x * (1.0 + jnp.tanh(math.sqrt(2.0 / math.pi) * (x + 0.044715 * (x * x * x))))


def _rms(x, gain):
    return x * lax.rsqrt(jnp.mean(x * x, axis=-1, keepdims=True) + EPS) * gain


def _rows(vec_ref, bb, ts):
    v = vec_ref[...]
    d = v.shape[-1]
    return jnp.broadcast_to(v, (bb, ts, d)).reshape(bb * ts, d)


def _vec_spec(bb, tiles_per_seq, col, ngrid):
    if ngrid == 1:
        return pl.BlockSpec((bb, 1, D_MODEL), lambda i: (i // tiles_per_seq, 0, col))
    return pl.BlockSpec((bb, 1, D_MODEL), lambda j, i: (i // tiles_per_seq, 0, col))


def _row_tiling(bn, s, tm=512):
    if s >= tm:
        assert s % tm == 0
        return tm, 1, tm, s // tm
    assert tm % s == 0 and s % 8 == 0
    bb = min(bn, tm // s)
    assert bn % bb == 0
    return bb * s, bb, s, 1


def _bf16_weight(w_ref, cache_ref, first_row_tile):
    if cache_ref is None:
        return w_ref[...].astype(BF16)

    @pl.when(first_row_tile)
    def _():
        cache_ref[...] = w_ref[...].astype(BF16)

    return cache_ref[...]


def _weight_cache(w, block, n_row_tiles):
    return [pltpu.VMEM(block, BF16)] if (w.dtype != BF16 and n_row_tiles > 1) else []


def _mm_body(*refs, act, has_bias, cached):
    cache_ref = refs[-1] if cached else None
    refs = refs[:-1] if cached else refs
    if has_bias:
        a_ref, w_ref, b_ref, o_ref = refs
    else:
        a_ref, w_ref, o_ref = refs
    a = a_ref[...]
    if act == "silu":
        a = a.astype(F32)
        a = a * jax.nn.sigmoid(a)
    y = _dot(a.astype(BF16), _bf16_weight(w_ref, cache_ref, pl.program_id(1) == 0))
    if has_bias:
        y = y + b_ref[...]
    o_ref[...] = y.astype(o_ref.dtype)


def _layered(w):
    return w if isinstance(w, tuple) else (w[None], 0)


def mm_plain(a, w, bias=None, act=None, tm=512, tn=None, out_dtype=F32):
    w, layer = _layered(w)
    m, k = a.shape
    n = w.shape[2]
    tm = min(tm, m)
    tn = tn or n
    assert m % tm == 0 and n % tn == 0
    in_specs = [pl.BlockSpec((tm, k), lambda j, i: (i, 0)), pl.BlockSpec((None, k, tn), lambda j, i: (layer, 0, j))]
    args = [a, w]
    if bias is not None:
        in_specs.append(pl.BlockSpec((1, tn), lambda j, i: (0, j)))
        args.append(bias.reshape(1, n))
    cache = _weight_cache(w, (k, tn), m // tm)
    return pl.pallas_call(
        functools.partial(_mm_body, act=act, has_bias=bias is not None, cached=bool(cache)),
        grid=(n // tn, m // tm),
        in_specs=in_specs,
        out_specs=pl.BlockSpec((tm, tn), lambda j, i: (i, j)),
        out_shape=jax.ShapeDtypeStruct((m, n), out_dtype),
        scratch_shapes=cache,
        compiler_params=_cparams("arbitrary", "arbitrary"),
        name="mm_plain",
    )(*args)


def _kv_padded_body(a_ref, w_ref, kv_ref, kvb_ref, cache_ref):
    b, t = pl.program_id(0), pl.program_id(1)
    w = _bf16_weight(w_ref, cache_ref, (b == 0) & (t == 0))

    @pl.when(t == 0)
    def _():
        kvb_ref[...] = jnp.zeros_like(kvb_ref)

    @pl.when(t > 0)
    def _():
        y = _dot(a_ref[...], w)
        kv_ref[...] = y
        kvb_ref[...] = y.astype(kvb_ref.dtype)


def mm_kv_padded(a, w, bn, s, pad):
    m, k = a.shape
    n = w.shape[1]
    tm = pad
    assert s % tm == 0
    tps = s // tm
    src = lambda b, t: (b * tps + jnp.maximum(t - 1, 0), 0)
    return pl.pallas_call(
        _kv_padded_body,
        grid=(bn, tps + 1),
        in_specs=[pl.BlockSpec((tm, k), src),
                  pl.BlockSpec((k, n), lambda b, t: (0, 0), pipeline_mode=pl.Buffered(1))],
        out_specs=[pl.BlockSpec((tm, n), src), pl.BlockSpec((None, tm, n), lambda b, t: (b, t, 0))],
        out_shape=[jax.ShapeDtypeStruct((m, n), F32), jax.ShapeDtypeStruct((bn, pad + s, n), BF16)],
        scratch_shapes=[pltpu.VMEM((k, n), BF16)],
        compiler_params=_cparams("arbitrary", "arbitrary"),
        name="mm_kv_padded",
    )(a, w)


def _swiglu_body(h_ref, wg_ref, wu_ref, o_ref, *caches):
    first = pl.program_id(1) == 0
    wg = _bf16_weight(wg_ref, caches[0] if caches else None, first)
    wu = _bf16_weight(wu_ref, caches[1] if caches else None, first)
    tm = h_ref.shape[0]
    n_part = tm // RESNORM_PART if tm % RESNORM_PART == 0 else 1
    rows = tm // n_part
    gu = [(_dot(h_ref[i * rows:(i + 1) * rows, :], wg), _dot(h_ref[i * rows:(i + 1) * rows, :], wu))
          for i in range(n_part)]
    for i, (g, u) in enumerate(gu):
        o_ref[i * rows:(i + 1) * rows, :] = (g * jax.nn.sigmoid(g) * u).astype(o_ref.dtype)


def mm_swiglu(h, w_in, tm=2 * RESNORM_PART, tn=1408):
    w_in, layer = _layered(w_in)
    m, k = h.shape
    nf = w_in.shape[2] // 2
    tm = min(tm, m)
    assert nf % tn == 0 and m % tm == 0
    nj = nf // tn
    return pl.pallas_call(
        _swiglu_body,
        grid=(nj, m // tm),
        in_specs=[pl.BlockSpec((tm, k), lambda j, i: (i, 0)),
                  pl.BlockSpec((None, k, tn), lambda j, i: (layer, 0, j), pipeline_mode=pl.Buffered(1)),
                  pl.BlockSpec((None, k, tn), lambda j, i: (layer, 0, j + nj), pipeline_mode=pl.Buffered(1))],
        out_specs=pl.BlockSpec((tm, tn), lambda j, i: (i, j)),
        out_shape=jax.ShapeDtypeStruct((m, nf), BF16),
        scratch_shapes=_weight_cache(w_in, (k, tn), m // tm) * 2,
        compiler_params=_cparams("arbitrary", "arbitrary"),
        name="mm_swiglu",
    )(h, w_in, w_in)


def _normmod_body(x_ref, g_ref, sh_ref, sc_ref, o_ref, *, bb, ts):
    y = _rms(x_ref[...], g_ref[...])
    o_ref[...] = (y * (1.0 + _rows(sc_ref, bb, ts)) + _rows(sh_ref, bb, ts)).astype(o_ref.dtype)


def normmod(x, gain, ada, col_shift, col_scale, bn, s):
    m = x.shape[0]
    tm, bb, ts, tps = _row_tiling(bn, s)
    return pl.pallas_call(
        functools.partial(_normmod_body, bb=bb, ts=ts),
        grid=(m // tm,),
        in_specs=[pl.BlockSpec((tm, D_MODEL), lambda i: (i, 0)),
                  pl.BlockSpec((1, D_MODEL), lambda i: (0, 0)),
                  _vec_spec(bb, tps, col_shift, 1),
                  _vec_spec(bb, tps, col_scale, 1)],
        out_specs=pl.BlockSpec((tm, D_MODEL), lambda i: (i, 0)),
        out_shape=jax.ShapeDtypeStruct((m, D_MODEL), BF16),
        compiler_params=_cparams("arbitrary"),
        name="normmod",
    )(x, gain.reshape(1, D_MODEL), ada, ada)


def _resnorm_body(*refs, bb, ts, n_heads, cached):
    cache_ref = refs[-1] if cached else None
    refs = refs[:-1] if cached else refs
    a_ref, w_ref, x_ref, gate_ref, ngain_ref = refs[:5]
    head_refs = refs[5:5 + 3 * n_heads]
    xo_ref = refs[5 + 3 * n_heads]
    ho_refs = refs[6 + 3 * n_heads:]
    w = _bf16_weight(w_ref, cache_ref, pl.program_id(0) == 0)
    tm = a_ref.shape[0]
    n_part = tm // RESNORM_PART if tm % RESNORM_PART == 0 else 1
    rows = tm // n_part
    ys = [_dot(a_ref[i * rows:(i + 1) * rows, :].astype(BF16), w) for i in range(n_part)]
    if n_part > 1:
        assert bb == 1
        part_rows = lambda ref: _rows(ref, 1, rows)
    else:
        part_rows = lambda ref: _rows(ref, bb, ts)
    gate = part_rows(gate_ref)
    mods = [(head_refs[3 * k][...], part_rows(head_refs[3 * k + 1]), part_rows(head_refs[3 * k + 2]))
            for k in range(n_heads)]
    for i, y in enumerate(ys):
        sl = slice(i * rows, (i + 1) * rows)
        x = x_ref[sl, :] + gate * _rms(y, ngain_ref[...])
        xo_ref[sl, :] = x
        for k, (hg, sh, sc) in enumerate(mods):
            ho_refs[k][sl, :] = (_rms(x, hg) * (1.0 + sc) + sh).astype(ho_refs[k].dtype)


def mm_resnorm(a, w, x, gate, ngain, heads, bn, s):
    w, layer = _layered(w)
    m, k = a.shape
    tile_bytes = 2 * RESNORM_PART * (2 * k * a.dtype.itemsize + 4 * D_MODEL * 4 + 2 * len(heads) * D_MODEL * 2)
    weight_bytes = k * D_MODEL * (w.dtype.itemsize + 2)
    two_parts = s % (2 * RESNORM_PART) == 0 and tile_bytes + weight_bytes <= (3 * VMEM_LIMIT) // 4
    tm, bb, ts, tps = _row_tiling(bn, s, tm=2 * RESNORM_PART if two_parts else RESNORM_PART)
    row = lambda i: (i, 0)
    const = lambda i: (0, 0)
    in_specs = [pl.BlockSpec((tm, k), row),
                pl.BlockSpec((None, k, D_MODEL), lambda i: (layer, 0, 0), pipeline_mode=pl.Buffered(1)),
                pl.BlockSpec((tm, D_MODEL), row), _vec_spec(bb, tps, gate[1], 1),
                pl.BlockSpec((1, D_MODEL), const)]
    args = [a, w, x, gate[0], ngain.reshape(1, D_MODEL)]
    for hg, sh_arr, sh_col, sc_arr, sc_col in heads:
        in_specs += [pl.BlockSpec((1, D_MODEL), const), _vec_spec(bb, tps, sh_col, 1), _vec_spec(bb, tps, sc_col, 1)]
        args += [hg.reshape(1, D_MODEL), sh_arr, sc_arr]
    out_shape = [jax.ShapeDtypeStruct((m, D_MODEL), F32)] + [jax.ShapeDtypeStruct((m, D_MODEL), BF16)] * len(heads)
    out_specs = [pl.BlockSpec((tm, D_MODEL), row)] * (1 + len(heads))
    cache = _weight_cache(w, (k, D_MODEL), m // tm)
    outs = pl.pallas_call(
        functools.partial(_resnorm_body, bb=bb, ts=ts, n_heads=len(heads), cached=bool(cache)),
        grid=(m // tm,),
        in_specs=in_specs,
        out_specs=out_specs,
        out_shape=out_shape,
        scratch_shapes=cache,
        compiler_params=_cparams("arbitrary"),
        name="mm_resnorm",
    )(*args)
    return outs[0], list(outs[1:])


def _rglru_body(u_ref, conv8_ref, hprev_ref, cw_ref, cb_ref, wa_ref, ba_ref, wx_ref, bx_ref, lam_ref,
                y_ref, convo_ref, ho_ref, carry_h, carry_conv, *, bb, tt):
    j = pl.program_id(1)

    @pl.when(j == 0)
    def _():
        carry_h[...] = hprev_ref[...]
        carry_conv[...] = conv8_ref[...]

    u = u_ref[...]
    xb = u[:, :, :D_RNN]
    yb = u[:, :, D_RNN:]
    ext = jnp.concatenate([carry_conv[...], xb], axis=1)
    cw = cw_ref[...]
    xc = cb_ref[...] + ext[:, 5:5 + tt] * cw[0:1]
    for k in range(1, CONV_W):
        xc = xc + ext[:, 5 + k:5 + k + tt] * cw[k:k + 1]
    last8 = xb[:, tt - 8:tt]
    carry_conv[...] = last8
    convo_ref[...] = last8

    xc2 = xc.reshape(bb * tt, D_RNN)
    xcb = xc2.astype(BF16)

    def gate(w_ref, b_ref):
        cols = [_dot(xcb[:, n * LRU_BW:(n + 1) * LRU_BW], w_ref[n]) for n in range(LRU_BLOCKS)]
        return jax.nn.sigmoid(jnp.concatenate(cols, axis=1) + b_ref[...])

    r = gate(wa_ref, ba_ref)
    ig = gate(wx_ref, bx_ref)
    nl = -lam_ref[...]
    z = jnp.exp(-jnp.abs(nl))
    w1 = 1.0 + z
    log1p_z = jnp.where(w1 == 1.0, z, jnp.log(w1) * (z / jnp.where(w1 == 1.0, 1.0, w1 - 1.0)))
    softplus = jnp.maximum(nl, 0.0) + log1p_z
    log_a = -LRU_C * r * softplus
    a = jnp.exp(log_a)
    b = jnp.sqrt(1.0 - a * a) * (ig * xc2)

    a = a.reshape(bb * tt // 8, 8, D_RNN)
    b = b.reshape(bb * tt // 8, 8, D_RNN)
    row8 = lax.broadcasted_iota(jnp.int32, a.shape, 1)
    for d in (1, 2, 4):
        a_s = pltpu.roll(a, d, 1)
        b_s = pltpu.roll(b, d, 1)
        keep = row8 >= d
        b = jnp.where(keep, a * b_s + b, b)
        a = jnp.where(keep, a * a_s, a)

    a3 = a.reshape(bb, tt, D_RNN)
    b3 = b.reshape(bb, tt, D_RNN)
    carry = carry_h[...]
    groups = []
    for g in range(tt // 8):
        hg = a3[:, 8 * g:8 * g + 8] * carry + b3[:, 8 * g:8 * g + 8]
        carry = hg[:, 7:8]
        groups.append(hg)
    hs = groups[0] if len(groups) == 1 else jnp.concatenate(groups, axis=1)
    carry_h[...] = carry
    ho_ref[...] = carry
    y_ref[...] = (hs * _gelu_tanh(yb)).astype(y_ref.dtype)


def rglru(u, conv8, hprev, cw, cb, wa, ba, wx, bx, lam):
    bn, s, _ = u.shape
    if s >= 256:
        bb, tt = 1, 256
    else:
        bb, tt = min(bn, 512 // s), s
    assert s % tt == 0 and bn % bb == 0 and tt % 8 == 0
    vec = lambda a: a.reshape(1, D_RNN)
    c2 = lambda i, j: (0, 0)
    c3 = lambda i, j: (0, 0, 0)
    return pl.pallas_call(
        functools.partial(_rglru_body, bb=bb, tt=tt),
        grid=(bn // bb, s // tt),
        in_specs=[pl.BlockSpec((bb, tt, 2 * D_RNN), lambda i, j: (i, j, 0)),
                  pl.BlockSpec((bb, 8, D_RNN), lambda i, j: (i, 0, 0)),
                  pl.BlockSpec((bb, 1, D_RNN), lambda i, j: (i, 0, 0)),
                  pl.BlockSpec((CONV_W, D_RNN), c2),
                  pl.BlockSpec((1, D_RNN), c2),
                  pl.BlockSpec((LRU_BLOCKS, LRU_BW, LRU_BW), c3),
                  pl.BlockSpec((1, D_RNN), c2),
                  pl.BlockSpec((LRU_BLOCKS, LRU_BW, LRU_BW), c3),
                  pl.BlockSpec((1, D_RNN), c2),
                  pl.BlockSpec((1, D_RNN), c2)],
        out_specs=[pl.BlockSpec((bb, tt, D_RNN), lambda i, j: (i, j, 0)),
                   pl.BlockSpec((bb, 8, D_RNN), lambda i, j: (i, 0, 0)),
                   pl.BlockSpec((bb, 1, D_RNN), lambda i, j: (i, 0, 0))],
        out_shape=[jax.ShapeDtypeStruct((bn, s, D_RNN), BF16),
                   jax.ShapeDtypeStruct((bn, 8, D_RNN), F32),
                   jax.ShapeDtypeStruct((bn, 1, D_RNN), F32)],
        scratch_shapes=[pltpu.VMEM((bb, 1, D_RNN), F32), pltpu.VMEM((bb, 8, D_RNN), F32)],
        compiler_params=_cparams("arbitrary", "arbitrary"),
        name="rglru",
    )(u, conv8, hprev, cw, vec(cb), wa, vec(ba), wx, vec(bx), vec(lam))


def _bias_body(tb_ref, d_ref, o_ref):
    h = pl.program_id(0)
    d = d_ref[...]
    dc = jnp.maximum(d, 0)
    max_exact = N_BUCKETS // 2
    scaled = jnp.log(jnp.maximum(dc, 1).astype(F32) / max_exact) / math.log(MAX_DIST / max_exact)
    large = jnp.minimum(max_exact + (scaled * (N_BUCKETS - max_exact)).astype(jnp.int32), N_BUCKETS - 1)
    bucket = jnp.where(dc < max_exact, dc, large)
    acc = jnp.full(d.shape, tb_ref[0, h], F32)
    for k in range(1, N_BUCKETS):
        acc = jnp.where(bucket == k, tb_ref[k, h], acc)
    o_ref[...] = jnp.where(d < 0, NEG, acc)


def rel_bias_tiles(table, dist):
    rows, cols = dist.shape
    return pl.pallas_call(
        _bias_body,
        grid=(N_HEADS,),
        in_specs=[pl.BlockSpec(memory_space=pltpu.SMEM), pl.BlockSpec((rows, cols), lambda h: (0, 0))],
        out_specs=pl.BlockSpec((None, rows, cols), lambda h: (h, 0, 0)),
        out_shape=jax.ShapeDtypeStruct((N_HEADS, rows, cols), F32),
        compiler_params=_cparams("arbitrary"),
        name="rel_bias_tiles",
    )(table, dist)


def _reorder_emit(sources, stage, o_ref):
    n_col = stage.shape[0]
    row = 0
    for src in sources:
        n = src.shape[0]
        for c in range(n_col):
            stage[c, row:row + n, :] = src[:, c * LANES:(c + 1) * LANES]
        row += n
    n16 = row // CMP_STRIDE
    for r in range(CMP_STRIDE):
        x = jnp.concatenate([stage[c, pl.ds(r, n16, stride=CMP_STRIDE), :] for c in range(n_col)], axis=1)
        o_ref[r] = x.astype(o_ref.dtype)


def _reorder_dense_body(x_ref, o_ref, stage):
    _reorder_emit([x_ref], stage, o_ref)


def _reorder_paged_body(pt_ref, *refs, n_pages):
    del pt_ref
    _reorder_emit(refs[:n_pages], refs[n_pages + 1], refs[n_pages])


def reorder_dense(kv):
    bn, s, _ = kv.shape
    n16 = s // CMP_STRIDE
    return pl.pallas_call(
        _reorder_dense_body,
        grid=(bn,),
        in_specs=[pl.BlockSpec((None, s, 512), lambda b: (b, 0, 0))],
        out_specs=pl.BlockSpec((None, CMP_STRIDE, n16, 512), lambda b: (b, 0, 0, 0)),
        out_shape=jax.ShapeDtypeStruct((bn, CMP_STRIDE, n16, 512), BF16),
        scratch_shapes=[pltpu.VMEM((512 // LANES, s, LANES), F32)],
        compiler_params=_cparams("arbitrary"),
        name="reorder_dense",
    )(kv)


def reorder_paged(cache, page_table):
    bn, n_pages = page_table.shape
    page = cache.shape[1]
    n16 = n_pages * page // CMP_STRIDE
    page_specs = [pl.BlockSpec((None, page, 512), functools.partial(lambda b, pt, k: (pt[b * n_pages + k], 0, 0), k=k))
                  for k in range(n_pages)]
    return pl.pallas_call(
        functools.partial(_reorder_paged_body, n_pages=n_pages),
        grid_spec=pltpu.PrefetchScalarGridSpec(
            num_scalar_prefetch=1, grid=(bn,),
            in_specs=page_specs,
            out_specs=pl.BlockSpec((None, CMP_STRIDE, n16, 512), lambda b, pt: (b, 0, 0, 0)),
            scratch_shapes=[pltpu.VMEM((512 // LANES, n_pages * page, LANES), F32)]),
        out_shape=jax.ShapeDtypeStruct((bn, CMP_STRIDE, n16, 512), BF16),
        compiler_params=_cparams("arbitrary"),
        name="reorder_paged",
    )(page_table.reshape(-1), *([cache] * n_pages))


def _compress_body(x_ref, w1_ref, c0_ref, w2_ref, o_ref, acc, *, rows):
    r = pl.program_id(2)

    @pl.when(r == 0)
    def _():
        acc[...] = jnp.zeros_like(acc)

    a = acc[...]
    for k in range(CMP_RSTEP):
        a = a + _dot(x_ref[:, k].reshape(rows, 256), w1_ref[k])
    acc[...] = a

    @pl.when(r == CMP_STRIDE // CMP_RSTEP - 1)
    def _():
        half = acc.shape[1] // 2
        hidden = _gelu_tanh(acc[:, :half] + pltpu.roll(acc[:, half:], rows - 1, 0) + c0_ref[...])
        o_ref[...] = _dot(hidden.astype(BF16), w2_ref[...]).reshape(o_ref.shape).astype(o_ref.dtype)


def compress(x, w1cat, c0, w2bd, sb):
    bn, _, n16, _ = x.shape
    assert bn % sb == 0
    rows = sb * n16
    nh = w1cat.shape[-1]
    return pl.pallas_call(
        functools.partial(_compress_body, rows=rows),
        grid=(bn // sb, 2, CMP_STRIDE // CMP_RSTEP),
        in_specs=[pl.BlockSpec((sb, CMP_RSTEP, n16, 256), lambda g, s, r: (g, r, 0, s)),
                  pl.BlockSpec((None, CMP_RSTEP, 256, nh), lambda g, s, r: (s, r, 0, 0)),
                  pl.BlockSpec((None, 1, nh // 2), lambda g, s, r: (s, 0, 0)),
                  pl.BlockSpec((None, nh // 2, 256), lambda g, s, r: (s, 0, 0))],
        out_specs=pl.BlockSpec((sb, n16, 256), lambda g, s, r: (g, 0, s)),
        out_shape=jax.ShapeDtypeStruct((bn, n16, 512), BF16),
        scratch_shapes=[pltpu.VMEM((rows, nh), F32)],
        compiler_params=_cparams("arbitrary", "arbitrary", "arbitrary"),
        name="compress",
    )(x, w1cat, c0, w2bd)


def _assemble_body(pt_ref, *refs, n_pages, page, s_new, win_len):
    del pt_ref
    pages = refs[:n_pages]
    new_slc_ref, new_win_ref, cwin_ref, slc_ref, win_ref = refs[n_pages:]
    for k in range(n_pages):
        slc_ref[k * page:(k + 1) * page, :] = pages[k][...].astype(slc_ref.dtype)
    pad = jnp.zeros((SLC_LEN - s_new, 512), F32)
    slc_ref[n_pages * page:n_pages * page + SLC_LEN, :] = jnp.concatenate(
        [new_slc_ref[...], pad], axis=0).astype(slc_ref.dtype)
    win_ref[0:win_len, :] = cwin_ref[...].astype(win_ref.dtype)
    win_ref[win_len:win_len + SLC_LEN, :] = jnp.concatenate([new_win_ref[...], pad], axis=0).astype(win_ref.dtype)


def assemble_sample_kv(cache, page_table, kv_new, cache_win):
    bn, n_pages = page_table.shape
    page = cache.shape[1]
    s_new = kv_new.shape[1]
    win_len = cache_win.shape[1]
    past = n_pages * page
    page_specs = [pl.BlockSpec((None, page, 512), functools.partial(lambda b, pt, k: (pt[b * n_pages + k], 0, 1), k=k))
                  for k in range(n_pages)]
    return pl.pallas_call(
        functools.partial(_assemble_body, n_pages=n_pages, page=page, s_new=s_new, win_len=win_len),
        grid_spec=pltpu.PrefetchScalarGridSpec(
            num_scalar_prefetch=1, grid=(bn,),
            in_specs=page_specs + [pl.BlockSpec((None, s_new, 512), lambda b, pt: (b, 0, 1)),
                                   pl.BlockSpec((None, s_new, 512), lambda b, pt: (b, 0, 2)),
                                   pl.BlockSpec((None, win_len, 512), lambda b, pt: (b, 0, 0))],
            out_specs=[pl.BlockSpec((None, past + SLC_LEN, 512), lambda b, pt: (b, 0, 0)),
                       pl.BlockSpec((None, win_len + SLC_LEN, 512), lambda b, pt: (b, 0, 0))]),
        out_shape=[jax.ShapeDtypeStruct((bn, past + SLC_LEN, 512), BF16),
                   jax.ShapeDtypeStruct((bn, win_len + SLC_LEN, 512), BF16)],
        compiler_params=_cparams("arbitrary"),
        name="assemble_sample_kv",
    )(page_table.reshape(-1), *([cache] * n_pages), kv_new, kv_new, cache_win)


def _value_heads(vv):
    own_a = lax.broadcasted_iota(jnp.int32, vv.shape, 1) < HEAD_DIM
    return jnp.where(own_a, vv, 1.0), jnp.where(own_a, 1.0, vv)


def _softmax_update(state, s, vv, rr):
    smax = jnp.max(s, axis=-1, keepdims=True)
    mn = smax if state is None else jnp.maximum(state[0], smax)
    e = jnp.exp2(s - mn).astype(BF16)
    va, vb = _value_heads(vv)
    pv = jnp.concatenate([_dot(e[:rr], va), _dot(e[rr:], vb)], axis=0)
    return mn, (pv if state is None else jnp.exp2(state[0] - mn) * state[1] + pv)


def _softmax_finish(state):
    _, acc = state
    return acc / pltpu.roll(acc, HEAD_DIM, 1)


def _attn_seq(qi, q_ref, gl_ref, kc_ref, vc_ref, ks_ref, vs_ref, kw_ref, vw_ref, augk_ref, augw_ref,
              ovt_ref, cb_ref, nb_ref, wb_ref, augc_ref, ex_ref, o_ref,
              *, qb, n_slc, top_n, pad_s, t0, multi_block, n_streams):
    rr = GQA * qb
    r2 = 2 * rr
    pairs = range(N_PAIRS)
    is_a = lax.broadcasted_iota(jnp.int32, (rr, LANES), 1) < HEAD_DIM
    if multi_block:
        cur = t0 // SLC_LEN + qi
        win_start = pl.multiple_of(qi * qb, SLC_LEN)
    else:
        cur = t0 // SLC_LEN
        win_start = 0

    def cols(ref, p, start, size):
        return ref[pl.ds(start, size), p * LANES:(p + 1) * LANES]

    def init():
        return jnp.full((r2, 1), -jnp.inf, F32), jnp.zeros((r2, LANES), F32)

    wlen = WINDOW + SLC_LEN
    augw = augw_ref[pl.ds(win_start, wlen), :]
    glt = jnp.concatenate([gl_ref[...]] * GQA, axis=0)
    slab = lax.broadcasted_iota(jnp.int32, (rr, LANES), 0) // qb
    gsrc = jnp.concatenate([jnp.where(slab == g, glt, 0.0) for g in range(GQA)], axis=1)
    ghi = gsrc.astype(BF16)
    glo = (gsrc - ghi.astype(F32)).astype(BF16)

    q2, s_win, s_cmp, o_win, gates, o_cmp, imp_t, partial = [], [], [], [], [], [], [], []
    for p in pairs:
        qfull = q_ref[:, p * GQA * LANES:(p + 1) * GQA * LANES]
        qs = jnp.concatenate([qfull[:, g * LANES:(g + 1) * LANES] for g in range(GQA)], axis=0)
        qs = qs * (HEAD_DIM ** -0.5 * LOG2E)
        q2.append(jnp.concatenate([jnp.where(is_a, qs, 0.0), jnp.where(is_a, 0.0, qs)], axis=0).astype(BF16))
        s_cmp.append(_dot_nt(q2[p], kc_ref[:, p * LANES:(p + 1) * LANES]) + cb_ref[p])
    for p in pairs:
        lhs_win = jnp.concatenate([q2[p], augc_ref[p, 1].astype(BF16)], axis=1)
        kk = jnp.concatenate([cols(kw_ref, p, win_start, wlen), augw], axis=1)
        s_win.append(_dot_nt(lhs_win, kk) + wb_ref[p])

    for p in pairs:
        s = s_cmp[p]
        m = jnp.max(s, axis=-1, keepdims=True)
        e = jnp.exp2(s - m)
        pr = e / jnp.sum(e, axis=-1, keepdims=True)
        pr = jnp.where(m > 0.1 * NEG, pr, 0.0)
        o_cmp.append(_dot(pr.astype(BF16), vc_ref[:, p * LANES:(p + 1) * LANES]))

        pcat = jnp.concatenate(
            [jnp.concatenate([pr[x * rr + g * qb:x * rr + (g + 1) * qb] for g in range(GQA)], axis=1)
             for x in range(2)], axis=0)
        if 2 * qb < LANES:
            pcat = jnp.concatenate([pcat, jnp.zeros((LANES - 2 * qb, pcat.shape[1]), F32)], axis=0)
        imp_t.append(_dot_nt(ovt_ref[...], pcat.astype(BF16)))

    def ranked():
        jr = lax.broadcasted_iota(jnp.int32, (N_BLOCK_LANES, LANES), 0)
        forced = (jr == 0) | (jr == cur) | (jr == cur - 1)
        sub8 = lax.broadcasted_iota(jnp.int32, (8, LANES), 0)
        res = []
        for p in pairs:
            v = jnp.where(forced, FORCE, jnp.where((jr > cur) | (jr >= n_slc), NEG, imp_t[p][:N_BLOCK_LANES]))
            groups = [v[8 * r:8 * r + 8] for r in range(N_BLOCK_LANES // 8)]
            ranks = [jnp.zeros((8, LANES), F32) for _ in groups]
            for k in range(n_slc):
                vk = v[k:k + 1, :]
                for r, vr in enumerate(groups):
                    if 8 * r > k:
                        ranks[r] = ranks[r] + jnp.where(vk >= vr, 1.0, 0.0)
                    elif 8 * r + 7 < k:
                        ranks[r] = ranks[r] + jnp.where(vk > vr, 1.0, 0.0)
                    else:
                        ranks[r] = ranks[r] + jnp.where(sub8 > k - 8 * r, jnp.where(vk >= vr, 1.0, 0.0),
                                                        jnp.where(vk > vr, 1.0, 0.0))
            res.append(jnp.where(jnp.concatenate(ranks, axis=0) < top_n, 0.0, 1.0))
        return tuple(res)

    notsel_t = ranked()

    for p in pairs:
        o_win.append(_softmax_finish(_softmax_update(None, s_win[p], cols(vw_ref, p, win_start, wlen), rr)))
        gates.append(jax.nn.sigmoid(_dot(ghi, ex_ref[p]) + _dot(glo, ex_ref[p])))
        partial.append(gates[p][:, :LANES] * jnp.where(is_a, o_cmp[p][:rr], o_cmp[p][rr:])
                       + gates[p][:, 2 * LANES:] * jnp.where(is_a, o_win[p][:rr], o_win[p][rr:]))

    st = pad_s + (cur - 2) * SLC_LEN
    if multi_block:
        st = pl.multiple_of(st, SLC_LEN)
    augk_near = augk_ref[pl.ds(st, 3 * SLC_LEN), :]
    lane2 = lax.broadcasted_iota(jnp.int32, (r2, LANES), 1)
    lhs_far, s_near = [], []
    for p in pairs:
        notsel = jnp.concatenate([notsel_t[p], jnp.zeros((LANES - N_BLOCK_LANES, LANES), F32)], axis=0).T
        ns = jnp.concatenate([notsel[x * qb:(x + 1) * qb] for x in range(2) for _ in range(GQA)], axis=0)
        ns_far = jnp.where((lane2 >= cur - 2) & (lane2 < N_BLOCK_LANES), 1.0, ns)
        lhs_far.append(jnp.concatenate([q2[p], (ns_far + augc_ref[p, 0]).astype(BF16)], axis=1))
        lhs_near = jnp.concatenate([q2[p], (ns + augc_ref[p, 1]).astype(BF16)], axis=1)
        kk = jnp.concatenate([cols(ks_ref, p, st, 3 * SLC_LEN), augk_near], axis=1)
        s_near.append(_dot_nt(lhs_near, kk) + nb_ref[p])
    near = [_softmax_update(None, s_near[p], cols(vs_ref, p, st, 3 * SLC_LEN), rr) for p in pairs]

    def far_start(it, k):
        return pl.multiple_of(pad_s + (it * n_streams + k) * FAR_CHUNK, math.gcd(pad_s, FAR_CHUNK))

    def far_scores(it):
        scores = []
        for k in range(n_streams):
            st = far_start(it, k)
            augk = augk_ref[pl.ds(st, FAR_CHUNK), :]
            for p in pairs:
                kk = jnp.concatenate([cols(ks_ref, p, st, FAR_CHUNK), augk], axis=1)
                scores.append(_dot_nt(lhs_far[p], kk))
        return tuple(scores)

    def far_update(it, states, scores):
        return tuple(_softmax_update(states[k * N_PAIRS + p], scores[k * N_PAIRS + p],
                                     cols(vs_ref, p, far_start(it, k), FAR_CHUNK), rr)
                     for k in range(n_streams) for p in pairs)

    per_iter = n_streams * (FAR_CHUNK // SLC_LEN)
    if multi_block:
        n_iter = (jnp.maximum(cur - 2, 0) + per_iter - 1) // per_iter
        states = tuple(near) + tuple(init() for _ in range((n_streams - 1) * N_PAIRS))
        states = lax.fori_loop(0, n_iter, lambda it, states: far_update(it, states, far_scores(it)), states)
    else:
        states = tuple(near) + (None,) * ((n_streams - 1) * N_PAIRS)
        for it in range((max(cur - 2, 0) + per_iter - 1) // per_iter):
            states = far_update(it, states, far_scores(it))

    outs = []
    for p in pairs:
        mine = tuple(st for st in states[p::N_PAIRS] if st is not None)
        if len(mine) == 1:
            o_slc = _softmax_finish(mine[0])
        else:
            m_all = functools.reduce(jnp.maximum, [m for m, _ in mine])
            o_slc = _softmax_finish((m_all, sum(acc * jnp.exp2(m - m_all) for m, acc in mine)))
        out = partial[p] + gates[p][:, LANES:2 * LANES] * jnp.where(is_a, o_slc[:rr], o_slc[rr:])
        outs += [out[g * qb:(g + 1) * qb] for g in range(GQA)]
    o_ref[...] = jnp.concatenate(outs, axis=1).astype(o_ref.dtype)


def _attn_body(*refs, sbq, **kw):
    qi = pl.program_id(1)
    per_seq, shared = refs[:8], refs[8:16]
    for i in range(sbq):
        _attn_seq(qi, *[r.at[i] for r in per_seq], *shared, refs[16].at[i], **kw)


def _cmp_bias_body(cbe_ref, o_ref, *, nc, step):
    tau = pl.program_id(0) * step
    for p in range(N_PAIRS):
        o_ref[p] = pltpu.roll(cbe_ref[p], tau, 1)[:, nc:]


def cmp_bias_per_block(cbe, nqb, step):
    n_p, rows, nc2 = cbe.shape
    nc = nc2 // 2
    return pl.pallas_call(
        functools.partial(_cmp_bias_body, nc=nc, step=step),
        grid=(nqb,),
        in_specs=[pl.BlockSpec(cbe.shape, lambda i: (0, 0, 0))],
        out_specs=pl.BlockSpec((None, n_p, rows, nc), lambda i: (i, 0, 0, 0)),
        out_shape=jax.ShapeDtypeStruct((nqb, n_p, rows, nc), F32),
        compiler_params=_cparams("arbitrary"),
        name="cmp_bias_per_block",
    )(cbe)


def nsa_attention(u, cmp_kv, slc_kv, slc_cols, win_kv, win_cols, augk, augw, ovt, tiles, ex,
                  *, qb, n_slc, pad_s, t0, sbq):
    bn, s, _ = u.shape
    nqb = s // qb
    nc = cmp_kv.shape[1]
    r2 = 2 * GQA * qb
    cb, nb, wb, augc = tiles
    srows = slc_kv.shape[1]
    wrows = win_kv.shape[1]
    kcol, vcol = slc_cols
    kwcol, vwcol = win_cols
    assert qb == SLC_LEN or nqb == 1
    assert t0 % SLC_LEN == 0 and bn % sbq == 0
    per_chunk = FAR_CHUNK // SLC_LEN
    max_chunks = -(-max(n_slc - 3, 0) // per_chunk)
    n_streams = min(1, max_chunks) if nqb > 1 else max_chunks
    n_streams = max(n_streams, 1)
    assert pad_s + -(-max_chunks // n_streams) * n_streams * FAR_CHUNK <= srows
    body = functools.partial(_attn_body, sbq=sbq, qb=qb, n_slc=n_slc, top_n=min(N_SELECT, n_slc),
                             pad_s=pad_s, t0=t0, multi_block=nqb > 1, n_streams=n_streams)
    whole = lambda a: pl.BlockSpec(a.shape, lambda b, i: (0,) * a.ndim)
    kvw = N_PAIRS * LANES
    nq = N_HEADS * HEAD_DIM
    return pl.pallas_call(
        body,
        grid=(bn // sbq, nqb),
        in_specs=[pl.BlockSpec((sbq, qb, nq), lambda b, i: (b, i, 0)),
                  pl.BlockSpec((sbq, qb, LANES), lambda b, i: (b, i, nq // LANES)),
                  pl.BlockSpec((sbq, nc, kvw), lambda b, i: (b, 0, 0)),
                  pl.BlockSpec((sbq, nc, kvw), lambda b, i: (b, 0, 1)),
                  pl.BlockSpec((sbq, srows, kvw), lambda b, i: (b, 0, kcol)),
                  pl.BlockSpec((sbq, srows, kvw), lambda b, i: (b, 0, vcol)),
                  pl.BlockSpec((sbq, wrows, kvw), lambda b, i: (b, 0, kwcol)),
                  pl.BlockSpec((sbq, wrows, kvw), lambda b, i: (b, 0, vwcol)),
                  whole(augk), whole(augw), whole(ovt),
                  pl.BlockSpec((None, N_PAIRS, r2, nc), lambda b, i: (i, 0, 0, 0)),
                  whole(nb), whole(wb), whole(augc), whole(ex)],
        out_specs=pl.BlockSpec((sbq, qb, nq), lambda b, i: (b, i, 0)),
        out_shape=jax.ShapeDtypeStruct((bn, s, nq), F32),
        compiler_params=_cparams("arbitrary", "arbitrary"),
        name="nsa_attention",
    )(u, u, cmp_kv, cmp_kv, slc_kv, slc_kv, win_kv, win_kv, augk, augw, ovt, cb, nb, wb, augc, ex)


def _head_perm():
    idx = np.empty((N_PAIRS, GQA, 2, HEAD_DIM), np.int32)
    for p in range(N_PAIRS):
        for g in range(GQA):
            for half in range(2):
                head = (2 * p + half) * GQA + g
                idx[p, g, half] = head * HEAD_DIM + np.arange(HEAD_DIM)
    return idx.reshape(-1)


def _gate_expand():
    ex = np.zeros((N_PAIRS, GQA * LANES, 3 * LANES), np.float32)
    for p in range(N_PAIRS):
        for g in range(GQA):
            for half in range(2):
                head = (2 * p + half) * GQA + g
                for br in range(3):
                    ex[p, g * LANES + head * 3 + br, br * LANES + half * HEAD_DIM:br * LANES + (half + 1) * HEAD_DIM] = 1.0
    return ex


def _overlap_t(nc, n_cmp, n_slc):
    i = np.arange(nc)[None, :] * CMP_STRIDE
    j = np.arange(LANES)[:, None] * SLC_LEN
    ov = (i < j + SLC_LEN) & (i + CMP_LEN > j) & (np.arange(nc)[None, :] < n_cmp) & (np.arange(LANES)[:, None] < n_slc)
    return np.tile(ov.astype(np.float32), (1, GQA))


def _aug_keys(pad, n_keys):
    a = np.zeros((pad + n_keys, LANES), np.float32)
    a[:pad, AUG_PAD] = NEG
    k = np.arange(n_keys)
    a[pad + k, k // SLC_LEN] = NEG
    a[pad:, AUG_BIAS_HI] = 1.0
    a[pad:, AUG_BIAS_LO] = 1.0
    return a


def _attn_tiles(rel_bias, qb, t0, nc, nqb):
    rolled = nqb > 1
    i = np.arange(qb)[:, None]
    near = np.arange(3 * SLC_LEN)[None, :]
    d_near = SLC_LEN * (2 - near // SLC_LEN) + i - near % SLC_LEN
    c = np.arange(WINDOW + SLC_LEN)[None, :]
    d_win = i + WINDOW - c
    d_win = np.where(d_win < WINDOW, d_win, -1)
    if rolled:
        e = np.arange(2 * nc)[None, :] - nc
        d_cmp = i - (CMP_LEN - 1) - CMP_STRIDE * e
    else:
        d_cmp = t0 + i - CMP_STRIDE * np.arange(nc)[None, :] - (CMP_LEN - 1)
    widths = [d_near.shape[1], d_win.shape[1], d_cmp.shape[1]]
    padded = [-(-w // LANES) * LANES for w in widths]
    dist = np.full((qb, sum(padded)), -1, np.int32)
    off = 0
    offs = []
    for d, w, pw in zip((d_near, d_win, d_cmp), widths, padded):
        dist[:, off:off + w] = np.clip(d, -1, 4 * MAX_DIST)
        offs.append(off)
        off += pw
    tiles = rel_bias_tiles(rel_bias, jnp.asarray(dist))
    r2 = 2 * GQA * qb
    tiles = tiles.reshape(N_PAIRS, r2, dist.shape[1]) * LOG2E
    nb, wb, cbe = (tiles[..., o:o + w] for o, w in zip(offs, widths))
    if rolled:
        cb = cmp_bias_per_block(cbe, nqb, (t0 + qb) // CMP_STRIDE - t0 // CMP_STRIDE)
    else:
        cb = cbe[None]
    far = rel_bias[N_BUCKETS - 1].reshape(N_PAIRS, 2 * GQA) * LOG2E
    hi = far.astype(BF16).astype(F32)
    lo = far - hi
    rows = jnp.repeat(jnp.stack([hi, lo], axis=-1), qb, axis=1)
    augc = jnp.zeros((N_PAIRS, 2, r2, LANES), F32)
    augc = augc.at[:, 0, :, AUG_BIAS_HI].set(rows[..., 0]).at[:, 0, :, AUG_BIAS_LO].set(rows[..., 1])
    augc = augc.at[:, 1, :, AUG_PAD].set(1.0)
    return cb, nb, wb, augc


def _forward(x, ada, ada_kv, conv_state, h_state, past, w, shared):
    bn, s, _ = x.shape
    m = bn * s
    xf = x.reshape(m, D_MODEL)
    new_conv, new_h = [], []
    h = normmod(xf, w["norm_g"][0, 0], ada[0], 0, 1, bn, s)
    kv = None
    attn_args = None
    for l in range(DEPTH):
        if l < N_A_LAYERS:
            u = mm_plain(h, (w["w_in_a"], l)).reshape(bn, s, 2 * D_RNN)
            conv8 = jnp.pad(conv_state[l], ((0, 0), (8 - (CONV_W - 1), 0), (0, 0)))
            y, conv_o, h_o = rglru(u, conv8, h_state[l][:, None, :], w["conv_w"][l], w["conv_b"][l],
                                   w["w_rg_a"][l], w["b_rg_a"][l], w["w_rg_x"][l], w["b_rg_x"][l], w["lru_lambda"][l])
            new_conv.append(conv_o[:, 8 - (CONV_W - 1):])
            new_h.append(h_o[:, 0])
            a_in = y.reshape(m, D_RNN)
            w_out = (w["w_out_a"], l)
        else:
            lb = l - N_A_LAYERS
            if lb == 0:
                if past is None:
                    kv, kvb = mm_kv_padded(h_kv, w["w_kv"], bn, s, WINDOW)
                else:
                    kv, kvb = mm_plain(h_kv, w["w_kv"]), None
                kv = kv.reshape(bn, s, 6 * N_KV_HEADS * HEAD_DIM)
                attn_args = _prepare_attention(kv, kvb, past, w, shared, bn, s)
            u = mm_plain(h, (w["w_in_b"], lb)).reshape(bn, s, -1)
            a_in = nsa_attention(u, *attn_args[0], **attn_args[1]).reshape(m, N_HEADS * HEAD_DIM)
            w_out = (w["w_out_b"], lb)
        x_new, (h2,) = mm_resnorm(a_in, w_out, xf, (ada[l], 2), w["norm_g"][l, 1],
                                  [(w["norm_g"][l, 2], ada[l], 3, ada[l], 4)], bn, s)
        f = mm_swiglu(h2, (w["w_ffn_in"], l))
        heads = []
        if l + 1 < DEPTH:
            heads.append((w["norm_g"][l + 1, 0], ada[l + 1], 0, ada[l + 1], 1))
        if l + 1 == N_A_LAYERS:
            heads.append((w["norm_kv"], ada_kv, 0, ada_kv, 1))
        xf, hs = mm_resnorm(f, (w["w_ffn_out"], l), x_new, (ada[l], 5), w["norm_g"][l, 3], heads, bn, s)
        if heads:
            h = hs[0]
        if l + 1 == N_A_LAYERS:
            h_kv = hs[1]
    kv_rows = kv[:, :, :4 * N_KV_HEADS * HEAD_DIM].reshape(bn, s, 4, N_KV_HEADS, HEAD_DIM)
    new_win = kv[:, :, 4 * N_KV_HEADS * HEAD_DIM:].reshape(bn, s, 2, N_KV_HEADS, HEAD_DIM)
    if past is None:
        win_state = new_win[:, -min(WINDOW, s):]
    else:
        cache_win = past[2]
        win_state = jnp.concatenate([cache_win, new_win], axis=1)[:, -cache_win.shape[1]:]
    return xf.reshape(bn, s, D_MODEL), kv_rows, win_state, jnp.stack(new_conv), jnp.stack(new_h)


def _prepare_attention(kv, kvb, past, w, shared, bn, s):
    rel_bias = w["rel_bias"]
    if past is None:
        t0, qb, pad = 0, Q_BLOCK, WINDOW
        n_keys = s
        xr = reorder_dense(kv)
        slc_kv = win_kv = kvb
        slc_cols, win_cols = (2, 3), (4, 5)
        augk = jnp.asarray(_aug_keys(pad, n_keys), BF16)
        augw = augk
        sb = bn
    else:
        cache, page_table, cache_win = past
        n_pool, page = cache.shape[:2]
        cache2 = cache.reshape(n_pool, page, -1)
        t0 = page_table.shape[1] * page
        qb, pad = s, 0
        n_keys = t0 + SLC_LEN
        assert cache_win.shape[1] == WINDOW and t0 >= WINDOW
        xr = reorder_paged(cache2, page_table)
        slc_kv, win_kv = assemble_sample_kv(cache2, page_table, kv, cache_win.reshape(bn, WINDOW, -1))
        slc_cols, win_cols = (0, 1), (0, 1)
        augk = jnp.asarray(_aug_keys(0, n_keys), BF16)
        augw = jnp.zeros((win_kv.shape[1], LANES), BF16)
        sb = 8
    n16 = (t0 + s) // CMP_STRIDE
    n_cmp = n16 - CMP_LEN // CMP_STRIDE + 1
    nc = xr.shape[2]
    assert nc >= n_cmp and nc % LANES == 0
    n_slc = -(-(t0 + s) // SLC_LEN)
    assert n_slc <= N_BLOCK_LANES
    cmp_kv = compress(xr, shared["w1cat"], shared["c0"], shared["w2bd"], sb)
    nqb = s // qb
    assert nqb == 1 or t0 == 0
    tiles = _attn_tiles(rel_bias, qb, t0, nc, nqb)
    ovt = jnp.asarray(_overlap_t(nc, n_cmp, n_slc), BF16)
    args = (cmp_kv, slc_kv, slc_cols, win_kv, win_cols, augk, augw, ovt, tiles, shared["ex"])
    return args, dict(qb=qb, n_slc=n_slc, pad_s=pad, t0=t0, sbq=1 if nqb > 1 else min(bn, 2))


def kernel(x_prompt, x_sample, c_prompt, c_sample, cache_kv, cache_win, state_conv, state_h, page_table, w_ada, b_ada, norm_g, w_in_a, conv_w, conv_b, w_rg_a, b_rg_a, w_rg_x, b_rg_x, lru_lambda, w_out_a, w_ada_kv, b_ada_kv, norm_kv, w_kv, cmp_pos, cmp_w1, cmp_w2, w_in_b, w_out_b, rel_bias, w_ffn_in, w_ffn_out):
    bp = x_prompt.shape[0]
    bs = x_sample.shape[0]

    perm = _head_perm()
    n_q = N_HEADS * HEAD_DIM
    w_in_b_p = jnp.concatenate([w_in_b[:, :, :n_q][:, :, perm], w_in_b[:, :, n_q:],
                                jnp.zeros(w_in_b.shape[:2] + (LANES - 3 * N_HEADS,), F32)], axis=-1)
    w = dict(norm_g=norm_g, norm_kv=norm_kv, conv_w=conv_w, conv_b=conv_b, b_rg_a=b_rg_a, b_rg_x=b_rg_x,
             lru_lambda=lru_lambda, rel_bias=rel_bias,
             w_in_a=w_in_a, w_rg_a=w_rg_a.astype(BF16), w_rg_x=w_rg_x.astype(BF16),
             w_out_a=w_out_a, w_kv=w_kv, w_in_b=w_in_b_p,
             w_out_b=w_out_b[:, perm, :], w_ffn_in=w_ffn_in, w_ffn_out=w_ffn_out)

    eye = jnp.eye(N_KV_HEADS, dtype=F32)
    w1r = cmp_w1.reshape(2, CMP_LEN, HEAD_DIM, CMP_HIDDEN)
    w1bd = jnp.einsum("hk,srdj->srhdkj", eye, w1r).reshape(2, CMP_LEN, N_KV_HEADS * HEAD_DIM, N_KV_HEADS * CMP_HIDDEN)
    w2bd = jnp.einsum("hk,sjd->shjkd", eye, cmp_w2).reshape(2, N_KV_HEADS * CMP_HIDDEN, N_KV_HEADS * HEAD_DIM)
    w1cat = jnp.concatenate([w1bd[:, :CMP_STRIDE], w1bd[:, CMP_STRIDE:]], axis=-1).astype(BF16)
    pos_rows = jnp.pad(cmp_pos.reshape(2, 1, CMP_LEN * HEAD_DIM), ((0, 0), (0, 7), (0, 0)))
    c0 = jnp.stack([mm_plain(pos_rows[st], (cmp_w1, st))[:1] for st in range(2)])
    shared = dict(w1cat=w1cat, c0=jnp.tile(c0, (1, 1, N_KV_HEADS)), w2bd=w2bd.astype(BF16),
                  ex=jnp.asarray(_gate_expand(), BF16))

    n_c = bp + bs
    c_all = jnp.concatenate([c_prompt, c_sample, jnp.zeros((-n_c % 8, D_MODEL), F32)], axis=0)
    ada_all = [mm_plain(c_all, (w_ada, l), bias=b_ada[l], act="silu", tn=2048) for l in range(DEPTH)]
    ada_kv_all = mm_plain(c_all, w_ada_kv, bias=b_ada_kv, act="silu", tn=2048)
    ada_p = [a[:bp, None, :] for a in ada_all]
    ada_s = [a[bp:n_c, None, :] for a in ada_all]

    zero_conv = jnp.zeros((N_A_LAYERS, bp, CONV_W - 1, D_RNN), F32)
    zero_h = jnp.zeros((N_A_LAYERS, bp, D_RNN), F32)
    y_p, kv_p, win_p, conv_p, h_p = _forward(x_prompt, ada_p, ada_kv_all[:bp, None, :], zero_conv, zero_h, None, w, shared)
    y_s, kv_s, win_s, conv_s, h_s = _forward(x_sample, ada_s, ada_kv_all[bp:n_c, None, :], state_conv, state_h,
                                             (cache_kv, page_table, cache_win), w, shared)
    return (y_p, y_s, kv_p, kv_s, win_p, win_s, conv_p, conv_s, h_p, h_s)
```

```python
import functools
import math

import numpy as np
import jax
import jax.numpy as jnp
from jax import lax
from jax.experimental import pallas as pl
from jax.experimental.pallas import tpu as pltpu

D_MODEL = 1024
DEPTH = 4
N_A_LAYERS = 2
D_RNN = 1024
LRU_BLOCKS = 8
LRU_BW = 128
CONV_W = 4
LRU_C = 8.0
N_HEADS = 16
HEAD_DIM = 64
N_KV_HEADS = 4
GQA = 4
CMP_LEN = 32
CMP_STRIDE = 16
CMP_HIDDEN = 128
SLC_LEN = 64
N_SELECT = 16
WINDOW = 512
Q_BLOCK = 64
N_BUCKETS = 32
MAX_DIST = 128
EPS = 1e-6
NEG = -1e30
FORCE = 1e9

LANES = 128
N_PAIRS = N_KV_HEADS // 2
STREAM_W = N_KV_HEADS * HEAD_DIM
KV_PAIR_W = 2 * STREAM_W
AUG_BIAS_HI = 64
AUG_BIAS_LO = 65
AUG_PAD = 66
N_BLOCK_LANES = 64
FAR_CHUNK = 1024
CMP_RSTEP = 4
RESNORM_PART = 512
LOG2E = 1.4426950408889634
VMEM_LIMIT = 56 * 1024 * 1024

F32 = jnp.float32
BF16 = jnp.bfloat16


def _cparams(*sem):
    return pltpu.CompilerParams(dimension_semantics=sem, vmem_limit_bytes=VMEM_LIMIT)


def _dot(a, b):
    return jnp.dot(a, b, preferred_element_type=F32)


def _dot_nt(a, b):
    return lax.dot_general(a, b, (((1,), (1,)), ((), ())), preferred_element_type=F32)


def _gelu_tanh(x):
    return 0.5 * x * (1.0 + jnp.tanh(math.sqrt(2.0 / math.pi) * (x + 0.044715 * (x * x * x))))


def _rms(x, gain):
    return x * lax.rsqrt(jnp.mean(x * x, axis=-1, keepdims=True) + EPS) * gain


def _rows(vec_ref, bb, ts):
    v = vec_ref[...]
    d = v.shape[-1]
    return jnp.broadcast_to(v, (bb, ts, d)).reshape(bb * ts, d)


def _vec_spec(bb, tiles_per_seq, col, ngrid):
    if ngrid == 1:
        return pl.BlockSpec((bb, 1, D_MODEL), lambda i: (i // tiles_per_seq, 0, col))
    return pl.BlockSpec((bb, 1, D_MODEL), lambda j, i: (i // tiles_per_seq, 0, col))


def _row_tiling(bn, s, tm=512):
    if s >= tm:
        assert s % tm == 0
        return tm, 1, tm, s // tm
    assert tm % s == 0 and s % 8 == 0
    bb = min(bn, tm // s)
    assert bn % bb == 0
    return bb * s, bb, s, 1


def _bf16_weight(w_ref, cache_ref, first_row_tile):
    if cache_ref is None:
        return w_ref[...].astype(BF16)

    @pl.when(first_row_tile)
    def _():
        cache_ref[...] = w_ref[...].astype(BF16)

    return cache_ref[...]


def _weight_cache(w, block, n_row_tiles):
    return [pltpu.VMEM(block, BF16)] if (w.dtype != BF16 and n_row_tiles > 1) else []


def _mm_body(*refs, act, has_bias, cached):
    cache_ref = refs[-1] if cached else None
    refs = refs[:-1] if cached else refs
    if has_bias:
        a_ref, w_ref, b_ref, o_ref = refs
    else:
        a_ref, w_ref, o_ref = refs
    a = a_ref[...]
    if act == "silu":
        a = a.astype(F32)
        a = a * jax.nn.sigmoid(a)
    y = _dot(a.astype(BF16), _bf16_weight(w_ref, cache_ref, pl.program_id(1) == 0))
    if has_bias:
        y = y + b_ref[...]
    o_ref[...] = y.astype(o_ref.dtype)


def _layered(w):
    return w if isinstance(w, tuple) else (w[None], 0)


def mm_plain(a, w, bias=None, act=None, tm=512, tn=None, out_dtype=F32):
    w, layer = _layered(w)
    m, k = a.shape
    n = w.shape[2]
    tm = min(tm, m)
    tn = tn or n
    assert m % tm == 0 and n % tn == 0
    in_specs = [pl.BlockSpec((tm, k), lambda j, i: (i, 0)), pl.BlockSpec((None, k, tn), lambda j, i: (layer, 0, j))]
    args = [a, w]
    if bias is not None:
        in_specs.append(pl.BlockSpec((1, tn), lambda j, i: (0, j)))
        args.append(bias.reshape(1, n))
    cache = _weight_cache(w, (k, tn), m // tm)
    return pl.pallas_call(
        functools.partial(_mm_body, act=act, has_bias=bias is not None, cached=bool(cache)),
        grid=(n // tn, m // tm),
        in_specs=in_specs,
        out_specs=pl.BlockSpec((tm, tn), lambda j, i: (i, j)),
        out_shape=jax.ShapeDtypeStruct((m, n), out_dtype),
        scratch_shapes=cache,
        compiler_params=_cparams("arbitrary", "arbitrary"),
        name="mm_plain",
    )(*args)


def _kv_padded_body(a_ref, w_ref, kv_ref, kvb_ref, cache_ref):
    b, t = pl.program_id(0), pl.program_id(1)
    w = _bf16_weight(w_ref, cache_ref, (b == 0) & (t == 0))

    @pl.when(t == 0)
    def _():
        kvb_ref[...] = jnp.zeros_like(kvb_ref)

    @pl.when(t > 0)
    def _():
        y = _dot(a_ref[...], w)
        kv_ref[...] = y
        kvb_ref[...] = y.astype(kvb_ref.dtype)


def mm_kv_padded(a, w, bn, s, pad):
    m, k = a.shape
    n = w.shape[1]
    tm = pad
    assert s % tm == 0
    tps = s // tm
    src = lambda b, t: (b * tps + jnp.maximum(t - 1, 0), 0)
    return pl.pallas_call(
        _kv_padded_body,
        grid=(bn, tps + 1),
        in_specs=[pl.BlockSpec((tm, k), src),
                  pl.BlockSpec((k, n), lambda b, t: (0, 0), pipeline_mode=pl.Buffered(1))],
        out_specs=[pl.BlockSpec((tm, n), src), pl.BlockSpec((None, tm, n), lambda b, t: (b, t, 0))],
        out_shape=[jax.ShapeDtypeStruct((m, n), F32), jax.ShapeDtypeStruct((bn, pad + s, n), BF16)],
        scratch_shapes=[pltpu.VMEM((k, n), BF16)],
        compiler_params=_cparams("arbitrary", "arbitrary"),
        name="mm_kv_padded",
    )(a, w)


def _swiglu_body(h_ref, wg_ref, wu_ref, o_ref, *caches):
    first = pl.program_id(1) == 0
    wg = _bf16_weight(wg_ref, caches[0] if caches else None, first)
    wu = _bf16_weight(wu_ref, caches[1] if caches else None, first)
    tm = h_ref.shape[0]
    n_part = tm // RESNORM_PART if tm % RESNORM_PART == 0 else 1
    rows = tm // n_part
    gu = [(_dot(h_ref[i * rows:(i + 1) * rows, :], wg), _dot(h_ref[i * rows:(i + 1) * rows, :], wu))
          for i in range(n_part)]
    for i, (g, u) in enumerate(gu):
        o_ref[i * rows:(i + 1) * rows, :] = (g * jax.nn.sigmoid(g) * u).astype(o_ref.dtype)


def mm_swiglu(h, w_in, tm=2 * RESNORM_PART, tn=None):
    w_in, layer = _layered(w_in)
    m, k = h.shape
    nf = w_in.shape[2] // 2
    tm = min(tm, m)
    tn = tn or nf // 2
    assert nf % tn == 0 and tn % LANES == 0 and m % tm == 0
    nj = nf // tn
    return pl.pallas_call(
        _swiglu_body,
        grid=(nj, m // tm),
        in_specs=[pl.BlockSpec((tm, k), lambda j, i: (i, 0)),
                  pl.BlockSpec((None, k, tn), lambda j, i: (layer, 0, j), pipeline_mode=pl.Buffered(1)),
                  pl.BlockSpec((None, k, tn), lambda j, i: (layer, 0, j + nj), pipeline_mode=pl.Buffered(1))],
        out_specs=pl.BlockSpec((tm, tn), lambda j, i: (i, j)),
        out_shape=jax.ShapeDtypeStruct((m, nf), BF16),
        scratch_shapes=_weight_cache(w_in, (k, tn), m // tm) * 2,
        compiler_params=_cparams("arbitrary", "arbitrary"),
        name="mm_swiglu",
    )(h, w_in, w_in)


def _normmod_body(x_ref, g_ref, sh_ref, sc_ref, o_ref, *, bb, ts):
    y = _rms(x_ref[...], g_ref[...])
    o_ref[...] = (y * (1.0 + _rows(sc_ref, bb, ts)) + _rows(sh_ref, bb, ts)).astype(o_ref.dtype)


def normmod(x, gain, ada, col_shift, col_scale, bn, s):
    m = x.shape[0]
    tm, bb, ts, tps = _row_tiling(bn, s)
    return pl.pallas_call(
        functools.partial(_normmod_body, bb=bb, ts=ts),
        grid=(m // tm,),
        in_specs=[pl.BlockSpec((tm, D_MODEL), lambda i: (i, 0)),
                  pl.BlockSpec((1, D_MODEL), lambda i: (0, 0)),
                  _vec_spec(bb, tps, col_shift, 1),
                  _vec_spec(bb, tps, col_scale, 1)],
        out_specs=pl.BlockSpec((tm, D_MODEL), lambda i: (i, 0)),
        out_shape=jax.ShapeDtypeStruct((m, D_MODEL), BF16),
        compiler_params=_cparams("arbitrary"),
        name="normmod",
    )(x, gain.reshape(1, D_MODEL), ada, ada)


def _resnorm_body(*refs, bb, ts, n_heads, cached):
    cache_ref = refs[-1] if cached else None
    refs = refs[:-1] if cached else refs
    a_ref, w_ref, x_ref, gate_ref, ngain_ref = refs[:5]
    head_refs = refs[5:5 + 3 * n_heads]
    xo_ref = refs[5 + 3 * n_heads]
    ho_refs = refs[6 + 3 * n_heads:]
    w = _bf16_weight(w_ref, cache_ref, pl.program_id(0) == 0)
    tm = a_ref.shape[0]
    n_part = tm // RESNORM_PART if tm % RESNORM_PART == 0 else 1
    rows = tm // n_part
    ys = [_dot(a_ref[i * rows:(i + 1) * rows, :].astype(BF16), w) for i in range(n_part)]
    if n_part > 1:
        assert bb == 1
        part_rows = lambda ref: _rows(ref, 1, rows)
    else:
        part_rows = lambda ref: _rows(ref, bb, ts)
    gate = part_rows(gate_ref)
    mods = [(head_refs[3 * k][...], part_rows(head_refs[3 * k + 1]), part_rows(head_refs[3 * k + 2]))
            for k in range(n_heads)]
    for i, y in enumerate(ys):
        sl = slice(i * rows, (i + 1) * rows)
        x = x_ref[sl, :] + gate * _rms(y, ngain_ref[...])
        xo_ref[sl, :] = x
        for k, (hg, sh, sc) in enumerate(mods):
            ho_refs[k][sl, :] = (_rms(x, hg) * (1.0 + sc) + sh).astype(ho_refs[k].dtype)


def mm_resnorm(a, w, x, gate, ngain, heads, bn, s):
    w, layer = _layered(w)
    m, k = a.shape
    tile_bytes = 2 * RESNORM_PART * (2 * k * a.dtype.itemsize + 4 * D_MODEL * 4 + 2 * len(heads) * D_MODEL * 2)
    weight_bytes = k * D_MODEL * (w.dtype.itemsize + 2)
    two_parts = s % (2 * RESNORM_PART) == 0 and tile_bytes + weight_bytes <= (3 * VMEM_LIMIT) // 4
    tm, bb, ts, tps = _row_tiling(bn, s, tm=2 * RESNORM_PART if two_parts else RESNORM_PART)
    row = lambda i: (i, 0)
    const = lambda i: (0, 0)
    in_specs = [pl.BlockSpec((tm, k), row),
                pl.BlockSpec((None, k, D_MODEL), lambda i: (layer, 0, 0), pipeline_mode=pl.Buffered(1)),
                pl.BlockSpec((tm, D_MODEL), row), _vec_spec(bb, tps, gate[1], 1),
                pl.BlockSpec((1, D_MODEL), const)]
    args = [a, w, x, gate[0], ngain.reshape(1, D_MODEL)]
    for hg, sh_arr, sh_col, sc_arr, sc_col in heads:
        in_specs += [pl.BlockSpec((1, D_MODEL), const), _vec_spec(bb, tps, sh_col, 1), _vec_spec(bb, tps, sc_col, 1)]
        args += [hg.reshape(1, D_MODEL), sh_arr, sc_arr]
    out_shape = [jax.ShapeDtypeStruct((m, D_MODEL), F32)] + [jax.ShapeDtypeStruct((m, D_MODEL), BF16)] * len(heads)
    out_specs = [pl.BlockSpec((tm, D_MODEL), row)] * (1 + len(heads))
    cache = _weight_cache(w, (k, D_MODEL), m // tm)
    outs = pl.pallas_call(
        functools.partial(_resnorm_body, bb=bb, ts=ts, n_heads=len(heads), cached=bool(cache)),
        grid=(m // tm,),
        in_specs=in_specs,
        out_specs=out_specs,
        out_shape=out_shape,
        scratch_shapes=cache,
        compiler_params=_cparams("arbitrary"),
        name="mm_resnorm",
    )(*args)
    return outs[0], list(outs[1:])


def _rglru_body(u_ref, conv8_ref, hprev_ref, cw_ref, cb_ref, wa_ref, ba_ref, wx_ref, bx_ref, lam_ref,
                y_ref, convo_ref, ho_ref, carry_h, carry_conv, *, bb, tt):
    j = pl.program_id(1)

    @pl.when(j == 0)
    def _():
        carry_h[...] = hprev_ref[...]
        carry_conv[...] = conv8_ref[...]

    u = u_ref[...]
    xb = u[:, :, :D_RNN]
    yb = u[:, :, D_RNN:]
    n_grp = tt // 8
    ext = jnp.concatenate([carry_conv[...].reshape(bb, 1, 8, D_RNN), xb.reshape(bb, n_grp, 8, D_RNN)], axis=1)
    row8 = lax.broadcasted_iota(jnp.int32, (bb, n_grp, 8, D_RNN), 2)
    cw = cw_ref[...]
    xc = cb_ref[...]
    for k in range(CONV_W - 1):
        d = CONV_W - 1 - k
        rolled = pltpu.roll(ext.reshape(bb * (n_grp + 1), 8, D_RNN), d, 1).reshape(bb, n_grp + 1, 8, D_RNN)
        xc = xc + jnp.where(row8 >= d, rolled[:, 1:], rolled[:, :n_grp]) * cw[k:k + 1]
    xc = xc + ext[:, 1:] * cw[CONV_W - 1:CONV_W]
    last8 = xb[:, tt - 8:tt]
    carry_conv[...] = last8
    convo_ref[...] = last8

    xc2 = xc.reshape(bb * tt, D_RNN)
    xcb = xc2.astype(BF16)

    def gate(w_ref, b_ref):
        cols = [_dot(xcb[:, n * LRU_BW:(n + 1) * LRU_BW], w_ref[n]) for n in range(LRU_BLOCKS)]
        return jax.nn.sigmoid(jnp.concatenate(cols, axis=1) + b_ref[...])

    r = gate(wa_ref, ba_ref)
    ig = gate(wx_ref, bx_ref)
    nl = -lam_ref[...]
    z = jnp.exp(-jnp.abs(nl))
    w1 = 1.0 + z
    log1p_z = jnp.where(w1 == 1.0, z, jnp.log(w1) * (z / jnp.where(w1 == 1.0, 1.0, w1 - 1.0)))
    softplus = jnp.maximum(nl, 0.0) + log1p_z
    log_a = -LRU_C * r * softplus
    a = jnp.exp(log_a)
    b = jnp.sqrt(1.0 - a * a) * (ig * xc2)

    a = a.reshape(bb * tt // 8, 8, D_RNN)
    b = b.reshape(bb * tt // 8, 8, D_RNN)
    row8 = lax.broadcasted_iota(jnp.int32, a.shape, 1)
    for d in (1, 2, 4):
        a_s = pltpu.roll(a, d, 1)
        b_s = pltpu.roll(b, d, 1)
        keep = row8 >= d
        b = jnp.where(keep, a * b_s + b, b)
        a = jnp.where(keep, a * a_s, a)

    a3 = a.reshape(bb, tt, D_RNN)
    b3 = b.reshape(bb, tt, D_RNN)
    carry = carry_h[...]
    groups = []
    for g in range(tt // 8):
        hg = a3[:, 8 * g:8 * g + 8] * carry + b3[:, 8 * g:8 * g + 8]
        carry = hg[:, 7:8]
        groups.append(hg)
    hs = groups[0] if len(groups) == 1 else jnp.concatenate(groups, axis=1)
    carry_h[...] = carry
    ho_ref[...] = carry
    y_ref[...] = (hs * _gelu_tanh(yb)).astype(y_ref.dtype)


def rglru(u, conv8, hprev, cw, cb, wa, ba, wx, bx, lam):
    bn, s, _ = u.shape
    if s >= 256:
        bb, tt = 1, 256
    else:
        bb, tt = min(bn, 512 // s), s
    assert s % tt == 0 and bn % bb == 0 and tt % 8 == 0
    vec = lambda a: a.reshape(1, D_RNN)
    c2 = lambda i, j: (0, 0)
    c3 = lambda i, j: (0, 0, 0)
    return pl.pallas_call(
        functools.partial(_rglru_body, bb=bb, tt=tt),
        grid=(bn // bb, s // tt),
        in_specs=[pl.BlockSpec((bb, tt, 2 * D_RNN), lambda i, j: (i, j, 0)),
                  pl.BlockSpec((bb, 8, D_RNN), lambda i, j: (i, 0, 0)),
                  pl.BlockSpec((bb, 1, D_RNN), lambda i, j: (i, 0, 0)),
                  pl.BlockSpec((CONV_W, D_RNN), c2),
                  pl.BlockSpec((1, D_RNN), c2),
                  pl.BlockSpec((LRU_BLOCKS, LRU_BW, LRU_BW), c3),
                  pl.BlockSpec((1, D_RNN), c2),
                  pl.BlockSpec((LRU_BLOCKS, LRU_BW, LRU_BW), c3),
                  pl.BlockSpec((1, D_RNN), c2),
                  pl.BlockSpec((1, D_RNN), c2)],
        out_specs=[pl.BlockSpec((bb, tt, D_RNN), lambda i, j: (i, j, 0)),
                   pl.BlockSpec((bb, 8, D_RNN), lambda i, j: (i, 0, 0)),
                   pl.BlockSpec((bb, 1, D_RNN), lambda i, j: (i, 0, 0))],
        out_shape=[jax.ShapeDtypeStruct((bn, s, D_RNN), BF16),
                   jax.ShapeDtypeStruct((bn, 8, D_RNN), F32),
                   jax.ShapeDtypeStruct((bn, 1, D_RNN), F32)],
        scratch_shapes=[pltpu.VMEM((bb, 1, D_RNN), F32), pltpu.VMEM((bb, 8, D_RNN), F32)],
        compiler_params=_cparams("arbitrary", "arbitrary"),
        name="rglru",
    )(u, conv8, hprev, cw, vec(cb), wa, vec(ba), wx, vec(bx), vec(lam))


def _bias_body(tb_ref, d_ref, o_ref):
    h = pl.program_id(0)
    d = d_ref[...]
    dc = jnp.maximum(d, 0)
    max_exact = N_BUCKETS // 2
    scaled = jnp.log(jnp.maximum(dc, 1).astype(F32) / max_exact) / math.log(MAX_DIST / max_exact)
    large = jnp.minimum(max_exact + (scaled * (N_BUCKETS - max_exact)).astype(jnp.int32), N_BUCKETS - 1)
    bucket = jnp.where(dc < max_exact, dc, large)
    acc = jnp.full(d.shape, tb_ref[0, h], F32)
    for k in range(1, N_BUCKETS):
        acc = jnp.where(bucket == k, tb_ref[k, h], acc)
    o_ref[...] = jnp.where(d < 0, NEG, acc)


def rel_bias_tiles(table, dist):
    rows, cols = dist.shape
    return pl.pallas_call(
        _bias_body,
        grid=(N_HEADS,),
        in_specs=[pl.BlockSpec(memory_space=pltpu.SMEM), pl.BlockSpec((rows, cols), lambda h: (0, 0))],
        out_specs=pl.BlockSpec((None, rows, cols), lambda h: (h, 0, 0)),
        out_shape=jax.ShapeDtypeStruct((N_HEADS, rows, cols), F32),
        compiler_params=_cparams("arbitrary"),
        name="rel_bias_tiles",
    )(table, dist)


def _reorder_emit(sources, stage, o_ref):
    n_col = stage.shape[0]
    row = 0
    for src in sources:
        n = src.shape[0]
        for c in range(n_col):
            stage[c, row:row + n, :] = src[:, c * LANES:(c + 1) * LANES]
        row += n
    n16 = row // CMP_STRIDE
    for r in range(CMP_STRIDE):
        x = jnp.concatenate([stage[c, pl.ds(r, n16, stride=CMP_STRIDE), :] for c in range(n_col)], axis=1)
        o_ref[r] = x.astype(o_ref.dtype)


def _reorder_dense_body(x_ref, o_ref, stage):
    _reorder_emit([x_ref], stage, o_ref)


def _reorder_paged_body(pt_ref, *refs, n_pages):
    del pt_ref
    _reorder_emit(refs[:n_pages], refs[n_pages + 1], refs[n_pages])


def reorder_dense(kv):
    bn, s, _ = kv.shape
    n16 = s // CMP_STRIDE
    return pl.pallas_call(
        _reorder_dense_body,
        grid=(bn,),
        in_specs=[pl.BlockSpec((None, s, KV_PAIR_W), lambda b: (b, 0, 0))],
        out_specs=pl.BlockSpec((None, CMP_STRIDE, n16, KV_PAIR_W), lambda b: (b, 0, 0, 0)),
        out_shape=jax.ShapeDtypeStruct((bn, CMP_STRIDE, n16, KV_PAIR_W), BF16),
        scratch_shapes=[pltpu.VMEM((KV_PAIR_W // LANES, s, LANES), F32)],
        compiler_params=_cparams("arbitrary"),
        name="reorder_dense",
    )(kv)


def reorder_paged(cache, page_table):
    bn, n_pages = page_table.shape
    page = cache.shape[1]
    n16 = n_pages * page // CMP_STRIDE
    page_specs = [pl.BlockSpec((None, page, KV_PAIR_W), functools.partial(lambda b, pt, k: (pt[b * n_pages + k], 0, 0), k=k))
                  for k in range(n_pages)]
    return pl.pallas_call(
        functools.partial(_reorder_paged_body, n_pages=n_pages),
        grid_spec=pltpu.PrefetchScalarGridSpec(
            num_scalar_prefetch=1, grid=(bn,),
            in_specs=page_specs,
            out_specs=pl.BlockSpec((None, CMP_STRIDE, n16, KV_PAIR_W), lambda b, pt: (b, 0, 0, 0)),
            scratch_shapes=[pltpu.VMEM((KV_PAIR_W // LANES, n_pages * page, LANES), F32)]),
        out_shape=jax.ShapeDtypeStruct((bn, CMP_STRIDE, n16, KV_PAIR_W), BF16),
        compiler_params=_cparams("arbitrary"),
        name="reorder_paged",
    )(page_table.reshape(-1), *([cache] * n_pages))


def _compress_body(x_ref, w1_ref, c0_ref, w2_ref, o_ref, acc, *, rows):
    r = pl.program_id(2)

    @pl.when(r == 0)
    def _():
        acc[...] = jnp.zeros_like(acc)

    a = acc[...]
    for k in range(CMP_RSTEP):
        a = a + _dot(x_ref[:, k].reshape(rows, STREAM_W), w1_ref[k])
    acc[...] = a

    @pl.when(r == CMP_STRIDE // CMP_RSTEP - 1)
    def _():
        half = acc.shape[1] // 2
        hidden = _gelu_tanh(acc[:, :half] + pltpu.roll(acc[:, half:], rows - 1, 0) + c0_ref[...])
        o_ref[...] = _dot(hidden.astype(BF16), w2_ref[...]).reshape(o_ref.shape).astype(o_ref.dtype)


def compress(x, w1cat, c0, w2bd, sb):
    bn, _, n16, _ = x.shape
    assert bn % sb == 0
    rows = sb * n16
    nh = w1cat.shape[-1]
    return pl.pallas_call(
        functools.partial(_compress_body, rows=rows),
        grid=(bn // sb, 2, CMP_STRIDE // CMP_RSTEP),
        in_specs=[pl.BlockSpec((sb, CMP_RSTEP, n16, STREAM_W), lambda g, s, r: (g, r, 0, s)),
                  pl.BlockSpec((None, CMP_RSTEP, STREAM_W, nh), lambda g, s, r: (s, r, 0, 0)),
                  pl.BlockSpec((None, 1, nh // 2), lambda g, s, r: (s, 0, 0)),
                  pl.BlockSpec((None, nh // 2, STREAM_W), lambda g, s, r: (s, 0, 0))],
        out_specs=pl.BlockSpec((sb, n16, STREAM_W), lambda g, s, r: (g, 0, s)),
        out_shape=jax.ShapeDtypeStruct((bn, n16, KV_PAIR_W), BF16),
        scratch_shapes=[pltpu.VMEM((rows, nh), F32)],
        compiler_params=_cparams("arbitrary", "arbitrary", "arbitrary"),
        name="compress",
    )(x, w1cat, c0, w2bd)


def _assemble_body(pt_ref, *refs, n_pages, page, s_new, win_len):
    del pt_ref
    pages = refs[:n_pages]
    new_slc_ref, new_win_ref, cwin_ref, slc_ref, win_ref = refs[n_pages:]
    for k in range(n_pages):
        slc_ref[k * page:(k + 1) * page, :] = pages[k][...].astype(slc_ref.dtype)
    pad = jnp.zeros((SLC_LEN - s_new, KV_PAIR_W), F32)
    slc_ref[n_pages * page:n_pages * page + SLC_LEN, :] = jnp.concatenate(
        [new_slc_ref[...], pad], axis=0).astype(slc_ref.dtype)
    win_ref[0:win_len, :] = cwin_ref[...].astype(win_ref.dtype)
    win_ref[win_len:win_len + SLC_LEN, :] = jnp.concatenate([new_win_ref[...], pad], axis=0).astype(win_ref.dtype)


def assemble_sample_kv(cache, page_table, kv_new, cache_win):
    bn, n_pages = page_table.shape
    page = cache.shape[1]
    s_new = kv_new.shape[1]
    win_len = cache_win.shape[1]
    past = n_pages * page
    page_specs = [pl.BlockSpec((None, page, KV_PAIR_W), functools.partial(lambda b, pt, k: (pt[b * n_pages + k], 0, 1), k=k))
                  for k in range(n_pages)]
    return pl.pallas_call(
        functools.partial(_assemble_body, n_pages=n_pages, page=page, s_new=s_new, win_len=win_len),
        grid_spec=pltpu.PrefetchScalarGridSpec(
            num_scalar_prefetch=1, grid=(bn,),
            in_specs=page_specs + [pl.BlockSpec((None, s_new, KV_PAIR_W), lambda b, pt: (b, 0, 1)),
                                   pl.BlockSpec((None, s_new, KV_PAIR_W), lambda b, pt: (b, 0, 2)),
                                   pl.BlockSpec((None, win_len, KV_PAIR_W), lambda b, pt: (b, 0, 0))],
            out_specs=[pl.BlockSpec((None, past + SLC_LEN, KV_PAIR_W), lambda b, pt: (b, 0, 0)),
                       pl.BlockSpec((None, win_len + SLC_LEN, KV_PAIR_W), lambda b, pt: (b, 0, 0))]),
        out_shape=[jax.ShapeDtypeStruct((bn, past + SLC_LEN, KV_PAIR_W), BF16),
                   jax.ShapeDtypeStruct((bn, win_len + SLC_LEN, KV_PAIR_W), BF16)],
        compiler_params=_cparams("arbitrary"),
        name="assemble_sample_kv",
    )(page_table.reshape(-1), *([cache] * n_pages), kv_new, kv_new, cache_win)


def _value_heads(vv):
    own_a = lax.broadcasted_iota(jnp.int32, vv.shape, 1) < HEAD_DIM
    return jnp.where(own_a, vv, 1.0), jnp.where(own_a, 1.0, vv)


def _softmax_update(state, s, vv, rr):
    smax = jnp.max(s, axis=-1, keepdims=True)
    mn = smax if state is None else jnp.maximum(state[0], smax)
    e = jnp.exp2(s - mn).astype(BF16)
    va, vb = _value_heads(vv)
    pv = jnp.concatenate([_dot(e[:rr], va), _dot(e[rr:], vb)], axis=0)
    return mn, (pv if state is None else jnp.exp2(state[0] - mn) * state[1] + pv)


def _softmax_finish(state):
    _, acc = state
    return acc / pltpu.roll(acc, HEAD_DIM, 1)


def _attn_seq(qi, q_ref, gl_ref, kc_ref, vc_ref, ks_ref, vs_ref, kw_ref, vw_ref, augk_ref, augw_ref,
              ovt_ref, cb_ref, nb_ref, wb_ref, augc_ref, ex_ref, o_ref,
              *, qb, n_slc, top_n, pad_s, t0, multi_block, n_streams):
    rr = GQA * qb
    r2 = 2 * rr
    pairs = range(N_PAIRS)
    is_a = lax.broadcasted_iota(jnp.int32, (rr, LANES), 1) < HEAD_DIM
    if multi_block:
        cur = t0 // SLC_LEN + qi
        win_start = pl.multiple_of(qi * qb, SLC_LEN)
    else:
        cur = t0 // SLC_LEN
        win_start = 0

    def cols(ref, p, start, size):
        return ref[pl.ds(start, size), p * LANES:(p + 1) * LANES]

    def init():
        return jnp.full((r2, 1), -jnp.inf, F32), jnp.zeros((r2, LANES), F32)

    wlen = WINDOW + SLC_LEN
    augw = augw_ref[pl.ds(win_start, wlen), :]
    glt = jnp.concatenate([gl_ref[...]] * GQA, axis=0)
    slab = lax.broadcasted_iota(jnp.int32, (rr, LANES), 0) // qb
    gsrc = jnp.concatenate([jnp.where(slab == g, glt, 0.0) for g in range(GQA)], axis=1)
    ghi = gsrc.astype(BF16)
    glo = (gsrc - ghi.astype(F32)).astype(BF16)

    q2, s_win, s_cmp, o_win, gates, o_cmp, imp_t, partial = [], [], [], [], [], [], [], []
    for p in pairs:
        qfull = q_ref[:, p * GQA * LANES:(p + 1) * GQA * LANES]
        qs = jnp.concatenate([qfull[:, g * LANES:(g + 1) * LANES] for g in range(GQA)], axis=0)
        qs = qs * (HEAD_DIM ** -0.5 * LOG2E)
        q2.append(jnp.concatenate([jnp.where(is_a, qs, 0.0), jnp.where(is_a, 0.0, qs)], axis=0).astype(BF16))
        s_cmp.append(_dot_nt(q2[p], kc_ref[:, p * LANES:(p + 1) * LANES]) + cb_ref[p])
    for p in pairs:
        lhs_win = jnp.concatenate([q2[p], augc_ref[p, 1].astype(BF16)], axis=1)
        kk = jnp.concatenate([cols(kw_ref, p, win_start, wlen), augw], axis=1)
        s_win.append(_dot_nt(lhs_win, kk) + wb_ref[p])

    for p in pairs:
        s = s_cmp[p]
        m = jnp.max(s, axis=-1, keepdims=True)
        e = jnp.exp2(s - m)
        pr = e / jnp.sum(e, axis=-1, keepdims=True)
        pr = jnp.where(m > 0.1 * NEG, pr, 0.0)
        o_cmp.append(_dot(pr.astype(BF16), vc_ref[:, p * LANES:(p + 1) * LANES]))

        pcat = jnp.concatenate(
            [jnp.concatenate([pr[x * rr + g * qb:x * rr + (g + 1) * qb] for g in range(GQA)], axis=1)
             for x in range(2)], axis=0)
        if 2 * qb < LANES:
            pcat = jnp.concatenate([pcat, jnp.zeros((LANES - 2 * qb, pcat.shape[1]), F32)], axis=0)
        imp_t.append(_dot_nt(ovt_ref[...], pcat.astype(BF16)))

    def ranked():
        jr = lax.broadcasted_iota(jnp.int32, (N_BLOCK_LANES, LANES), 0)
        forced = (jr == 0) | (jr == cur) | (jr == cur - 1)
        sub8 = lax.broadcasted_iota(jnp.int32, (8, LANES), 0)
        res = []
        for p in pairs:
            v = jnp.where(forced, FORCE, jnp.where((jr > cur) | (jr >= n_slc), NEG, imp_t[p][:N_BLOCK_LANES]))
            groups = [v[8 * r:8 * r + 8] for r in range(N_BLOCK_LANES // 8)]
            ranks = [jnp.zeros((8, LANES), F32) for _ in groups]
            for k in range(n_slc):
                vk = v[k:k + 1, :]
                for r, vr in enumerate(groups):
                    if 8 * r > k:
                        ranks[r] = ranks[r] + jnp.where(vk >= vr, 1.0, 0.0)
                    elif 8 * r + 7 < k:
                        ranks[r] = ranks[r] + jnp.where(vk > vr, 1.0, 0.0)
                    else:
                        ranks[r] = ranks[r] + jnp.where(sub8 > k - 8 * r, jnp.where(vk >= vr, 1.0, 0.0),
                                                        jnp.where(vk > vr, 1.0, 0.0))
            res.append(jnp.where(jnp.concatenate(ranks, axis=0) < top_n, 0.0, 1.0))
        return tuple(res)

    notsel_t = ranked()

    for p in pairs:
        o_win.append(_softmax_finish(_softmax_update(None, s_win[p], cols(vw_ref, p, win_start, wlen), rr)))
        gates.append(jax.nn.sigmoid(_dot(ghi, ex_ref[p]) + _dot(glo, ex_ref[p])))
        partial.append(gates[p][:, :LANES] * jnp.where(is_a, o_cmp[p][:rr], o_cmp[p][rr:])
                       + gates[p][:, 2 * LANES:] * jnp.where(is_a, o_win[p][:rr], o_win[p][rr:]))

    st = pad_s + (cur - 2) * SLC_LEN
    if multi_block:
        st = pl.multiple_of(st, SLC_LEN)
    augk_near = augk_ref[pl.ds(st, 3 * SLC_LEN), :]
    lane2 = lax.broadcasted_iota(jnp.int32, (r2, LANES), 1)
    lhs_far, s_near = [], []
    for p in pairs:
        notsel = jnp.concatenate([notsel_t[p], jnp.zeros((LANES - N_BLOCK_LANES, LANES), F32)], axis=0).T
        ns = jnp.concatenate([notsel[x * qb:(x + 1) * qb] for x in range(2) for _ in range(GQA)], axis=0)
        ns_far = jnp.where((lane2 >= cur - 2) & (lane2 < N_BLOCK_LANES), 1.0, ns)
        lhs_far.append(jnp.concatenate([q2[p], (ns_far + augc_ref[p, 0]).astype(BF16)], axis=1))
        lhs_near = jnp.concatenate([q2[p], (ns + augc_ref[p, 1]).astype(BF16)], axis=1)
        kk = jnp.concatenate([cols(ks_ref, p, st, 3 * SLC_LEN), augk_near], axis=1)
        s_near.append(_dot_nt(lhs_near, kk) + nb_ref[p])
    near = [_softmax_update(None, s_near[p], cols(vs_ref, p, st, 3 * SLC_LEN), rr) for p in pairs]

    def far_start(it, k):
        return pl.multiple_of(pad_s + (it * n_streams + k) * FAR_CHUNK, math.gcd(pad_s, FAR_CHUNK))

    def far_scores(it):
        scores = []
        for k in range(n_streams):
            st = far_start(it, k)
            augk = augk_ref[pl.ds(st, FAR_CHUNK), :]
            for p in pairs:
                kk = jnp.concatenate([cols(ks_ref, p, st, FAR_CHUNK), augk], axis=1)
                scores.append(_dot_nt(lhs_far[p], kk))
        return tuple(scores)

    def far_update(it, states, scores):
        return tuple(_softmax_update(states[k * N_PAIRS + p], scores[k * N_PAIRS + p],
                                     cols(vs_ref, p, far_start(it, k), FAR_CHUNK), rr)
                     for k in range(n_streams) for p in pairs)

    per_iter = n_streams * (FAR_CHUNK // SLC_LEN)
    if multi_block:
        n_iter = (jnp.maximum(cur - 2, 0) + per_iter - 1) // per_iter
        states = tuple(near) + tuple(init() for _ in range((n_streams - 1) * N_PAIRS))
        states = lax.fori_loop(0, n_iter, lambda it, states: far_update(it, states, far_scores(it)), states)
    else:
        states = tuple(near) + (None,) * ((n_streams - 1) * N_PAIRS)
        for it in range((max(cur - 2, 0) + per_iter - 1) // per_iter):
            states = far_update(it, states, far_scores(it))

    outs = []
    for p in pairs:
        mine = tuple(st for st in states[p::N_PAIRS] if st is not None)
        if len(mine) == 1:
            o_slc = _softmax_finish(mine[0])
        else:
            m_all = functools.reduce(jnp.maximum, [m for m, _ in mine])
            o_slc = _softmax_finish((m_all, sum(acc * jnp.exp2(m - m_all) for m, acc in mine)))
        out = partial[p] + gates[p][:, LANES:2 * LANES] * jnp.where(is_a, o_slc[:rr], o_slc[rr:])
        outs += [out[g * qb:(g + 1) * qb] for g in range(GQA)]
    o_ref[...] = jnp.concatenate(outs, axis=1).astype(o_ref.dtype)


def _attn_body(*refs, sbq, **kw):
    qi = pl.program_id(1)
    per_seq, shared = refs[:8], refs[8:16]
    for i in range(sbq):
        _attn_seq(qi, *[r.at[i] for r in per_seq], *shared, refs[16].at[i], **kw)


def _cmp_bias_body(cbe_ref, o_ref, *, nc, step):
    tau = pl.program_id(0) * step
    for p in range(N_PAIRS):
        o_ref[p] = pltpu.roll(cbe_ref[p], tau, 1)[:, nc:]


def cmp_bias_per_block(cbe, nqb, step):
    n_p, rows, nc2 = cbe.shape
    nc = nc2 // 2
    return pl.pallas_call(
        functools.partial(_cmp_bias_body, nc=nc, step=step),
        grid=(nqb,),
        in_specs=[pl.BlockSpec(cbe.shape, lambda i: (0, 0, 0))],
        out_specs=pl.BlockSpec((None, n_p, rows, nc), lambda i: (i, 0, 0, 0)),
        out_shape=jax.ShapeDtypeStruct((nqb, n_p, rows, nc), F32),
        compiler_params=_cparams("arbitrary"),
        name="cmp_bias_per_block",
    )(cbe)


def nsa_attention(u, cmp_kv, slc_kv, slc_cols, win_kv, win_cols, augk, augw, ovt, tiles, ex,
                  *, qb, n_slc, pad_s, t0, sbq):
    bn, s, _ = u.shape
    nqb = s // qb
    nc = cmp_kv.shape[1]
    r2 = 2 * GQA * qb
    cb, nb, wb, augc = tiles
    srows = slc_kv.shape[1]
    wrows = win_kv.shape[1]
    kcol, vcol = slc_cols
    kwcol, vwcol = win_cols
    assert qb == SLC_LEN or nqb == 1
    assert t0 % SLC_LEN == 0 and bn % sbq == 0
    per_chunk = FAR_CHUNK // SLC_LEN
    max_chunks = -(-max(n_slc - 3, 0) // per_chunk)
    n_streams = min(1, max_chunks) if nqb > 1 else max_chunks
    n_streams = max(n_streams, 1)
    assert pad_s + -(-max_chunks // n_streams) * n_streams * FAR_CHUNK <= srows
    body = functools.partial(_attn_body, sbq=sbq, qb=qb, n_slc=n_slc, top_n=min(N_SELECT, n_slc),
                             pad_s=pad_s, t0=t0, multi_block=nqb > 1, n_streams=n_streams)
    whole = lambda a: pl.BlockSpec(a.shape, lambda b, i: (0,) * a.ndim)
    kvw = N_PAIRS * LANES
    nq = N_HEADS * HEAD_DIM
    return pl.pallas_call(
        body,
        grid=(bn // sbq, nqb),
        in_specs=[pl.BlockSpec((sbq, qb, nq), lambda b, i: (b, i, 0)),
                  pl.BlockSpec((sbq, qb, LANES), lambda b, i: (b, i, nq // LANES)),
                  pl.BlockSpec((sbq, nc, kvw), lambda b, i: (b, 0, 0)),
                  pl.BlockSpec((sbq, nc, kvw), lambda b, i: (b, 0, 1)),
                  pl.BlockSpec((sbq, srows, kvw), lambda b, i: (b, 0, kcol)),
                  pl.BlockSpec((sbq, srows, kvw), lambda b, i: (b, 0, vcol)),
                  pl.BlockSpec((sbq, wrows, kvw), lambda b, i: (b, 0, kwcol)),
                  pl.BlockSpec((sbq, wrows, kvw), lambda b, i: (b, 0, vwcol)),
                  whole(augk), whole(augw), whole(ovt),
                  pl.BlockSpec((None, N_PAIRS, r2, nc), lambda b, i: (i, 0, 0, 0)),
                  whole(nb), whole(wb), whole(augc), whole(ex)],
        out_specs=pl.BlockSpec((sbq, qb, nq), lambda b, i: (b, i, 0)),
        out_shape=jax.ShapeDtypeStruct((bn, s, nq), F32),
        compiler_params=_cparams("arbitrary", "arbitrary"),
        name="nsa_attention",
    )(u, u, cmp_kv, cmp_kv, slc_kv, slc_kv, win_kv, win_kv, augk, augw, ovt, cb, nb, wb, augc, ex)


def _head_perm():
    idx = np.empty((N_PAIRS, GQA, 2, HEAD_DIM), np.int32)
    for p in range(N_PAIRS):
        for g in range(GQA):
            for half in range(2):
                head = (2 * p + half) * GQA + g
                idx[p, g, half] = head * HEAD_DIM + np.arange(HEAD_DIM)
    return idx.reshape(-1)


def _gate_expand():
    ex = np.zeros((N_PAIRS, GQA * LANES, 3 * LANES), np.float32)
    for p in range(N_PAIRS):
        for g in range(GQA):
            for half in range(2):
                head = (2 * p + half) * GQA + g
                for br in range(3):
                    ex[p, g * LANES + head * 3 + br, br * LANES + half * HEAD_DIM:br * LANES + (half + 1) * HEAD_DIM] = 1.0
    return ex


def _overlap_t(nc, n_cmp, n_slc):
    i = np.arange(nc)[None, :] * CMP_STRIDE
    j = np.arange(LANES)[:, None] * SLC_LEN
    ov = (i < j + SLC_LEN) & (i + CMP_LEN > j) & (np.arange(nc)[None, :] < n_cmp) & (np.arange(LANES)[:, None] < n_slc)
    return np.tile(ov.astype(np.float32), (1, GQA))


def _aug_keys(pad, n_keys):
    a = np.zeros((pad + n_keys, LANES), np.float32)
    a[:pad, AUG_PAD] = NEG
    k = np.arange(n_keys)
    a[pad + k, k // SLC_LEN] = NEG
    a[pad:, AUG_BIAS_HI] = 1.0
    a[pad:, AUG_BIAS_LO] = 1.0
    return a


def _attn_tiles(rel_bias, qb, t0, nc, nqb):
    rolled = nqb > 1
    i = np.arange(qb)[:, None]
    near = np.arange(3 * SLC_LEN)[None, :]
    d_near = SLC_LEN * (2 - near // SLC_LEN) + i - near % SLC_LEN
    c = np.arange(WINDOW + SLC_LEN)[None, :]
    d_win = i + WINDOW - c
    d_win = np.where(d_win < WINDOW, d_win, -1)
    if rolled:
        e = np.arange(2 * nc)[None, :] - nc
        d_cmp = i - (CMP_LEN - 1) - CMP_STRIDE * e
    else:
        d_cmp = t0 + i - CMP_STRIDE * np.arange(nc)[None, :] - (CMP_LEN - 1)
    widths = [d_near.shape[1], d_win.shape[1], d_cmp.shape[1]]
    padded = [-(-w // LANES) * LANES for w in widths]
    dist = np.full((qb, sum(padded)), -1, np.int32)
    off = 0
    offs = []
    for d, w, pw in zip((d_near, d_win, d_cmp), widths, padded):
        dist[:, off:off + w] = np.clip(d, -1, 4 * MAX_DIST)
        offs.append(off)
        off += pw
    tiles = rel_bias_tiles(rel_bias, jnp.asarray(dist))
    r2 = 2 * GQA * qb
    tiles = tiles.reshape(N_PAIRS, r2, dist.shape[1]) * LOG2E
    nb, wb, cbe = (tiles[..., o:o + w] for o, w in zip(offs, widths))
    if rolled:
        cb = cmp_bias_per_block(cbe, nqb, (t0 + qb) // CMP_STRIDE - t0 // CMP_STRIDE)
    else:
        cb = cbe[None]
    far = rel_bias[N_BUCKETS - 1].reshape(N_PAIRS, 2 * GQA) * LOG2E
    hi = far.astype(BF16).astype(F32)
    lo = far - hi
    rows = jnp.repeat(jnp.stack([hi, lo], axis=-1), qb, axis=1)
    augc = jnp.zeros((N_PAIRS, 2, r2, LANES), F32)
    augc = augc.at[:, 0, :, AUG_BIAS_HI].set(rows[..., 0]).at[:, 0, :, AUG_BIAS_LO].set(rows[..., 1])
    augc = augc.at[:, 1, :, AUG_PAD].set(1.0)
    return cb, nb, wb, augc


def _forward(x, ada, ada_kv, conv_state, h_state, past, w, shared):
    bn, s, _ = x.shape
    m = bn * s
    xf = x.reshape(m, D_MODEL)
    new_conv, new_h = [], []
    h = normmod(xf, w["norm_g"][0, 0], ada[0], 0, 1, bn, s)
    kv = None
    attn_args = None
    for l in range(DEPTH):
        if l < N_A_LAYERS:
            u = mm_plain(h, (w["w_in_a"], l)).reshape(bn, s, 2 * D_RNN)
            conv8 = jnp.pad(conv_state[l], ((0, 0), (8 - (CONV_W - 1), 0), (0, 0)))
            y, conv_o, h_o = rglru(u, conv8, h_state[l][:, None, :], w["conv_w"][l], w["conv_b"][l],
                                   w["w_rg_a"][l], w["b_rg_a"][l], w["w_rg_x"][l], w["b_rg_x"][l], w["lru_lambda"][l])
            new_conv.append(conv_o[:, 8 - (CONV_W - 1):])
            new_h.append(h_o[:, 0])
            a_in = y.reshape(m, D_RNN)
            w_out = (w["w_out_a"], l)
        else:
            lb = l - N_A_LAYERS
            if lb == 0:
                if past is None:
                    kv, kvb = mm_kv_padded(h_kv, w["w_kv"], bn, s, WINDOW)
                else:
                    kv, kvb = mm_plain(h_kv, w["w_kv"]), None
                kv = kv.reshape(bn, s, 6 * N_KV_HEADS * HEAD_DIM)
                attn_args = _prepare_attention(kv, kvb, past, w, shared, bn, s)
            u = mm_plain(h, (w["w_in_b"], lb)).reshape(bn, s, -1)
            a_in = nsa_attention(u, *attn_args[0], **attn_args[1]).reshape(m, N_HEADS * HEAD_DIM)
            w_out = (w["w_out_b"], lb)
        x_new, (h2,) = mm_resnorm(a_in, w_out, xf, (ada[l], 2), w["norm_g"][l, 1],
                                  [(w["norm_g"][l, 2], ada[l], 3, ada[l], 4)], bn, s)
        f = mm_swiglu(h2, (w["w_ffn_in"], l))
        heads = []
        if l + 1 < DEPTH:
            heads.append((w["norm_g"][l + 1, 0], ada[l + 1], 0, ada[l + 1], 1))
        if l + 1 == N_A_LAYERS:
            heads.append((w["norm_kv"], ada_kv, 0, ada_kv, 1))
        xf, hs = mm_resnorm(f, (w["w_ffn_out"], l), x_new, (ada[l], 5), w["norm_g"][l, 3], heads, bn, s)
        if heads:
            h = hs[0]
        if l + 1 == N_A_LAYERS:
            h_kv = hs[1]
    kv_rows = kv[:, :, :4 * N_KV_HEADS * HEAD_DIM].reshape(bn, s, 4, N_KV_HEADS, HEAD_DIM)
    new_win = kv[:, :, 4 * N_KV_HEADS * HEAD_DIM:].reshape(bn, s, 2, N_KV_HEADS, HEAD_DIM)
    if past is None:
        win_state = new_win[:, -min(WINDOW, s):]
    else:
        cache_win = past[2]
        win_state = jnp.concatenate([cache_win, new_win], axis=1)[:, -cache_win.shape[1]:]
    return xf.reshape(bn, s, D_MODEL), kv_rows, win_state, jnp.stack(new_conv), jnp.stack(new_h)


def _prepare_attention(kv, kvb, past, w, shared, bn, s):
    rel_bias = w["rel_bias"]
    if past is None:
        t0, qb, pad = 0, Q_BLOCK, WINDOW
        n_keys = s
        xr = reorder_dense(kv)
        slc_kv = win_kv = kvb
        slc_cols, win_cols = (2, 3), (4, 5)
        augk = jnp.asarray(_aug_keys(pad, n_keys), BF16)
        augw = augk
        sb = bn
    else:
        cache, page_table, cache_win = past
        n_pool, page = cache.shape[:2]
        cache2 = cache.reshape(n_pool, page, -1)
        t0 = page_table.shape[1] * page
        qb, pad = s, 0
        n_keys = t0 + SLC_LEN
        assert cache_win.shape[1] == WINDOW and t0 >= WINDOW
        xr = reorder_paged(cache2, page_table)
        slc_kv, win_kv = assemble_sample_kv(cache2, page_table, kv, cache_win.reshape(bn, WINDOW, -1))
        slc_cols, win_cols = (0, 1), (0, 1)
        augk = jnp.asarray(_aug_keys(0, n_keys), BF16)
        augw = jnp.zeros((win_kv.shape[1], LANES), BF16)
        sb = 8
    n16 = (t0 + s) // CMP_STRIDE
    n_cmp = n16 - CMP_LEN // CMP_STRIDE + 1
    nc = xr.shape[2]
    assert nc >= n_cmp and nc % LANES == 0
    n_slc = -(-(t0 + s) // SLC_LEN)
    assert n_slc <= N_BLOCK_LANES
    cmp_kv = compress(xr, shared["w1cat"], shared["c0"], shared["w2bd"], sb)
    nqb = s // qb
    assert nqb == 1 or t0 == 0
    tiles = _attn_tiles(rel_bias, qb, t0, nc, nqb)
    ovt = jnp.asarray(_overlap_t(nc, n_cmp, n_slc), BF16)
    args = (cmp_kv, slc_kv, slc_cols, win_kv, win_cols, augk, augw, ovt, tiles, shared["ex"])
    return args, dict(qb=qb, n_slc=n_slc, pad_s=pad, t0=t0, sbq=1 if nqb > 1 else min(bn, 2))


def kernel(x_prompt, x_sample, c_prompt, c_sample, cache_kv, cache_win, state_conv, state_h, page_table, w_ada, b_ada, norm_g, w_in_a, conv_w, conv_b, w_rg_a, b_rg_a, w_rg_x, b_rg_x, lru_lambda, w_out_a, w_ada_kv, b_ada_kv, norm_kv, w_kv, cmp_pos, cmp_w1, cmp_w2, w_in_b, w_out_b, rel_bias, w_ffn_in, w_ffn_out):
    bp = x_prompt.shape[0]
    bs = x_sample.shape[0]

    perm = _head_perm()
    n_q = N_HEADS * HEAD_DIM
    w_in_b_p = jnp.concatenate([w_in_b[:, :, :n_q][:, :, perm], w_in_b[:, :, n_q:],
                                jnp.zeros(w_in_b.shape[:2] + (LANES - 3 * N_HEADS,), F32)], axis=-1)
    w = dict(norm_g=norm_g, norm_kv=norm_kv, conv_w=conv_w, conv_b=conv_b, b_rg_a=b_rg_a, b_rg_x=b_rg_x,
             lru_lambda=lru_lambda, rel_bias=rel_bias,
             w_in_a=w_in_a, w_rg_a=w_rg_a.astype(BF16), w_rg_x=w_rg_x.astype(BF16),
             w_out_a=w_out_a, w_kv=w_kv, w_in_b=w_in_b_p,
             w_out_b=w_out_b[:, perm, :], w_ffn_in=w_ffn_in, w_ffn_out=w_ffn_out)

    eye = jnp.eye(N_KV_HEADS, dtype=F32)
    w1r = cmp_w1.reshape(2, CMP_LEN, HEAD_DIM, CMP_HIDDEN)
    w1bd = jnp.einsum("hk,srdj->srhdkj", eye, w1r).reshape(2, CMP_LEN, N_KV_HEADS * HEAD_DIM, N_KV_HEADS * CMP_HIDDEN)
    w2bd = jnp.einsum("hk,sjd->shjkd", eye, cmp_w2).reshape(2, N_KV_HEADS * CMP_HIDDEN, N_KV_HEADS * HEAD_DIM)
    w1cat = jnp.concatenate([w1bd[:, :CMP_STRIDE], w1bd[:, CMP_STRIDE:]], axis=-1).astype(BF16)
    pos_rows = jnp.pad(cmp_pos.reshape(2, 1, CMP_LEN * HEAD_DIM), ((0, 0), (0, 7), (0, 0)))
    c0 = jnp.stack([mm_plain(pos_rows[st], (cmp_w1, st))[:1] for st in range(2)])
    shared = dict(w1cat=w1cat, c0=jnp.tile(c0, (1, 1, N_KV_HEADS)), w2bd=w2bd.astype(BF16),
                  ex=jnp.asarray(_gate_expand(), BF16))

    n_c = bp + bs
    c_all = jnp.concatenate([c_prompt, c_sample, jnp.zeros((-n_c % 8, D_MODEL), F32)], axis=0)
    ada_all = [mm_plain(c_all, (w_ada, l), bias=b_ada[l], act="silu", tn=2048) for l in range(DEPTH)]
    ada_kv_all = mm_plain(c_all, w_ada_kv, bias=b_ada_kv, act="silu", tn=2048)
    ada_p = [a[:bp, None, :] for a in ada_all]
    ada_s = [a[bp:n_c, None, :] for a in ada_all]

    zero_conv = jnp.zeros((N_A_LAYERS, bp, CONV_W - 1, D_RNN), F32)
    zero_h = jnp.zeros((N_A_LAYERS, bp, D_RNN), F32)
    y_p, kv_p, win_p, conv_p, h_p = _forward(x_prompt, ada_p, ada_kv_all[:bp, None, :], zero_conv, zero_h, None, w, shared)
    y_s, kv_s, win_s, conv_s, h_s = _forward(x_sample, ada_s, ada_kv_all[bp:n_c, None, :], state_conv, state_h,
                                             (cache_kv, page_table, cache_win), w, shared)
    return (y_p, y_s, kv_p, kv_s, win_p, win_s, conv_p, conv_s, h_p, h_s)
```

```python
import functools
import math

import numpy as np
import jax
import jax.numpy as jnp
from jax import lax
from jax.experimental import pallas as pl
from jax.experimental.pallas import tpu as pltpu

D_MODEL = 1024
DEPTH = 4
N_A_LAYERS = 2
D_RNN = 1024
LRU_BLOCKS = 8
LRU_BW = 128
CONV_W = 4
LRU_C = 8.0
N_HEADS = 16
HEAD_DIM = 64
N_KV_HEADS = 4
GQA = 4
CMP_LEN = 32
CMP_STRIDE = 16
CMP_HIDDEN = 128
SLC_LEN = 64
N_SELECT = 16
WINDOW = 512
Q_BLOCK = 64
N_BUCKETS = 32
MAX_DIST = 128
EPS = 1e-6
NEG = -1e30
FORCE = 1e9

LANES = 128
N_PAIRS = N_KV_HEADS // 2
STREAM_W = N_KV_HEADS * HEAD_DIM
KV_PAIR_W = 2 * STREAM_W
AUG_BIAS_HI = 64
AUG_BIAS_LO = 65
AUG_PAD = 66
N_BLOCK_LANES = 64
FAR_CHUNK = 1024
CMP_RSTEP = 4
RESNORM_PART = 512
LOG2E = 1.4426950408889634
VMEM_LIMIT = 56 * 1024 * 1024

F32 = jnp.float32
BF16 = jnp.bfloat16


def _cparams(*sem):
    return pltpu.CompilerParams(dimension_semantics=sem, vmem_limit_bytes=VMEM_LIMIT)


def _dot(a, b):
    return jnp.dot(a, b, preferred_element_type=F32)


def _dot_nt(a, b):
    return lax.dot_general(a, b, (((1,), (1,)), ((), ())), preferred_element_type=F32)


def _gelu_tanh(x):
    return 0.5 * x * (1.0 + jnp.tanh(math.sqrt(2.0 / math.pi) * (x + 0.044715 * (x * x * x))))


def _rms(x, gain):
    return x * lax.rsqrt(jnp.mean(x * x, axis=-1, keepdims=True) + EPS) * gain


def _rows(vec_ref, bb, ts):
    v = vec_ref[...]
    d = v.shape[-1]
    return jnp.broadcast_to(v, (bb, ts, d)).reshape(bb * ts, d)


def _vec_spec(bb, tiles_per_seq, col, ngrid):
    if ngrid == 1:
        return pl.BlockSpec((bb, 1, D_MODEL), lambda i: (i // tiles_per_seq, 0, col))
    return pl.BlockSpec((bb, 1, D_MODEL), lambda j, i: (i // tiles_per_seq, 0, col))


def _row_tiling(bn, s, tm=512):
    if s >= tm:
        assert s % tm == 0
        return tm, 1, tm, s // tm
    assert tm % s == 0 and s % 8 == 0
    bb = min(bn, tm // s)
    assert bn % bb == 0
    return bb * s, bb, s, 1


def _bf16_weight(w_ref, cache_ref, first_row_tile):
    if cache_ref is None:
        return w_ref[...].astype(BF16)

    @pl.when(first_row_tile)
    def _():
        cache_ref[...] = w_ref[...].astype(BF16)

    return cache_ref[...]


def _weight_cache(w, block, n_row_tiles):
    return [pltpu.VMEM(block, BF16)] if (w.dtype != BF16 and n_row_tiles > 1) else []


def _mm_body(*refs, act, has_bias, cached):
    cache_ref = refs[-1] if cached else None
    refs = refs[:-1] if cached else refs
    if has_bias:
        a_ref, w_ref, b_ref, o_ref = refs
    else:
        a_ref, w_ref, o_ref = refs
    a = a_ref[...]
    if act == "silu":
        a = a.astype(F32)
        a = a * jax.nn.sigmoid(a)
    y = _dot(a.astype(BF16), _bf16_weight(w_ref, cache_ref, pl.program_id(1) == 0))
    if has_bias:
        y = y + b_ref[...]
    o_ref[...] = y.astype(o_ref.dtype)


def _layered(w):
    return w if isinstance(w, tuple) else (w[None], 0)


def mm_plain(a, w, bias=None, act=None, tm=512, tn=None, out_dtype=F32):
    w, layer = _layered(w)
    m, k = a.shape
    n = w.shape[2]
    tm = min(tm, m)
    tn = tn or n
    assert m % tm == 0 and n % tn == 0
    in_specs = [pl.BlockSpec((tm, k), lambda j, i: (i, 0)), pl.BlockSpec((None, k, tn), lambda j, i: (layer, 0, j))]
    args = [a, w]
    if bias is not None:
        in_specs.append(pl.BlockSpec((1, tn), lambda j, i: (0, j)))
        args.append(bias.reshape(1, n))
    cache = _weight_cache(w, (k, tn), m // tm)
    return pl.pallas_call(
        functools.partial(_mm_body, act=act, has_bias=bias is not None, cached=bool(cache)),
        grid=(n // tn, m // tm),
        in_specs=in_specs,
        out_specs=pl.BlockSpec((tm, tn), lambda j, i: (i, j)),
        out_shape=jax.ShapeDtypeStruct((m, n), out_dtype),
        scratch_shapes=cache,
        compiler_params=_cparams("arbitrary", "arbitrary"),
        name="mm_plain",
    )(*args)


def _ada_body(c_ref, w_ref, b_ref, o_ref):
    a = c_ref[...]
    a = a * jax.nn.sigmoid(a)
    o_ref[...] = _dot(a.astype(BF16), w_ref[...].astype(BF16)) + b_ref[...]


def ada_all_layers(c_all, w_ada, b_ada, tn=2048):
    n_layers, k, n = w_ada.shape
    m = c_all.shape[0]
    assert n % tn == 0
    return pl.pallas_call(
        _ada_body,
        grid=(n_layers, n // tn),
        in_specs=[pl.BlockSpec((m, k), lambda l, j: (0, 0)),
                  pl.BlockSpec((None, k, tn), lambda l, j: (l, 0, j)),
                  pl.BlockSpec((None, 1, tn), lambda l, j: (l, 0, j))],
        out_specs=pl.BlockSpec((None, m, tn), lambda l, j: (l, 0, j)),
        out_shape=jax.ShapeDtypeStruct((n_layers, m, n), F32),
        compiler_params=_cparams("arbitrary", "arbitrary"),
        name="ada_all_layers",
    )(c_all, w_ada, b_ada.reshape(n_layers, 1, n))


def _kv_padded_body(a_ref, w_ref, kv_ref, kvb_ref, cache_ref):
    b, t = pl.program_id(0), pl.program_id(1)
    w = _bf16_weight(w_ref, cache_ref, (b == 0) & (t == 0))

    @pl.when(t == 0)
    def _():
        kvb_ref[...] = jnp.zeros_like(kvb_ref)

    @pl.when(t > 0)
    def _():
        y = _dot(a_ref[...], w)
        kv_ref[...] = y
        kvb_ref[...] = y.astype(kvb_ref.dtype)


def mm_kv_padded(a, w, bn, s, pad):
    m, k = a.shape
    n = w.shape[1]
    tm = pad
    assert s % tm == 0
    tps = s // tm
    src = lambda b, t: (b * tps + jnp.maximum(t - 1, 0), 0)
    return pl.pallas_call(
        _kv_padded_body,
        grid=(bn, tps + 1),
        in_specs=[pl.BlockSpec((tm, k), src),
                  pl.BlockSpec((k, n), lambda b, t: (0, 0), pipeline_mode=pl.Buffered(1))],
        out_specs=[pl.BlockSpec((tm, n), src), pl.BlockSpec((None, tm, n), lambda b, t: (b, t, 0))],
        out_shape=[jax.ShapeDtypeStruct((m, n), F32), jax.ShapeDtypeStruct((bn, pad + s, n), BF16)],
        scratch_shapes=[pltpu.VMEM((k, n), BF16)],
        compiler_params=_cparams("arbitrary", "arbitrary"),
        name="mm_kv_padded",
    )(a, w)


def _swiglu_body(h_ref, wg_ref, wu_ref, o_ref, *caches):
    first = pl.program_id(1) == 0
    wg = _bf16_weight(wg_ref, caches[0] if caches else None, first)
    wu = _bf16_weight(wu_ref, caches[1] if caches else None, first)
    tm = h_ref.shape[0]
    n_part = tm // RESNORM_PART if tm % RESNORM_PART == 0 else 1
    rows = tm // n_part
    gu = [(_dot(h_ref[i * rows:(i + 1) * rows, :], wg), _dot(h_ref[i * rows:(i + 1) * rows, :], wu))
          for i in range(n_part)]
    for i, (g, u) in enumerate(gu):
        o_ref[i * rows:(i + 1) * rows, :] = (g * jax.nn.sigmoid(g) * u).astype(o_ref.dtype)


def mm_swiglu(h, w_in, tm=2 * RESNORM_PART, tn=None):
    w_in, layer = _layered(w_in)
    m, k = h.shape
    nf = w_in.shape[2] // 2
    tm = min(tm, m)
    tn = tn or nf // 2
    assert nf % tn == 0 and tn % LANES == 0 and m % tm == 0
    nj = nf // tn
    return pl.pallas_call(
        _swiglu_body,
        grid=(nj, m // tm),
        in_specs=[pl.BlockSpec((tm, k), lambda j, i: (i, 0)),
                  pl.BlockSpec((None, k, tn), lambda j, i: (layer, 0, j), pipeline_mode=pl.Buffered(1)),
                  pl.BlockSpec((None, k, tn), lambda j, i: (layer, 0, j + nj), pipeline_mode=pl.Buffered(1))],
        out_specs=pl.BlockSpec((tm, tn), lambda j, i: (i, j)),
        out_shape=jax.ShapeDtypeStruct((m, nf), BF16),
        scratch_shapes=_weight_cache(w_in, (k, tn), m // tm) * 2,
        compiler_params=_cparams("arbitrary", "arbitrary"),
        name="mm_swiglu",
    )(h, w_in, w_in)


def _normmod_body(x_ref, g_ref, sh_ref, sc_ref, o_ref, *, bb, ts):
    y = _rms(x_ref[...], g_ref[...])
    o_ref[...] = (y * (1.0 + _rows(sc_ref, bb, ts)) + _rows(sh_ref, bb, ts)).astype(o_ref.dtype)


def normmod(x, gain, ada, col_shift, col_scale, bn, s):
    m = x.shape[0]
    tm, bb, ts, tps = _row_tiling(bn, s)
    return pl.pallas_call(
        functools.partial(_normmod_body, bb=bb, ts=ts),
        grid=(m // tm,),
        in_specs=[pl.BlockSpec((tm, D_MODEL), lambda i: (i, 0)),
                  pl.BlockSpec((1, D_MODEL), lambda i: (0, 0)),
                  _vec_spec(bb, tps, col_shift, 1),
                  _vec_spec(bb, tps, col_scale, 1)],
        out_specs=pl.BlockSpec((tm, D_MODEL), lambda i: (i, 0)),
        out_shape=jax.ShapeDtypeStruct((m, D_MODEL), BF16),
        compiler_params=_cparams("arbitrary"),
        name="normmod",
    )(x, gain.reshape(1, D_MODEL), ada, ada)


def _resnorm_body(*refs, bb, ts, n_heads, cached):
    cache_ref = refs[-1] if cached else None
    refs = refs[:-1] if cached else refs
    a_ref, w_ref, x_ref, gate_ref, ngain_ref = refs[:5]
    head_refs = refs[5:5 + 3 * n_heads]
    xo_ref = refs[5 + 3 * n_heads]
    ho_refs = refs[6 + 3 * n_heads:]
    w = _bf16_weight(w_ref, cache_ref, pl.program_id(0) == 0)
    tm = a_ref.shape[0]
    n_part = tm // RESNORM_PART if tm % RESNORM_PART == 0 else 1
    rows = tm // n_part
    ys = [_dot(a_ref[i * rows:(i + 1) * rows, :].astype(BF16), w) for i in range(n_part)]
    if n_part > 1:
        assert bb == 1
        part_rows = lambda ref: _rows(ref, 1, rows)
    else:
        part_rows = lambda ref: _rows(ref, bb, ts)
    gate = part_rows(gate_ref)
    mods = [(head_refs[3 * k][...], part_rows(head_refs[3 * k + 1]), part_rows(head_refs[3 * k + 2]))
            for k in range(n_heads)]
    for i, y in enumerate(ys):
        sl = slice(i * rows, (i + 1) * rows)
        x = x_ref[sl, :] + gate * _rms(y, ngain_ref[...])
        xo_ref[sl, :] = x
        for k, (hg, sh, sc) in enumerate(mods):
            ho_refs[k][sl, :] = (_rms(x, hg) * (1.0 + sc) + sh).astype(ho_refs[k].dtype)


def mm_resnorm(a, w, x, gate, ngain, heads, bn, s):
    w, layer = _layered(w)
    m, k = a.shape
    tile_bytes = 2 * RESNORM_PART * (2 * k * a.dtype.itemsize + 4 * D_MODEL * 4 + 2 * len(heads) * D_MODEL * 2)
    weight_bytes = k * D_MODEL * (w.dtype.itemsize + 2)
    two_parts = s % (2 * RESNORM_PART) == 0 and tile_bytes + weight_bytes <= (3 * VMEM_LIMIT) // 4
    tm, bb, ts, tps = _row_tiling(bn, s, tm=2 * RESNORM_PART if two_parts else RESNORM_PART)
    row = lambda i: (i, 0)
    const = lambda i: (0, 0)
    in_specs = [pl.BlockSpec((tm, k), row),
                pl.BlockSpec((None, k, D_MODEL), lambda i: (layer, 0, 0), pipeline_mode=pl.Buffered(1)),
                pl.BlockSpec((tm, D_MODEL), row), _vec_spec(bb, tps, gate[1], 1),
                pl.BlockSpec((1, D_MODEL), const)]
    args = [a, w, x, gate[0], ngain.reshape(1, D_MODEL)]
    for hg, sh_arr, sh_col, sc_arr, sc_col in heads:
        in_specs += [pl.BlockSpec((1, D_MODEL), const), _vec_spec(bb, tps, sh_col, 1), _vec_spec(bb, tps, sc_col, 1)]
        args += [hg.reshape(1, D_MODEL), sh_arr, sc_arr]
    out_shape = [jax.ShapeDtypeStruct((m, D_MODEL), F32)] + [jax.ShapeDtypeStruct((m, D_MODEL), BF16)] * len(heads)
    out_specs = [pl.BlockSpec((tm, D_MODEL), row)] * (1 + len(heads))
    cache = _weight_cache(w, (k, D_MODEL), m // tm)
    outs = pl.pallas_call(
        functools.partial(_resnorm_body, bb=bb, ts=ts, n_heads=len(heads), cached=bool(cache)),
        grid=(m // tm,),
        in_specs=in_specs,
        out_specs=out_specs,
        out_shape=out_shape,
        scratch_shapes=cache,
        compiler_params=_cparams("arbitrary"),
        name="mm_resnorm",
    )(*args)
    return outs[0], list(outs[1:])


def _rglru_body(u_ref, conv8_ref, hprev_ref, cw_ref, cb_ref, wa_ref, ba_ref, wx_ref, bx_ref, lam_ref,
                y_ref, convo_ref, ho_ref, carry_h, carry_conv, *, bb, tt):
    j = pl.program_id(1)

    @pl.when(j == 0)
    def _():
        carry_h[...] = hprev_ref[...]
        carry_conv[...] = conv8_ref[...]

    u = u_ref[...]
    xb = u[:, :, :D_RNN]
    yb = u[:, :, D_RNN:]
    n_grp = tt // 8
    ext = jnp.concatenate([carry_conv[...].reshape(bb, 1, 8, D_RNN), xb.reshape(bb, n_grp, 8, D_RNN)], axis=1)
    row8 = lax.broadcasted_iota(jnp.int32, (bb, n_grp, 8, D_RNN), 2)
    cw = cw_ref[...]
    xc = cb_ref[...]
    for k in range(CONV_W - 1):
        d = CONV_W - 1 - k
        rolled = pltpu.roll(ext.reshape(bb * (n_grp + 1), 8, D_RNN), d, 1).reshape(bb, n_grp + 1, 8, D_RNN)
        xc = xc + jnp.where(row8 >= d, rolled[:, 1:], rolled[:, :n_grp]) * cw[k:k + 1]
    xc = xc + ext[:, 1:] * cw[CONV_W - 1:CONV_W]
    last8 = xb[:, tt - 8:tt]
    carry_conv[...] = last8
    convo_ref[...] = last8

    xc2 = xc.reshape(bb * tt, D_RNN)
    xcb = xc2.astype(BF16)

    def gate(w_ref, b_ref):
        cols = [_dot(xcb[:, n * LRU_BW:(n + 1) * LRU_BW], w_ref[n]) for n in range(LRU_BLOCKS)]
        return jax.nn.sigmoid(jnp.concatenate(cols, axis=1) + b_ref[...])

    r = gate(wa_ref, ba_ref)
    ig = gate(wx_ref, bx_ref)
    nl = -lam_ref[...]
    z = jnp.exp(-jnp.abs(nl))
    w1 = 1.0 + z
    log1p_z = jnp.where(w1 == 1.0, z, jnp.log(w1) * (z / jnp.where(w1 == 1.0, 1.0, w1 - 1.0)))
    softplus = jnp.maximum(nl, 0.0) + log1p_z
    log_a = -LRU_C * r * softplus
    a = jnp.exp(log_a)
    b = jnp.sqrt(1.0 - a * a) * (ig * xc2)

    a = a.reshape(bb * tt // 8, 8, D_RNN)
    b = b.reshape(bb * tt // 8, 8, D_RNN)
    row8 = lax.broadcasted_iota(jnp.int32, a.shape, 1)
    for d in (1, 2, 4):
        a_s = pltpu.roll(a, d, 1)
        b_s = pltpu.roll(b, d, 1)
        keep = row8 >= d
        b = jnp.where(keep, a * b_s + b, b)
        a = jnp.where(keep, a * a_s, a)

    a3 = a.reshape(bb, tt, D_RNN)
    b3 = b.reshape(bb, tt, D_RNN)
    carry = carry_h[...]
    groups = []
    for g in range(tt // 8):
        hg = a3[:, 8 * g:8 * g + 8] * carry + b3[:, 8 * g:8 * g + 8]
        carry = hg[:, 7:8]
        groups.append(hg)
    hs = groups[0] if len(groups) == 1 else jnp.concatenate(groups, axis=1)
    carry_h[...] = carry
    ho_ref[...] = carry
    y_ref[...] = (hs * _gelu_tanh(yb)).astype(y_ref.dtype)


def rglru(u, conv8, hprev, cw, cb, wa, ba, wx, bx, lam):
    bn, s, _ = u.shape
    if s >= 256:
        bb, tt = 1, 256
    else:
        bb, tt = min(bn, 512 // s), s
    assert s % tt == 0 and bn % bb == 0 and tt % 8 == 0
    vec = lambda a: a.reshape(1, D_RNN)
    c2 = lambda i, j: (0, 0)
    c3 = lambda i, j: (0, 0, 0)
    return pl.pallas_call(
        functools.partial(_rglru_body, bb=bb, tt=tt),
        grid=(bn // bb, s // tt),
        in_specs=[pl.BlockSpec((bb, tt, 2 * D_RNN), lambda i, j: (i, j, 0)),
                  pl.BlockSpec((bb, 8, D_RNN), lambda i, j: (i, 0, 0)),
                  pl.BlockSpec((bb, 1, D_RNN), lambda i, j: (i, 0, 0)),
                  pl.BlockSpec((CONV_W, D_RNN), c2),
                  pl.BlockSpec((1, D_RNN), c2),
                  pl.BlockSpec((LRU_BLOCKS, LRU_BW, LRU_BW), c3),
                  pl.BlockSpec((1, D_RNN), c2),
                  pl.BlockSpec((LRU_BLOCKS, LRU_BW, LRU_BW), c3),
                  pl.BlockSpec((1, D_RNN), c2),
                  pl.BlockSpec((1, D_RNN), c2)],
        out_specs=[pl.BlockSpec((bb, tt, D_RNN), lambda i, j: (i, j, 0)),
                   pl.BlockSpec((bb, 8, D_RNN), lambda i, j: (i, 0, 0)),
                   pl.BlockSpec((bb, 1, D_RNN), lambda i, j: (i, 0, 0))],
        out_shape=[jax.ShapeDtypeStruct((bn, s, D_RNN), BF16),
                   jax.ShapeDtypeStruct((bn, 8, D_RNN), F32),
                   jax.ShapeDtypeStruct((bn, 1, D_RNN), F32)],
        scratch_shapes=[pltpu.VMEM((bb, 1, D_RNN), F32), pltpu.VMEM((bb, 8, D_RNN), F32)],
        compiler_params=_cparams("arbitrary", "arbitrary"),
        name="rglru",
    )(u, conv8, hprev, cw, vec(cb), wa, vec(ba), wx, vec(bx), vec(lam))


def _bias_body(tb_ref, d_ref, o_ref):
    h = pl.program_id(0)
    d = d_ref[...]
    dc = jnp.maximum(d, 0)
    max_exact = N_BUCKETS // 2
    scaled = jnp.log(jnp.maximum(dc, 1).astype(F32) / max_exact) / math.log(MAX_DIST / max_exact)
    large = jnp.minimum(max_exact + (scaled * (N_BUCKETS - max_exact)).astype(jnp.int32), N_BUCKETS - 1)
    bucket = jnp.where(dc < max_exact, dc, large)
    acc = jnp.full(d.shape, tb_ref[0, h], F32)
    for k in range(1, N_BUCKETS):
        acc = jnp.where(bucket == k, tb_ref[k, h], acc)
    o_ref[...] = jnp.where(d < 0, NEG, acc)


def rel_bias_tiles(table, dist):
    rows, cols = dist.shape
    return pl.pallas_call(
        _bias_body,
        grid=(N_HEADS,),
        in_specs=[pl.BlockSpec(memory_space=pltpu.SMEM), pl.BlockSpec((rows, cols), lambda h: (0, 0))],
        out_specs=pl.BlockSpec((None, rows, cols), lambda h: (h, 0, 0)),
        out_shape=jax.ShapeDtypeStruct((N_HEADS, rows, cols), F32),
        compiler_params=_cparams("arbitrary"),
        name="rel_bias_tiles",
    )(table, dist)


def _reorder_emit(sources, stage, o_ref):
    n_col = stage.shape[0]
    row = 0
    for src in sources:
        n = src.shape[0]
        for c in range(n_col):
            stage[c, row:row + n, :] = src[:, c * LANES:(c + 1) * LANES]
        row += n
    n16 = row // CMP_STRIDE
    for r in range(CMP_STRIDE):
        x = jnp.concatenate([stage[c, pl.ds(r, n16, stride=CMP_STRIDE), :] for c in range(n_col)], axis=1)
        o_ref[r] = x.astype(o_ref.dtype)


def _reorder_dense_body(x_ref, o_ref, stage):
    _reorder_emit([x_ref], stage, o_ref)


def _reorder_paged_body(pt_ref, *refs, n_pages):
    del pt_ref
    _reorder_emit(refs[:n_pages], refs[n_pages + 1], refs[n_pages])


def reorder_dense(kv):
    bn, s, _ = kv.shape
    n16 = s // CMP_STRIDE
    return pl.pallas_call(
        _reorder_dense_body,
        grid=(bn,),
        in_specs=[pl.BlockSpec((None, s, KV_PAIR_W), lambda b: (b, 0, 0))],
        out_specs=pl.BlockSpec((None, CMP_STRIDE, n16, KV_PAIR_W), lambda b: (b, 0, 0, 0)),
        out_shape=jax.ShapeDtypeStruct((bn, CMP_STRIDE, n16, KV_PAIR_W), BF16),
        scratch_shapes=[pltpu.VMEM((KV_PAIR_W // LANES, s, LANES), F32)],
        compiler_params=_cparams("arbitrary"),
        name="reorder_dense",
    )(kv)


def reorder_paged(cache, page_table):
    bn, n_pages = page_table.shape
    page = cache.shape[1]
    n16 = n_pages * page // CMP_STRIDE
    page_specs = [pl.BlockSpec((None, page, KV_PAIR_W), functools.partial(lambda b, pt, k: (pt[b * n_pages + k], 0, 0), k=k))
                  for k in range(n_pages)]
    return pl.pallas_call(
        functools.partial(_reorder_paged_body, n_pages=n_pages),
        grid_spec=pltpu.PrefetchScalarGridSpec(
            num_scalar_prefetch=1, grid=(bn,),
            in_specs=page_specs,
            out_specs=pl.BlockSpec((None, CMP_STRIDE, n16, KV_PAIR_W), lambda b, pt: (b, 0, 0, 0)),
            scratch_shapes=[pltpu.VMEM((KV_PAIR_W // LANES, n_pages * page, LANES), F32)]),
        out_shape=jax.ShapeDtypeStruct((bn, CMP_STRIDE, n16, KV_PAIR_W), BF16),
        compiler_params=_cparams("arbitrary"),
        name="reorder_paged",
    )(page_table.reshape(-1), *([cache] * n_pages))


def _compress_body(x_ref, w1_ref, c0_ref, w2_ref, o_ref, acc, *, rows):
    r = pl.program_id(2)

    @pl.when(r == 0)
    def _():
        acc[...] = jnp.zeros_like(acc)

    a = acc[...]
    for k in range(CMP_RSTEP):
        a = a + _dot(x_ref[:, k].reshape(rows, STREAM_W), w1_ref[k])
    acc[...] = a

    @pl.when(r == CMP_STRIDE // CMP_RSTEP - 1)
    def _():
        half = acc.shape[1] // 2
        hidden = _gelu_tanh(acc[:, :half] + pltpu.roll(acc[:, half:], rows - 1, 0) + c0_ref[...])
        o_ref[...] = _dot(hidden.astype(BF16), w2_ref[...]).reshape(o_ref.shape).astype(o_ref.dtype)


def compress(x, w1cat, c0, w2bd, sb):
    bn, _, n16, _ = x.shape
    assert bn % sb == 0
    rows = sb * n16
    nh = w1cat.shape[-1]
    return pl.pallas_call(
        functools.partial(_compress_body, rows=rows),
        grid=(bn // sb, 2, CMP_STRIDE // CMP_RSTEP),
        in_specs=[pl.BlockSpec((sb, CMP_RSTEP, n16, STREAM_W), lambda g, s, r: (g, r, 0, s)),
                  pl.BlockSpec((None, CMP_RSTEP, STREAM_W, nh), lambda g, s, r: (s, r, 0, 0)),
                  pl.BlockSpec((None, 1, nh // 2), lambda g, s, r: (s, 0, 0)),
                  pl.BlockSpec((None, nh // 2, STREAM_W), lambda g, s, r: (s, 0, 0))],
        out_specs=pl.BlockSpec((sb, n16, STREAM_W), lambda g, s, r: (g, 0, s)),
        out_shape=jax.ShapeDtypeStruct((bn, n16, KV_PAIR_W), BF16),
        scratch_shapes=[pltpu.VMEM((rows, nh), F32)],
        compiler_params=_cparams("arbitrary", "arbitrary", "arbitrary"),
        name="compress",
    )(x, w1cat, c0, w2bd)


def _assemble_body(pt_ref, *refs, n_pages, page, s_new, win_len):
    del pt_ref
    pages = refs[:n_pages]
    new_slc_ref, new_win_ref, cwin_ref, slc_ref, win_ref = refs[n_pages:]
    for k in range(n_pages):
        slc_ref[k * page:(k + 1) * page, :] = pages[k][...].astype(slc_ref.dtype)
    pad = jnp.zeros((SLC_LEN - s_new, KV_PAIR_W), F32)
    slc_ref[n_pages * page:n_pages * page + SLC_LEN, :] = jnp.concatenate(
        [new_slc_ref[...], pad], axis=0).astype(slc_ref.dtype)
    win_ref[0:win_len, :] = cwin_ref[...].astype(win_ref.dtype)
    win_ref[win_len:win_len + SLC_LEN, :] = jnp.concatenate([new_win_ref[...], pad], axis=0).astype(win_ref.dtype)


def assemble_sample_kv(cache, page_table, kv_new, cache_win):
    bn, n_pages = page_table.shape
    page = cache.shape[1]
    s_new = kv_new.shape[1]
    win_len = cache_win.shape[1]
    past = n_pages * page
    page_specs = [pl.BlockSpec((None, page, KV_PAIR_W), functools.partial(lambda b, pt, k: (pt[b * n_pages + k], 0, 1), k=k))
                  for k in range(n_pages)]
    return pl.pallas_call(
        functools.partial(_assemble_body, n_pages=n_pages, page=page, s_new=s_new, win_len=win_len),
        grid_spec=pltpu.PrefetchScalarGridSpec(
            num_scalar_prefetch=1, grid=(bn,),
            in_specs=page_specs + [pl.BlockSpec((None, s_new, KV_PAIR_W), lambda b, pt: (b, 0, 1)),
                                   pl.BlockSpec((None, s_new, KV_PAIR_W), lambda b, pt: (b, 0, 2)),
                                   pl.BlockSpec((None, win_len, KV_PAIR_W), lambda b, pt: (b, 0, 0))],
            out_specs=[pl.BlockSpec((None, past + SLC_LEN, KV_PAIR_W), lambda b, pt: (b, 0, 0)),
                       pl.BlockSpec((None, win_len + SLC_LEN, KV_PAIR_W), lambda b, pt: (b, 0, 0))]),
        out_shape=[jax.ShapeDtypeStruct((bn, past + SLC_LEN, KV_PAIR_W), BF16),
                   jax.ShapeDtypeStruct((bn, win_len + SLC_LEN, KV_PAIR_W), BF16)],
        compiler_params=_cparams("arbitrary"),
        name="assemble_sample_kv",
    )(page_table.reshape(-1), *([cache] * n_pages), kv_new, kv_new, cache_win)


def _value_heads(vv):
    own_a = lax.broadcasted_iota(jnp.int32, vv.shape, 1) < HEAD_DIM
    return jnp.where(own_a, vv, 1.0), jnp.where(own_a, 1.0, vv)


def _softmax_update(state, s, vv, rr):
    smax = jnp.max(s, axis=-1, keepdims=True)
    mn = smax if state is None else jnp.maximum(state[0], smax)
    e = jnp.exp2(s - mn).astype(BF16)
    va, vb = _value_heads(vv)
    pv = jnp.concatenate([_dot(e[:rr], va), _dot(e[rr:], vb)], axis=0)
    return mn, (pv if state is None else jnp.exp2(state[0] - mn) * state[1] + pv)


def _softmax_finish(state):
    _, acc = state
    return acc / pltpu.roll(acc, HEAD_DIM, 1)


def _attn_seq(qi, q_ref, gl_ref, kc_ref, vc_ref, ks_ref, vs_ref, kw_ref, vw_ref, augk_ref, augw_ref,
              ovt_ref, cb_ref, nb_ref, wb_ref, augc_ref, ex_ref, o_ref,
              *, qb, n_slc, top_n, pad_s, t0, multi_block, n_streams):
    rr = GQA * qb
    r2 = 2 * rr
    pairs = range(N_PAIRS)
    is_a = lax.broadcasted_iota(jnp.int32, (rr, LANES), 1) < HEAD_DIM
    if multi_block:
        cur = t0 // SLC_LEN + qi
        win_start = pl.multiple_of(qi * qb, SLC_LEN)
    else:
        cur = t0 // SLC_LEN
        win_start = 0

    def cols(ref, p, start, size):
        return ref[pl.ds(start, size), p * LANES:(p + 1) * LANES]

    def init():
        return jnp.full((r2, 1), -jnp.inf, F32), jnp.zeros((r2, LANES), F32)

    wlen = WINDOW + SLC_LEN
    augw = augw_ref[pl.ds(win_start, wlen), :]
    glt = jnp.concatenate([gl_ref[...]] * GQA, axis=0)
    slab = lax.broadcasted_iota(jnp.int32, (rr, LANES), 0) // qb
    gsrc = jnp.concatenate([jnp.where(slab == g, glt, 0.0) for g in range(GQA)], axis=1)
    ghi = gsrc.astype(BF16)
    glo = (gsrc - ghi.astype(F32)).astype(BF16)

    q2, s_win, s_cmp, o_win, gates, o_cmp, imp_t, partial = [], [], [], [], [], [], [], []
    for p in pairs:
        qfull = q_ref[:, p * GQA * LANES:(p + 1) * GQA * LANES]
        qs = jnp.concatenate([qfull[:, g * LANES:(g + 1) * LANES] for g in range(GQA)], axis=0)
        qs = qs * (HEAD_DIM ** -0.5 * LOG2E)
        q2.append(jnp.concatenate([jnp.where(is_a, qs, 0.0), jnp.where(is_a, 0.0, qs)], axis=0).astype(BF16))
        s_cmp.append(_dot_nt(q2[p], kc_ref[:, p * LANES:(p + 1) * LANES]) + cb_ref[p])
    for p in pairs:
        lhs_win = jnp.concatenate([q2[p], augc_ref[p, 1].astype(BF16)], axis=1)
        kk = jnp.concatenate([cols(kw_ref, p, win_start, wlen), augw], axis=1)
        s_win.append(_dot_nt(lhs_win, kk) + wb_ref[p])

    for p in pairs:
        s = s_cmp[p]
        m = jnp.max(s, axis=-1, keepdims=True)
        e = jnp.exp2(s - m)
        pr = e / jnp.sum(e, axis=-1, keepdims=True)
        pr = jnp.where(m > 0.1 * NEG, pr, 0.0)
        o_cmp.append(_dot(pr.astype(BF16), vc_ref[:, p * LANES:(p + 1) * LANES]))

        pcat = jnp.concatenate(
            [jnp.concatenate([pr[x * rr + g * qb:x * rr + (g + 1) * qb] for g in range(GQA)], axis=1)
             for x in range(2)], axis=0)
        if 2 * qb < LANES:
            pcat = jnp.concatenate([pcat, jnp.zeros((LANES - 2 * qb, pcat.shape[1]), F32)], axis=0)
        imp_t.append(_dot_nt(ovt_ref[...], pcat.astype(BF16)))

    def ranked():
        jr = lax.broadcasted_iota(jnp.int32, (N_BLOCK_LANES, LANES), 0)
        forced = (jr == 0) | (jr == cur) | (jr == cur - 1)
        sub8 = lax.broadcasted_iota(jnp.int32, (8, LANES), 0)
        res = []
        for p in pairs:
            v = jnp.where(forced, FORCE, jnp.where((jr > cur) | (jr >= n_slc), NEG, imp_t[p][:N_BLOCK_LANES]))
            groups = [v[8 * r:8 * r + 8] for r in range(N_BLOCK_LANES // 8)]
            ranks = [jnp.zeros((8, LANES), F32) for _ in groups]
            for k in range(n_slc):
                vk = v[k:k + 1, :]
                for r, vr in enumerate(groups):
                    if 8 * r > k:
                        ranks[r] = ranks[r] + jnp.where(vk >= vr, 1.0, 0.0)
                    elif 8 * r + 7 < k:
                        ranks[r] = ranks[r] + jnp.where(vk > vr, 1.0, 0.0)
                    else:
                        ranks[r] = ranks[r] + jnp.where(sub8 > k - 8 * r, jnp.where(vk >= vr, 1.0, 0.0),
                                                        jnp.where(vk > vr, 1.0, 0.0))
            res.append(jnp.where(jnp.concatenate(ranks, axis=0) < top_n, 0.0, 1.0))
        return tuple(res)

    notsel_t = ranked()

    for p in pairs:
        o_win.append(_softmax_finish(_softmax_update(None, s_win[p], cols(vw_ref, p, win_start, wlen), rr)))
        gates.append(jax.nn.sigmoid(_dot(ghi, ex_ref[p]) + _dot(glo, ex_ref[p])))
        partial.append(gates[p][:, :LANES] * jnp.where(is_a, o_cmp[p][:rr], o_cmp[p][rr:])
                       + gates[p][:, 2 * LANES:] * jnp.where(is_a, o_win[p][:rr], o_win[p][rr:]))

    st = pad_s + (cur - 2) * SLC_LEN
    if multi_block:
        st = pl.multiple_of(st, SLC_LEN)
    augk_near = augk_ref[pl.ds(st, 3 * SLC_LEN), :]
    lane2 = lax.broadcasted_iota(jnp.int32, (r2, LANES), 1)
    lhs_far, s_near = [], []
    for p in pairs:
        notsel = jnp.concatenate([notsel_t[p], jnp.zeros((LANES - N_BLOCK_LANES, LANES), F32)], axis=0).T
        ns = jnp.concatenate([notsel[x * qb:(x + 1) * qb] for x in range(2) for _ in range(GQA)], axis=0)
        ns_far = jnp.where((lane2 >= cur - 2) & (lane2 < N_BLOCK_LANES), 1.0, ns)
        lhs_far.append(jnp.concatenate([q2[p], (ns_far + augc_ref[p, 0]).astype(BF16)], axis=1))
        lhs_near = jnp.concatenate([q2[p], (ns + augc_ref[p, 1]).astype(BF16)], axis=1)
        kk = jnp.concatenate([cols(ks_ref, p, st, 3 * SLC_LEN), augk_near], axis=1)
        s_near.append(_dot_nt(lhs_near, kk) + nb_ref[p])
    near = [_softmax_update(None, s_near[p], cols(vs_ref, p, st, 3 * SLC_LEN), rr) for p in pairs]

    def far_start(it, k):
        return pl.multiple_of(pad_s + (it * n_streams + k) * FAR_CHUNK, math.gcd(pad_s, FAR_CHUNK))

    def far_scores(it):
        scores = []
        for k in range(n_streams):
            st = far_start(it, k)
            augk = augk_ref[pl.ds(st, FAR_CHUNK), :]
            for p in pairs:
                kk = jnp.concatenate([cols(ks_ref, p, st, FAR_CHUNK), augk], axis=1)
                scores.append(_dot_nt(lhs_far[p], kk))
        return tuple(scores)

    def far_update(it, states, scores):
        return tuple(_softmax_update(states[k * N_PAIRS + p], scores[k * N_PAIRS + p],
                                     cols(vs_ref, p, far_start(it, k), FAR_CHUNK), rr)
                     for k in range(n_streams) for p in pairs)

    per_iter = n_streams * (FAR_CHUNK // SLC_LEN)
    if multi_block:
        n_iter = (jnp.maximum(cur - 2, 0) + per_iter - 1) // per_iter
        states = tuple(near) + tuple(init() for _ in range((n_streams - 1) * N_PAIRS))
        states = lax.fori_loop(0, n_iter, lambda it, states: far_update(it, states, far_scores(it)), states)
    else:
        states = tuple(near) + (None,) * ((n_streams - 1) * N_PAIRS)
        for it in range((max(cur - 2, 0) + per_iter - 1) // per_iter):
            states = far_update(it, states, far_scores(it))

    outs = []
    for p in pairs:
        mine = tuple(st for st in states[p::N_PAIRS] if st is not None)
        if len(mine) == 1:
            o_slc = _softmax_finish(mine[0])
        else:
            m_all = functools.reduce(jnp.maximum, [m for m, _ in mine])
            o_slc = _softmax_finish((m_all, sum(acc * jnp.exp2(m - m_all) for m, acc in mine)))
        out = partial[p] + gates[p][:, LANES:2 * LANES] * jnp.where(is_a, o_slc[:rr], o_slc[rr:])
        outs += [out[g * qb:(g + 1) * qb] for g in range(GQA)]
    o_ref[...] = jnp.concatenate(outs, axis=1).astype(o_ref.dtype)


def _attn_body(*refs, sbq, **kw):
    qi = pl.program_id(1)
    per_seq, shared = refs[:8], refs[8:16]
    for i in range(sbq):
        _attn_seq(qi, *[r.at[i] for r in per_seq], *shared, refs[16].at[i], **kw)


def _cmp_bias_body(cbe_ref, o_ref, *, nc, step):
    tau = pl.program_id(0) * step
    for p in range(N_PAIRS):
        o_ref[p] = pltpu.roll(cbe_ref[p], tau, 1)[:, nc:]


def cmp_bias_per_block(cbe, nqb, step):
    n_p, rows, nc2 = cbe.shape
    nc = nc2 // 2
    return pl.pallas_call(
        functools.partial(_cmp_bias_body, nc=nc, step=step),
        grid=(nqb,),
        in_specs=[pl.BlockSpec(cbe.shape, lambda i: (0, 0, 0))],
        out_specs=pl.BlockSpec((None, n_p, rows, nc), lambda i: (i, 0, 0, 0)),
        out_shape=jax.ShapeDtypeStruct((nqb, n_p, rows, nc), F32),
        compiler_params=_cparams("arbitrary"),
        name="cmp_bias_per_block",
    )(cbe)


def nsa_attention(u, cmp_kv, slc_kv, slc_cols, win_kv, win_cols, augk, augw, ovt, tiles, ex,
                  *, qb, n_slc, pad_s, t0, sbq):
    bn, s, _ = u.shape
    nqb = s // qb
    nc = cmp_kv.shape[1]
    r2 = 2 * GQA * qb
    cb, nb, wb, augc = tiles
    srows = slc_kv.shape[1]
    wrows = win_kv.shape[1]
    kcol, vcol = slc_cols
    kwcol, vwcol = win_cols
    assert qb == SLC_LEN or nqb == 1
    assert t0 % SLC_LEN == 0 and bn % sbq == 0
    per_chunk = FAR_CHUNK // SLC_LEN
    max_chunks = -(-max(n_slc - 3, 0) // per_chunk)
    n_streams = min(1, max_chunks) if nqb > 1 else max_chunks
    n_streams = max(n_streams, 1)
    assert pad_s + -(-max_chunks // n_streams) * n_streams * FAR_CHUNK <= srows
    body = functools.partial(_attn_body, sbq=sbq, qb=qb, n_slc=n_slc, top_n=min(N_SELECT, n_slc),
                             pad_s=pad_s, t0=t0, multi_block=nqb > 1, n_streams=n_streams)
    whole = lambda a: pl.BlockSpec(a.shape, lambda b, i: (0,) * a.ndim)
    kvw = N_PAIRS * LANES
    nq = N_HEADS * HEAD_DIM
    return pl.pallas_call(
        body,
        grid=(bn // sbq, nqb),
        in_specs=[pl.BlockSpec((sbq, qb, nq), lambda b, i: (b, i, 0)),
                  pl.BlockSpec((sbq, qb, LANES), lambda b, i: (b, i, nq // LANES)),
                  pl.BlockSpec((sbq, nc, kvw), lambda b, i: (b, 0, 0)),
                  pl.BlockSpec((sbq, nc, kvw), lambda b, i: (b, 0, 1)),
                  pl.BlockSpec((sbq, srows, kvw), lambda b, i: (b, 0, kcol)),
                  pl.BlockSpec((sbq, srows, kvw), lambda b, i: (b, 0, vcol)),
                  pl.BlockSpec((sbq, wrows, kvw), lambda b, i: (b, 0, kwcol)),
                  pl.BlockSpec((sbq, wrows, kvw), lambda b, i: (b, 0, vwcol)),
                  whole(augk), whole(augw), whole(ovt),
                  pl.BlockSpec((None, N_PAIRS, r2, nc), lambda b, i: (i, 0, 0, 0)),
                  whole(nb), whole(wb), whole(augc), whole(ex)],
        out_specs=pl.BlockSpec((sbq, qb, nq), lambda b, i: (b, i, 0)),
        out_shape=jax.ShapeDtypeStruct((bn, s, nq), F32),
        compiler_params=_cparams("arbitrary", "arbitrary"),
        name="nsa_attention",
    )(u, u, cmp_kv, cmp_kv, slc_kv, slc_kv, win_kv, win_kv, augk, augw, ovt, cb, nb, wb, augc, ex)


def _head_perm():
    idx = np.empty((N_PAIRS, GQA, 2, HEAD_DIM), np.int32)
    for p in range(N_PAIRS):
        for g in range(GQA):
            for half in range(2):
                head = (2 * p + half) * GQA + g
                idx[p, g, half] = head * HEAD_DIM + np.arange(HEAD_DIM)
    return idx.reshape(-1)


def _gate_expand():
    ex = np.zeros((N_PAIRS, GQA * LANES, 3 * LANES), np.float32)
    for p in range(N_PAIRS):
        for g in range(GQA):
            for half in range(2):
                head = (2 * p + half) * GQA + g
                for br in range(3):
                    ex[p, g * LANES + head * 3 + br, br * LANES + half * HEAD_DIM:br * LANES + (half + 1) * HEAD_DIM] = 1.0
    return ex


def _overlap_t(nc, n_cmp, n_slc):
    i = np.arange(nc)[None, :] * CMP_STRIDE
    j = np.arange(LANES)[:, None] * SLC_LEN
    ov = (i < j + SLC_LEN) & (i + CMP_LEN > j) & (np.arange(nc)[None, :] < n_cmp) & (np.arange(LANES)[:, None] < n_slc)
    return np.tile(ov.astype(np.float32), (1, GQA))


def _aug_keys(pad, n_keys):
    a = np.zeros((pad + n_keys, LANES), np.float32)
    a[:pad, AUG_PAD] = NEG
    k = np.arange(n_keys)
    a[pad + k, k // SLC_LEN] = NEG
    a[pad:, AUG_BIAS_HI] = 1.0
    a[pad:, AUG_BIAS_LO] = 1.0
    return a


def _attn_tiles(rel_bias, qb, t0, nc, nqb):
    rolled = nqb > 1
    i = np.arange(qb)[:, None]
    near = np.arange(3 * SLC_LEN)[None, :]
    d_near = SLC_LEN * (2 - near // SLC_LEN) + i - near % SLC_LEN
    c = np.arange(WINDOW + SLC_LEN)[None, :]
    d_win = i + WINDOW - c
    d_win = np.where(d_win < WINDOW, d_win, -1)
    if rolled:
        e = np.arange(2 * nc)[None, :] - nc
        d_cmp = i - (CMP_LEN - 1) - CMP_STRIDE * e
    else:
        d_cmp = t0 + i - CMP_STRIDE * np.arange(nc)[None, :] - (CMP_LEN - 1)
    widths = [d_near.shape[1], d_win.shape[1], d_cmp.shape[1]]
    padded = [-(-w // LANES) * LANES for w in widths]
    dist = np.full((qb, sum(padded)), -1, np.int32)
    off = 0
    offs = []
    for d, w, pw in zip((d_near, d_win, d_cmp), widths, padded):
        dist[:, off:off + w] = np.clip(d, -1, 4 * MAX_DIST)
        offs.append(off)
        off += pw
    tiles = rel_bias_tiles(rel_bias, jnp.asarray(dist))
    r2 = 2 * GQA * qb
    tiles = tiles.reshape(N_PAIRS, r2, dist.shape[1]) * LOG2E
    nb, wb, cbe = (tiles[..., o:o + w] for o, w in zip(offs, widths))
    if rolled:
        cb = cmp_bias_per_block(cbe, nqb, (t0 + qb) // CMP_STRIDE - t0 // CMP_STRIDE)
    else:
        cb = cbe[None]
    far = rel_bias[N_BUCKETS - 1].reshape(N_PAIRS, 2 * GQA) * LOG2E
    hi = far.astype(BF16).astype(F32)
    lo = far - hi
    rows = jnp.repeat(jnp.stack([hi, lo], axis=-1), qb, axis=1)
    augc = jnp.zeros((N_PAIRS, 2, r2, LANES), F32)
    augc = augc.at[:, 0, :, AUG_BIAS_HI].set(rows[..., 0]).at[:, 0, :, AUG_BIAS_LO].set(rows[..., 1])
    augc = augc.at[:, 1, :, AUG_PAD].set(1.0)
    return cb, nb, wb, augc


def _forward(x, ada, ada_kv, conv_state, h_state, past, w, shared):
    bn, s, _ = x.shape
    m = bn * s
    xf = x.reshape(m, D_MODEL)
    new_conv, new_h = [], []
    h = normmod(xf, w["norm_g"][0, 0], ada[0], 0, 1, bn, s)
    kv = None
    attn_args = None
    for l in range(DEPTH):
        if l < N_A_LAYERS:
            u = mm_plain(h, (w["w_in_a"], l)).reshape(bn, s, 2 * D_RNN)
            conv8 = jnp.pad(conv_state[l], ((0, 0), (8 - (CONV_W - 1), 0), (0, 0)))
            y, conv_o, h_o = rglru(u, conv8, h_state[l][:, None, :], w["conv_w"][l], w["conv_b"][l],
                                   w["w_rg_a"][l], w["b_rg_a"][l], w["w_rg_x"][l], w["b_rg_x"][l], w["lru_lambda"][l])
            new_conv.append(conv_o[:, 8 - (CONV_W - 1):])
            new_h.append(h_o[:, 0])
            a_in = y.reshape(m, D_RNN)
            w_out = (w["w_out_a"], l)
        else:
            lb = l - N_A_LAYERS
            if lb == 0:
                if past is None:
                    kv, kvb = mm_kv_padded(h_kv, w["w_kv"], bn, s, WINDOW)
                else:
                    kv, kvb = mm_plain(h_kv, w["w_kv"]), None
                kv = kv.reshape(bn, s, 6 * N_KV_HEADS * HEAD_DIM)
                attn_args = _prepare_attention(kv, kvb, past, w, shared, bn, s)
            u = mm_plain(h, (w["w_in_b"], lb)).reshape(bn, s, -1)
            a_in = nsa_attention(u, *attn_args[0], **attn_args[1]).reshape(m, N_HEADS * HEAD_DIM)
            w_out = (w["w_out_b"], lb)
        x_new, (h2,) = mm_resnorm(a_in, w_out, xf, (ada[l], 2), w["norm_g"][l, 1],
                                  [(w["norm_g"][l, 2], ada[l], 3, ada[l], 4)], bn, s)
        f = mm_swiglu(h2, (w["w_ffn_in"], l))
        heads = []
        if l + 1 < DEPTH:
            heads.append((w["norm_g"][l + 1, 0], ada[l + 1], 0, ada[l + 1], 1))
        if l + 1 == N_A_LAYERS:
            heads.append((w["norm_kv"], ada_kv, 0, ada_kv, 1))
        xf, hs = mm_resnorm(f, (w["w_ffn_out"], l), x_new, (ada[l], 5), w["norm_g"][l, 3], heads, bn, s)
        if heads:
            h = hs[0]
        if l + 1 == N_A_LAYERS:
            h_kv = hs[1]
    kv_rows = kv[:, :, :4 * N_KV_HEADS * HEAD_DIM].reshape(bn, s, 4, N_KV_HEADS, HEAD_DIM)
    new_win = kv[:, :, 4 * N_KV_HEADS * HEAD_DIM:].reshape(bn, s, 2, N_KV_HEADS, HEAD_DIM)
    if past is None:
        win_state = new_win[:, -min(WINDOW, s):]
    else:
        cache_win = past[2]
        win_state = jnp.concatenate([cache_win, new_win], axis=1)[:, -cache_win.shape[1]:]
    return xf.reshape(bn, s, D_MODEL), kv_rows, win_state, jnp.stack(new_conv), jnp.stack(new_h)


def _prepare_attention(kv, kvb, past, w, shared, bn, s):
    rel_bias = w["rel_bias"]
    if past is None:
        t0, qb, pad = 0, Q_BLOCK, WINDOW
        n_keys = s
        xr = reorder_dense(kv)
        slc_kv = win_kv = kvb
        slc_cols, win_cols = (2, 3), (4, 5)
        augk = jnp.asarray(_aug_keys(pad, n_keys), BF16)
        augw = augk
        sb = bn
    else:
        cache, page_table, cache_win = past
        n_pool, page = cache.shape[:2]
        cache2 = cache.reshape(n_pool, page, -1)
        t0 = page_table.shape[1] * page
        qb, pad = s, 0
        n_keys = t0 + SLC_LEN
        assert cache_win.shape[1] == WINDOW and t0 >= WINDOW
        xr = reorder_paged(cache2, page_table)
        slc_kv, win_kv = assemble_sample_kv(cache2, page_table, kv, cache_win.reshape(bn, WINDOW, -1))
        slc_cols, win_cols = (0, 1), (0, 1)
        augk = jnp.asarray(_aug_keys(0, n_keys), BF16)
        augw = jnp.zeros((win_kv.shape[1], LANES), BF16)
        sb = 8
    n16 = (t0 + s) // CMP_STRIDE
    n_cmp = n16 - CMP_LEN // CMP_STRIDE + 1
    nc = xr.shape[2]
    assert nc >= n_cmp and nc % LANES == 0
    n_slc = -(-(t0 + s) // SLC_LEN)
    assert n_slc <= N_BLOCK_LANES
    cmp_kv = compress(xr, shared["w1cat"], shared["c0"], shared["w2bd"], sb)
    nqb = s // qb
    assert nqb == 1 or t0 == 0
    tiles = _attn_tiles(rel_bias, qb, t0, nc, nqb)
    ovt = jnp.asarray(_overlap_t(nc, n_cmp, n_slc), BF16)
    args = (cmp_kv, slc_kv, slc_cols, win_kv, win_cols, augk, augw, ovt, tiles, shared["ex"])
    return args, dict(qb=qb, n_slc=n_slc, pad_s=pad, t0=t0, sbq=1 if nqb > 1 else min(bn, 2))


def kernel(x_prompt, x_sample, c_prompt, c_sample, cache_kv, cache_win, state_conv, state_h, page_table, w_ada, b_ada, norm_g, w_in_a, conv_w, conv_b, w_rg_a, b_rg_a, w_rg_x, b_rg_x, lru_lambda, w_out_a, w_ada_kv, b_ada_kv, norm_kv, w_kv, cmp_pos, cmp_w1, cmp_w2, w_in_b, w_out_b, rel_bias, w_ffn_in, w_ffn_out):
    bp = x_prompt.shape[0]
    bs = x_sample.shape[0]

    perm = _head_perm()
    n_q = N_HEADS * HEAD_DIM
    w_in_b_p = jnp.concatenate([w_in_b[:, :, :n_q][:, :, perm], w_in_b[:, :, n_q:],
                                jnp.zeros(w_in_b.shape[:2] + (LANES - 3 * N_HEADS,), F32)], axis=-1)
    w = dict(norm_g=norm_g, norm_kv=norm_kv, conv_w=conv_w, conv_b=conv_b, b_rg_a=b_rg_a, b_rg_x=b_rg_x,
             lru_lambda=lru_lambda, rel_bias=rel_bias,
             w_in_a=w_in_a, w_rg_a=w_rg_a.astype(BF16), w_rg_x=w_rg_x.astype(BF16),
             w_out_a=w_out_a, w_kv=w_kv, w_in_b=w_in_b_p,
             w_out_b=w_out_b[:, perm, :], w_ffn_in=w_ffn_in, w_ffn_out=w_ffn_out)

    eye = jnp.eye(N_KV_HEADS, dtype=F32)
    w1r = cmp_w1.reshape(2, CMP_LEN, HEAD_DIM, CMP_HIDDEN)
    w1bd = jnp.einsum("hk,srdj->srhdkj", eye, w1r).reshape(2, CMP_LEN, N_KV_HEADS * HEAD_DIM, N_KV_HEADS * CMP_HIDDEN)
    w2bd = jnp.einsum("hk,sjd->shjkd", eye, cmp_w2).reshape(2, N_KV_HEADS * CMP_HIDDEN, N_KV_HEADS * HEAD_DIM)
    w1cat = jnp.concatenate([w1bd[:, :CMP_STRIDE], w1bd[:, CMP_STRIDE:]], axis=-1).astype(BF16)
    pos_rows = jnp.pad(cmp_pos.reshape(2, 1, CMP_LEN * HEAD_DIM), ((0, 0), (0, 7), (0, 0)))
    c0 = jnp.stack([mm_plain(pos_rows[st], (cmp_w1, st))[:1] for st in range(2)])
    shared = dict(w1cat=w1cat, c0=jnp.tile(c0, (1, 1, N_KV_HEADS)), w2bd=w2bd.astype(BF16),
                  ex=jnp.asarray(_gate_expand(), BF16))

    n_c = bp + bs
    c_all = jnp.concatenate([c_prompt, c_sample, jnp.zeros((-n_c % 8, D_MODEL), F32)], axis=0)
    ada_layers = ada_all_layers(c_all, w_ada, b_ada)
    ada_all = [ada_layers[l] for l in range(DEPTH)]
    ada_kv_all = mm_plain(c_all, w_ada_kv, bias=b_ada_kv, act="silu", tn=2048)
    ada_p = [a[:bp, None, :] for a in ada_all]
    ada_s = [a[bp:n_c, None, :] for a in ada_all]

    zero_conv = jnp.zeros((N_A_LAYERS, bp, CONV_W - 1, D_RNN), F32)
    zero_h = jnp.zeros((N_A_LAYERS, bp, D_RNN), F32)
    y_p, kv_p, win_p, conv_p, h_p = _forward(x_prompt, ada_p, ada_kv_all[:bp, None, :], zero_conv, zero_h, None, w, shared)
    y_s, kv_s, win_s, conv_s, h_s = _forward(x_sample, ada_s, ada_kv_all[bp:n_c, None, :], state_conv, state_h,
                                             (cache_kv, page_table, cache_win), w, shared)
    return (y_p, y_s, kv_p, kv_s, win_p, win_s, conv_p, conv_s, h_p, h_s)
```
